```python
import math
import jax
import jax.numpy as jnp
from jax import lax
import numpy as np

D_MODEL = 2048
BATCH = 8
SEQ = 4096
DEPTH = 4

CHUNK = 64
N_META = 16
Q_BLOCK = 128
EPS = 1e-6

N_MIXERS = 4
GROUP_WIDTH = D_MODEL // N_MIXERS
D_MIX = N_MIXERS * GROUP_WIDTH
HEAD_DIM = 64

SSD_HEADS = GROUP_WIDTH // HEAD_DIM
SSD_GROUPS = 2
SSD_STATE = 128
SSD_CONV = 4
SSD_XBC = GROUP_WIDTH + 2 * SSD_GROUPS * SSD_STATE

POOL_WINDOWS = (2, 4, 8, 16)
POOL_GROUP_DIM = GROUP_WIDTH // len(POOL_WINDOWS)

FOX_HEADS = GROUP_WIDTH // HEAD_DIM

DSA_HEADS = GROUP_WIDTH // HEAD_DIM
DSA_LATENT = 128
IDX_HEADS = 4
IDX_DIM = 64
DSA_TOPK_MAX = 256

FFN_DIM = 256 * (-(-8 * D_MODEL // (3 * 256)))
FFN_CONV = 3

IN_SIZES = (
    GROUP_WIDTH,
    SSD_XBC,
    SSD_HEADS,
    GROUP_WIDTH,
    3 * GROUP_WIDTH,
    FOX_HEADS,
    GROUP_WIDTH,
    DSA_LATENT,
    IDX_HEADS * IDX_DIM,
    IDX_DIM,
    IDX_HEADS,
)
D_IN = sum(IN_SIZES)

kernel_name = 'hybrid_stream_encoder_ssd_pool_fox_dsa'


def rms_norm(x, g):
    xf = x.astype(jnp.float32)
    y = xf * lax.rsqrt(jnp.mean(xf * xf, axis=-1, keepdims=True) + EPS)
    return (y * g.astype(jnp.float32)).astype(x.dtype)


def split_cols(u, sizes):
    offs, acc = [], 0
    for s in sizes[:-1]:
        acc += s
        offs.append(acc)
    return jnp.split(u, offs, axis=-1)


def causal_dwconv(x, w, b):
    k_w = w.shape[0]
    n = x.shape[1]
    xp = jnp.pad(x, ((0, 0), (k_w - 1, 0), (0, 0)))
    return sum((w[k] * xp[:, k:k + n] for k in range(k_w)), b)


def chunk_ids(n):
    p = jnp.arange(n)
    return jnp.where(p < N_META, 0, 1 + (p - N_META) // CHUNK)


def pad_seq(t, n_pad, left=False):
    cfg = [(0, 0)] * t.ndim
    cfg[1] = (n_pad, 0) if left else (0, n_pad)
    return jnp.pad(t, cfg)


def ssd_mixer(z, xbc, dt_raw, conv_w, conv_b, dt_bias, a_log, d_skip, norm_g):
    f32 = jnp.float32
    bsz, n, _ = z.shape
    r = SSD_HEADS // SSD_GROUPS
    xbc = jax.nn.silu(causal_dwconv(xbc, conv_w, conv_b)).astype(f32)
    xs, bm, cm = jnp.split(xbc, [GROUP_WIDTH, GROUP_WIDTH + SSD_GROUPS * SSD_STATE], axis=-1)
    dt = jax.nn.softplus(dt_raw.astype(f32) + dt_bias.astype(f32))
    a_neg = -jnp.exp(a_log.astype(f32))
    pad = (-n) % CHUNK
    xs, bm, cm, dt = (pad_seq(t, pad, left=True) for t in (xs, bm, cm, dt))
    nc = (n + pad) // CHUNK
    x = xs.reshape(bsz, nc, CHUNK, SSD_GROUPS, r, HEAD_DIM)
    bc = bm.reshape(bsz, nc, CHUNK, SSD_GROUPS, SSD_STATE)
    cc = cm.reshape(bsz, nc, CHUNK, SSD_GROUPS, SSD_STATE)
    dtc = dt.reshape(bsz, nc, CHUNK, SSD_GROUPS, r)
    xdt = x * dtc[..., None]
    a = jnp.moveaxis(dtc * a_neg.reshape(SSD_GROUPS, r), 2, -1)
    a_cs = jnp.cumsum(a, axis=-1)
    tril = jnp.tril(jnp.ones((CHUNK, CHUNK), dtype=bool))
    seg = a_cs[..., :, None] - a_cs[..., None, :]
    decay_in = jnp.exp(jnp.where(tril, seg, -jnp.inf))
    cb = jnp.einsum('bclgn,bcsgn->bcgls', cc, bc)
    m = cb[:, :, :, None] * decay_in
    y_diag = jnp.einsum('bcgrls,bcsgrp->bclgrp', m, xdt)
    decay_to_end = jnp.moveaxis(jnp.exp(a_cs[..., -1:] - a_cs), -1, 2)
    chunk_states = jnp.einsum('bclgn,bclgrp->bcgrpn', bc, xdt * decay_to_end[..., None])
    chunk_decay = jnp.exp(a_cs[..., -1])

    def step(h, inp):
        dec, st = inp
        return dec[..., None, None] * h + st, h

    h0 = jnp.zeros((bsz, SSD_GROUPS, r, HEAD_DIM, SSD_STATE), f32)
    _, h_in = lax.scan(step, h0, (jnp.moveaxis(chunk_decay, 1, 0), jnp.moveaxis(chunk_states, 1, 0)))
    h_in = jnp.moveaxis(h_in, 0, 1)
    decay_from_start = jnp.moveaxis(jnp.exp(a_cs), -1, 2)
    y_off = jnp.einsum('bclgn,bcgrpn->bclgrp', cc, h_in) * decay_from_start[..., None]
    y = y_diag + y_off + d_skip.astype(f32).reshape(SSD_GROUPS, r)[:, :, None] * x
    y = y.reshape(bsz, n + pad, GROUP_WIDTH)[:, pad:]
    gz = (y * jax.nn.silu(z.astype(f32))).reshape(bsz, n, SSD_GROUPS, GROUP_WIDTH // SSD_GROUPS)
    gz = gz * lax.rsqrt(jnp.mean(gz * gz, axis=-1, keepdims=True) + EPS)
    return (gz.reshape(bsz, n, GROUP_WIDTH) * norm_g.astype(f32)).astype(z.dtype)


def pool_mixer(u, w, scale):
    f32 = jnp.float32
    bsz, n, _ = u.shape
    uf = u.astype(f32).reshape(bsz, n, len(POOL_WINDOWS), POOL_GROUP_DIM)
    cs = jnp.cumsum(uf, axis=1)
    count = jnp.arange(1, n + 1, dtype=f32)[:, None]
    outs = []
    for gi, win in enumerate(POOL_WINDOWS):
        c = cs[:, :, gi]
        lag = jnp.pad(c, ((0, 0), (win, 0), (0, 0)))[:, :n]
        outs.append((c - lag) / jnp.minimum(count, float(win)) - uf[:, :, gi])
    pooled = jnp.stack(outs, axis=2)
    mixed = jnp.einsum('blgc,gcd->blgd', pooled, w.astype(f32)).reshape(bsz, n, GROUP_WIDTH)
    return (mixed * scale.astype(f32)).astype(u.dtype)


def fox_mixer(q, k, v, f_logit, f_bias):
    f32 = jnp.float32
    bsz, n, _ = q.shape
    n_pad = (-n) % Q_BLOCK
    lp = n + n_pad
    nb = lp // Q_BLOCK
    log_f = jax.nn.log_sigmoid(f_logit.astype(f32) + f_bias.astype(f32))
    fcum = jnp.moveaxis(pad_seq(jnp.cumsum(log_f, axis=1), n_pad), -1, 1)
    q, k, v = (pad_seq(t, n_pad).reshape(bsz, lp, FOX_HEADS, HEAD_DIM) for t in (q, k, v))
    qb = jnp.moveaxis(q.reshape(bsz, nb, Q_BLOCK, FOX_HEADS, HEAD_DIM), 1, 0)
    fb = jnp.moveaxis(fcum.reshape(bsz, FOX_HEADS, nb, Q_BLOCK), 2, 0)
    posb = jnp.arange(lp).reshape(nb, Q_BLOCK)
    key_pos = jnp.arange(lp)
    scale = HEAD_DIM ** -0.5

    def block(args):
        qi, fi, qpos = args
        s = jnp.einsum('bqhd,bkhd->bhqk', qi, k).astype(f32) * scale
        s = s + (fi[..., :, None] - fcum[..., None, :])
        s = jnp.where(key_pos[None, :] <= qpos[:, None], s, -jnp.inf)
        p = jax.nn.softmax(s, axis=-1).astype(v.dtype)
        return jnp.einsum('bhqk,bkhd->bqhd', p, v)

    out = lax.map(block, (qb, fb, posb))
    return jnp.moveaxis(out, 0, 1).reshape(bsz, lp, GROUP_WIDTH)[:, :n]


def dsa_mixer(q, c_kv, q_idx, k_idx, w_idx, kv_norm, w_uk, w_uv, topk):
    f32 = jnp.float32
    bsz, n, _ = q.shape
    n_pad = (-n) % Q_BLOCK
    lp = n + n_pad
    nb = lp // Q_BLOCK
    c = pad_seq(rms_norm(c_kv, kv_norm), n_pad)
    q = q.reshape(bsz, n, DSA_HEADS, HEAD_DIM)
    q_lat = jnp.einsum('blhd,hrd->blhr', q, w_uk) * (HEAD_DIM ** -0.5)
    qi = q_idx.reshape(bsz, n, IDX_HEADS, IDX_DIM)
    wi = w_idx * ((IDX_HEADS ** -0.5) * (IDX_DIM ** -0.5))
    ki = pad_seq(k_idx, n_pad)
    q_lat, qi, wi = (pad_seq(t, n_pad) for t in (q_lat, qi, wi))
    to_blocks = lambda t: jnp.moveaxis(t.reshape((bsz, nb, Q_BLOCK) + t.shape[2:]), 1, 0)
    cid = chunk_ids(lp)
    cidb = cid.reshape(nb, Q_BLOCK)
    bidx = jnp.arange(bsz)[:, None, None]

    def block(args):
        ql, qx, wx, qc = args
        logits = jnp.einsum('bqhd,bkd->bqhk', qx, ki).astype(f32)
        score = jnp.einsum('bqh,bqhk->bqk', wx.astype(f32), jax.nn.relu(logits))
        admissible = cid[None, :] <= qc[:, None]
        score = jnp.where(admissible[None], score, -jnp.inf)
        top_score, idx = lax.top_k(score, topk)
        valid = top_score > -jnp.inf
        c_sel = c[bidx, idx]
        s = jnp.einsum('bqhr,bqkr->bqhk', ql, c_sel).astype(f32)
        s = jnp.where(valid[:, :, None, :], s, -jnp.inf)
        p = jax.nn.softmax(s, axis=-1).astype(c_sel.dtype)
        o_lat = jnp.einsum('bqhk,bqkr->bqhr', p, c_sel)
        return jnp.einsum('bqhr,hrd->bqhd', o_lat, w_uv)

    out = lax.map(block, (to_blocks(q_lat), to_blocks(qi), to_blocks(wi), cidb))
    return jnp.moveaxis(out, 0, 1).reshape(bsz, lp, GROUP_WIDTH)[:, :n]


def hybrid_layer(x, g_mix_pre, g_mix_post, g_ffn_pre, g_ffn_post, w_in,
                 ssd_conv_w, ssd_conv_b, ssd_dt_bias, ssd_a_log, ssd_d, ssd_norm,
                 pool_w, pool_scale, fox_f_bias, dsa_kv_norm, dsa_w_uk, dsa_w_uv,
                 w_out, ffn_w_gate, ffn_w_up, ffn_conv_w, ffn_conv_b, ffn_w_down, topk):
    h = rms_norm(x, g_mix_pre)
    u = h @ w_in
    (z, xbc, dt_raw, pool_in, fox_qkv, f_logit,
     dq, dc, dqi, dki, dwi) = split_cols(u, IN_SIZES)
    y_a = ssd_mixer(z, xbc, dt_raw, ssd_conv_w, ssd_conv_b, ssd_dt_bias, ssd_a_log, ssd_d, ssd_norm)
    y_b = pool_mixer(pool_in, pool_w, pool_scale)
    fq, fk, fv = jnp.split(fox_qkv, 3, axis=-1)
    y_c = fox_mixer(fq, fk, fv, f_logit, fox_f_bias)
    y_d = dsa_mixer(dq, dc, dqi, dki, dwi, dsa_kv_norm, dsa_w_uk, dsa_w_uv, topk)
    mix = jnp.concatenate([y_a, y_b, y_c, y_d], axis=-1) @ w_out
    x = x + rms_norm(mix, g_mix_post)
    h = rms_norm(x, g_ffn_pre)
    gate = causal_dwconv(h @ ffn_w_gate, ffn_conv_w, ffn_conv_b)
    y = (jax.nn.silu(gate) * (h @ ffn_w_up)) @ ffn_w_down
    return x + rms_norm(y, g_ffn_post)


def setup_inputs(seed: int = 0) -> dict:
    key = jax.random.key(seed)
    ks = jax.random.split(key, 32)
    nrm = lambda k, shape, s: jax.random.normal(k, shape, jnp.float32) * s
    gain = lambda k, shape: 1.0 + 0.02 * jax.random.normal(k, shape, jnp.float32)
    r = SSD_HEADS // SSD_GROUPS
    dt0 = jnp.exp(jax.random.uniform(ks[10], (DEPTH, SSD_HEADS), jnp.float32, math.log(1e-3), math.log(1e-1)))
    return {
        'x': nrm(ks[0], (BATCH, SEQ, D_MODEL), 1.0),
        'meta_tokens': nrm(ks[1], (N_META, D_MODEL), 1.0),
        'norm_mix_pre': gain(ks[2], (DEPTH, D_MODEL)),
        'norm_mix_post': gain(ks[3], (DEPTH, D_MODEL)),
        'norm_ffn_pre': gain(ks[4], (DEPTH, D_MODEL)),
        'norm_ffn_post': gain(ks[5], (DEPTH, D_MODEL)),
        'w_in': nrm(ks[6], (DEPTH, D_MODEL, D_IN), D_MODEL ** -0.5),
        'ssd_conv_w': nrm(ks[7], (DEPTH, SSD_CONV, SSD_XBC), SSD_CONV ** -0.5),
        'ssd_conv_b': nrm(ks[8], (DEPTH, SSD_XBC), 0.02),
        'ssd_dt_bias': dt0 + jnp.log(-jnp.expm1(-dt0)),
        'ssd_a_log': jnp.log(jax.random.uniform(ks[11], (DEPTH, SSD_HEADS), jnp.float32, 1.0, 16.0)),
        'ssd_d': gain(ks[12], (DEPTH, SSD_HEADS)),
        'ssd_norm': gain(ks[13], (DEPTH, GROUP_WIDTH)),
        'pool_w': nrm(ks[14], (DEPTH, len(POOL_WINDOWS), POOL_GROUP_DIM, POOL_GROUP_DIM), POOL_GROUP_DIM ** -0.5),
        'pool_scale': gain(ks[15], (DEPTH, GROUP_WIDTH)),
        'fox_f_bias': jax.random.uniform(ks[16], (DEPTH, FOX_HEADS), jnp.float32, 1.0, 5.0),
        'dsa_kv_norm': gain(ks[17], (DEPTH, DSA_LATENT)),
        'dsa_w_uk': nrm(ks[18], (DEPTH, DSA_HEADS, DSA_LATENT, HEAD_DIM), DSA_LATENT ** -0.5),
        'dsa_w_uv': nrm(ks[19], (DEPTH, DSA_HEADS, DSA_LATENT, HEAD_DIM), DSA_LATENT ** -0.5),
        'w_out': nrm(ks[20], (DEPTH, D_MIX, D_MODEL), D_MIX ** -0.5),
        'ffn_w_gate': nrm(ks[21], (DEPTH, D_MODEL, FFN_DIM), D_MODEL ** -0.5),
        'ffn_w_up': nrm(ks[22], (DEPTH, D_MODEL, FFN_DIM), D_MODEL ** -0.5),
        'ffn_conv_w': nrm(ks[23], (DEPTH, FFN_CONV, FFN_DIM), FFN_CONV ** -0.5),
        'ffn_conv_b': nrm(ks[24], (DEPTH, FFN_DIM), 0.02),
        'ffn_w_down': nrm(ks[25], (DEPTH, FFN_DIM, D_MODEL), FFN_DIM ** -0.5),
    }


def reference(x, meta_tokens, norm_mix_pre, norm_mix_post, norm_ffn_pre, norm_ffn_post, w_in,
              ssd_conv_w, ssd_conv_b, ssd_dt_bias, ssd_a_log, ssd_d, ssd_norm,
              pool_w, pool_scale, fox_f_bias, dsa_kv_norm, dsa_w_uk, dsa_w_uv,
              w_out, ffn_w_gate, ffn_w_up, ffn_conv_w, ffn_conv_b, ffn_w_down):
    topk = min(DSA_TOPK_MAX, SEQ // 4)
    meta = jnp.broadcast_to(meta_tokens.astype(x.dtype)[None], (x.shape[0], N_META, D_MODEL))
    h = jnp.concatenate([meta, x], axis=1)
    for i in range(DEPTH):
        h = hybrid_layer(h, norm_mix_pre[i], norm_mix_post[i], norm_ffn_pre[i], norm_ffn_post[i], w_in[i],
                         ssd_conv_w[i], ssd_conv_b[i], ssd_dt_bias[i], ssd_a_log[i], ssd_d[i], ssd_norm[i],
                         pool_w[i], pool_scale[i], fox_f_bias[i], dsa_kv_norm[i], dsa_w_uk[i], dsa_w_uv[i],
                         w_out[i], ffn_w_gate[i], ffn_w_up[i], ffn_conv_w[i], ffn_conv_b[i], ffn_w_down[i], topk)
    return h[:, N_META:]
```

```python
import functools

import jax
import jax.numpy as jnp
from jax import lax
from jax.experimental import pallas as pl
from jax.experimental.pallas import tpu as pltpu

F32 = jnp.float32
BF16 = jnp.bfloat16
I32 = jnp.int32

EPS = 1e-6
N_META = 16
CHUNK = 64
HEAD_DIM = 64
GROUP_WIDTH = 512
N_HEADS = 8
SSD_GROUPS = 2
SSD_STATE = 128
SSD_CONV = 4
SSD_XBC = GROUP_WIDTH + 2 * SSD_GROUPS * SSD_STATE
POOL_WINDOWS = (2, 4, 8, 16)
DSA_LATENT = 128
IDX_HEADS = 4
IDX_DIM = 64
DSA_TOPK_MAX = 256
FFN_CONV = 3

LANES = 128
SEQ_TILE = 128
INT_MIN = -(2 ** 31)
NEG_BIG = -1e30

A_COLS = 2560
B_COLS = 2560
SM_DT = 0
SM_F = 8
SM_W = 16
SM_A = 24


def _vmem(mb):
    return int(mb * 1024 * 1024)


def _softplus_parts(x):
    t = jnp.log1p(jnp.exp(-jnp.abs(x)))
    return jnp.maximum(x, 0.0) + t, jnp.minimum(x, 0.0) - t


def _silu(x):
    return x / (1.0 + jnp.exp(-x))


def _inproj_body(x_ref, g_ref, w_ref, oa_ref, ob_ref, xn_ref, *, n_a):
    j = pl.program_id(1)

    @pl.when(j == 0)
    def _():
        x = x_ref[...]
        ms = jnp.mean(x * x, axis=-1, keepdims=True)
        xn_ref[...] = (x * lax.rsqrt(ms + EPS) * g_ref[...]).astype(BF16)

    acc = jnp.dot(xn_ref[...], w_ref[...], preferred_element_type=F32)

    @pl.when(j < n_a)
    def _():
        oa_ref[...] = acc

    @pl.when(j >= n_a)
    def _():
        ob_ref[...] = acc.astype(BF16)


def _in_proj(x2d, g, w_perm, *, tm, tn):
    m, d = x2d.shape
    n_a, n_b = A_COLS // tn, B_COLS // tn
    return pl.pallas_call(
        functools.partial(_inproj_body, n_a=n_a),
        grid=(m // tm, n_a + n_b),
        in_specs=[
            pl.BlockSpec((tm, d), lambda i, j: (i, 0)),
            pl.BlockSpec((1, d), lambda i, j: (0, 0)),
            pl.BlockSpec((d, tn), lambda i, j: (0, j)),
        ],
        out_specs=[
            pl.BlockSpec((tm, tn), lambda i, j: (i, jnp.minimum(j, n_a - 1))),
            pl.BlockSpec((tm, tn), lambda i, j: (i, jnp.maximum(j - n_a, 0))),
        ],
        out_shape=[
            jax.ShapeDtypeStruct((m, A_COLS), F32),
            jax.ShapeDtypeStruct((m, B_COLS), BF16),
        ],
        scratch_shapes=[pltpu.VMEM((tm, d), BF16)],
        compiler_params=pltpu.CompilerParams(
            dimension_semantics=("parallel", "arbitrary"), vmem_limit_bytes=_vmem(48)),
        name="in_proj",
    )(x2d, g, w_perm)


def _prep_body(sm_ref, dc_ref, bias_ref, mul_ref, kvg_ref, col_ref, row_ref, c_ref, carry_ref):
    t = pl.program_id(1)

    @pl.when(t == 0)
    def _():
        carry_ref[...] = jnp.zeros_like(carry_ref)

    tt = sm_ref.shape[1]
    s = sm_ref[0]
    lane = lax.broadcasted_iota(I32, (tt, LANES), 1)
    is_dt = lane < SM_F
    is_f = (lane >= SM_F) & (lane < SM_W)
    is_a = (lane >= SM_A) & (lane < SM_A + N_HEADS)
    sp, ls = _softplus_parts(s + bias_ref[...])
    v = jnp.where(is_dt, sp, jnp.where(is_f, ls, jnp.where(is_a, sp, s) * mul_ref[...]))
    ri = lax.broadcasted_iota(I32, (tt, tt), 0)
    ci = lax.broadcasted_iota(I32, (tt, tt), 1)
    tril = jnp.where(ci <= ri, 1.0, 0.0).astype(F32)
    local = jnp.dot(tril, v, precision=lax.Precision.HIGHEST, preferred_element_type=F32)
    out = jnp.where(is_f, local + carry_ref[...], jnp.where(is_a, local, v))
    col_ref[0] = out
    row_ref[0, 0] = out.T
    carry_ref[...] = jnp.where(is_f[0:1], out[tt - 1:tt, :], 0.0)

    dc = dc_ref[0]
    ms = jnp.mean(dc * dc, axis=-1, keepdims=True)
    c_ref[0] = (dc * lax.rsqrt(ms + EPS) * kvg_ref[...]).astype(BF16)


def _prep(ua, bias_vec, mul_vec, kv_g):
    b, lp, _ = ua.shape
    tt = SEQ_TILE
    nblk = lp // tt
    return pl.pallas_call(
        _prep_body,
        grid=(b, nblk),
        in_specs=[
            pl.BlockSpec((1, tt, LANES), lambda i, t: (i, t, 17)),
            pl.BlockSpec((1, tt, LANES), lambda i, t: (i, t, 16)),
            pl.BlockSpec((1, LANES), lambda i, t: (0, 0)),
            pl.BlockSpec((1, LANES), lambda i, t: (0, 0)),
            pl.BlockSpec((1, LANES), lambda i, t: (0, 0)),
        ],
        out_specs=[
            pl.BlockSpec((1, tt, LANES), lambda i, t: (i, t, 0)),
            pl.BlockSpec((1, 1, LANES, tt), lambda i, t: (i, t, 0, 0)),
            pl.BlockSpec((1, tt, LANES), lambda i, t: (i, t, 0)),
        ],
        out_shape=[
            jax.ShapeDtypeStruct((b, lp, LANES), F32),
            jax.ShapeDtypeStruct((b, nblk, LANES, tt), F32),
            jax.ShapeDtypeStruct((b, lp, DSA_LATENT), BF16),
        ],
        scratch_shapes=[pltpu.VMEM((1, LANES), F32)],
        compiler_params=pltpu.CompilerParams(dimension_semantics=("parallel", "arbitrary")),
        name="prep",
    )(ua, ua, bias_vec, mul_vec, kv_g)


def _expand_heads(colv, base, lo_half):
    parts = []
    for p in range(N_HEADS // 2):
        a = colv[:, base + 2 * p:base + 2 * p + 1]
        b = colv[:, base + 2 * p + 1:base + 2 * p + 2]
        parts.append(jnp.where(lo_half, a, b))
    return jnp.concatenate(parts, axis=1)


def _ssd_body(xbc_ref, z_ref, col_ref, row_ref, cw_ref, cb_ref, dsk_ref, ng_ref, y_ref,
              xpad_ref, st_ref):
    c = pl.program_id(1)
    ll = xbc_ref.shape[1]
    gw = GROUP_WIDTH
    ns = SSD_STATE
    hpg = N_HEADS // SSD_GROUPS
    gcols = hpg * HEAD_DIM

    @pl.when(c == 0)
    def _():
        xpad_ref[0:8, :] = jnp.zeros((8, SSD_XBC), F32)
        st_ref[...] = jnp.zeros_like(st_ref)

    x = xbc_ref[0]
    xpad_ref[8:8 + ll, :] = x
    conv = (cb_ref[...]
            + cw_ref[0:1, :] * xpad_ref[5:5 + ll, :]
            + cw_ref[1:2, :] * xpad_ref[6:6 + ll, :]
            + cw_ref[2:3, :] * xpad_ref[7:7 + ll, :]
            + cw_ref[3:4, :] * x)
    xpad_ref[0:8, :] = x[ll - 8:ll, :]
    act = _silu(conv)
    xs = act[:, 0:gw]
    bm = act[:, gw:gw + SSD_GROUPS * ns]
    cm = act[:, gw + SSD_GROUPS * ns:]

    colv = col_ref[0]
    rowv = row_ref[0, 0]
    lo_half = lax.broadcasted_iota(I32, (ll, LANES), 1) < HEAD_DIM
    dt_full = _expand_heads(colv, SM_DT, lo_half)
    acs_full = _expand_heads(colv, SM_A, lo_half)
    acs_last = acs_full[ll - 1:ll, :]
    dte_full = jnp.exp(acs_last - acs_full)
    dfs_full = jnp.exp(acs_full)
    xdt = xs * dt_full
    xdt_b = xdt.astype(BF16)
    xdte_b = (xdt * dte_full).astype(BF16)
    cm_b = cm.astype(BF16)
    ri = lax.broadcasted_iota(I32, (ll, ll), 0)
    ci = lax.broadcasted_iota(I32, (ll, ll), 1)
    tril = ci <= ri

    ys = []
    for g in range(SSD_GROUPS):
        bg = bm[:, ns * g:ns * (g + 1)]
        bg_b = bg.astype(BF16)
        bgt_b = bg.T.astype(BF16)
        cg_b = cm_b[:, ns * g:ns * (g + 1)]
        cb = lax.dot_general(cg_b, bg_b, (((1,), (1,)), ((), ())), preferred_element_type=F32)
        sg = st_ref[g]
        yoff = (jnp.dot(cg_b, sg.astype(BF16), preferred_element_type=F32)
                * dfs_full[:, gcols * g:gcols * (g + 1)])
        parts = []
        for pr in range(hpg // 2):
            xpair = xdt_b[:, gcols * g + LANES * pr:gcols * g + LANES * (pr + 1)]
            res = []
            for hh in range(2):
                h = hpg * g + 2 * pr + hh
                seg = colv[:, SM_A + h:SM_A + h + 1] - rowv[SM_A + h:SM_A + h + 1, :]
                lm = jnp.exp(jnp.where(tril, seg, -jnp.inf))
                res.append(jnp.dot((cb * lm).astype(BF16), xpair, preferred_element_type=F32))
            parts.append(jnp.where(lo_half, res[0], res[1]))
        ydiag = jnp.concatenate(parts, axis=1)
        decay = jnp.exp(acs_last[:, gcols * g:gcols * (g + 1)])
        st_ref[g] = decay * sg + jnp.dot(bgt_b, xdte_b[:, gcols * g:gcols * (g + 1)],
                                         preferred_element_type=F32)
        ys.append(ydiag + yoff)

    y = jnp.concatenate(ys, axis=1) + dsk_ref[...] * xs
    gz = y * _silu(z_ref[0])
    outs = []
    for g in range(SSD_GROUPS):
        gg = gz[:, gcols * g:gcols * (g + 1)]
        outs.append(gg * lax.rsqrt(jnp.mean(gg * gg, axis=-1, keepdims=True) + EPS))
    y_ref[0] = (jnp.concatenate(outs, axis=1) * ng_ref[...]).astype(BF16)


def _ssd(ua, col, row, conv_w, conv_b, dskip_full, norm_g):
    b, lp, _ = ua.shape
    ll = SEQ_TILE
    nblk = lp // ll
    hpg = N_HEADS // SSD_GROUPS
    return pl.pallas_call(
        _ssd_body,
        grid=(b, nblk),
        in_specs=[
            pl.BlockSpec((1, ll, SSD_XBC), lambda i, c: (i, c, 0)),
            pl.BlockSpec((1, ll, GROUP_WIDTH), lambda i, c: (i, c, 2)),
            pl.BlockSpec((1, ll, LANES), lambda i, c: (i, c, 0)),
            pl.BlockSpec((1, 1, LANES, ll), lambda i, c: (i, c, 0, 0)),
            pl.BlockSpec((SSD_CONV, SSD_XBC), lambda i, c: (0, 0)),
            pl.BlockSpec((1, SSD_XBC), lambda i, c: (0, 0)),
            pl.BlockSpec((1, GROUP_WIDTH), lambda i, c: (0, 0)),
            pl.BlockSpec((1, GROUP_WIDTH), lambda i, c: (0, 0)),
        ],
        out_specs=pl.BlockSpec((1, ll, GROUP_WIDTH), lambda i, c: (i, c, 0)),
        out_shape=jax.ShapeDtypeStruct((b, lp, GROUP_WIDTH), BF16),
        scratch_shapes=[
            pltpu.VMEM((8 + ll, SSD_XBC), F32),
            pltpu.VMEM((SSD_GROUPS, SSD_STATE, hpg * HEAD_DIM), F32),
        ],
        compiler_params=pltpu.CompilerParams(dimension_semantics=("parallel", "arbitrary")),
        name="ssd",
    )(ua, ua, col, row, conv_w, conv_b, dskip_full, norm_g)


def _pool_body(u_ref, w_ref, sc_ref, y_ref, buf_ref):
    t = pl.program_id(1)
    tt = u_ref.shape[1]
    hist = max(POOL_WINDOWS)
    gd = GROUP_WIDTH // len(POOL_WINDOWS)

    @pl.when(t == 0)
    def _():
        buf_ref[0:hist, :] = jnp.zeros((hist, GROUP_WIDTH), F32)

    u = u_ref[0]
    buf_ref[hist:hist + tt, :] = u
    count = (t * tt + 1 + lax.broadcasted_iota(I32, (tt, 1), 0)).astype(F32)
    outs = []
    for gi, win in enumerate(POOL_WINDOWS):
        ug = u[:, gd * gi:gd * (gi + 1)]
        acc = ug
        for k in range(1, win):
            acc = acc + buf_ref[hist - k:hist - k + tt, gd * gi:gd * (gi + 1)]
        pooled = acc / jnp.minimum(count, float(win)) - ug
        outs.append(jnp.dot(pooled.astype(BF16), w_ref[gi], preferred_element_type=F32))
    y_ref[0] = (jnp.concatenate(outs, axis=1) * sc_ref[...]).astype(BF16)
    buf_ref[0:hist, :] = u[tt - hist:tt, :]


def _pool(ua, w, scale):
    b, lp, _ = ua.shape
    tt = SEQ_TILE
    ng = len(POOL_WINDOWS)
    gd = GROUP_WIDTH // ng
    return pl.pallas_call(
        _pool_body,
        grid=(b, lp // tt),
        in_specs=[
            pl.BlockSpec((1, tt, GROUP_WIDTH), lambda i, t: (i, t, 3)),
            pl.BlockSpec((ng, gd, gd), lambda i, t: (0, 0, 0)),
            pl.BlockSpec((1, GROUP_WIDTH), lambda i, t: (0, 0)),
        ],
        out_specs=pl.BlockSpec((1, tt, GROUP_WIDTH), lambda i, t: (i, t, 0)),
        out_shape=jax.ShapeDtypeStruct((b, lp, GROUP_WIDTH), BF16),
        scratch_shapes=[pltpu.VMEM((max(POOL_WINDOWS) + tt, GROUP_WIDTH), F32)],
        compiler_params=pltpu.CompilerParams(dimension_semantics=("parallel", "arbitrary")),
        name="pool",
    )(ua, w, scale)


def _fox_body(q_ref, k_ref, v_ref, fc_ref, fr_ref, o_ref):
    qi = pl.program_id(1)
    tq = q_ref.shape[1]
    tk = tq
    lo_half = lax.broadcasted_iota(I32, (tq, LANES), 1) < HEAD_DIM
    causal = (lax.broadcasted_iota(I32, (tq, tk), 1) <= lax.broadcasted_iota(I32, (tq, tk), 0))
    nt = (((1,), (1,)), ((), ()))
    zero_b = jnp.zeros((tq, LANES), BF16)

    for pr in range(N_HEADS // 2):
        cols = slice(LANES * pr, LANES * (pr + 1))
        q2 = q_ref[0, :, cols] * jnp.asarray(HEAD_DIM ** -0.5, BF16)
        qm = (jnp.where(lo_half, q2, zero_b), jnp.where(lo_half, zero_b, q2))
        fq = tuple(fc_ref[0, :, SM_F + 2 * pr + hh:SM_F + 2 * pr + hh + 1] for hh in range(2))

        def step(kb, carry, masked, cols=cols, qm=qm, fq=fq, pr=pr):
            ks = pl.multiple_of(kb * tk, tk)
            k2 = k_ref[0, pl.ds(ks, tk), cols]
            v2 = v_ref[0, pl.ds(ks, tk), cols]
            fk = fr_ref[0, kb]
            new = []
            for hh in range(2):
                m, l, a = carry[3 * hh:3 * hh + 3]
                s = lax.dot_general(qm[hh], k2, nt, preferred_element_type=F32)
                s = s + (fq[hh] - fk[2 * pr + hh:2 * pr + hh + 1, :])
                if masked:
                    s = jnp.where(causal, s, -jnp.inf)
                mn = jnp.maximum(m, jnp.max(s, axis=1, keepdims=True))
                p = jnp.exp(s - mn)
                al = jnp.exp(m - mn)
                l = al * l + jnp.sum(p, axis=1, keepdims=True)
                a = al * a + jnp.dot(p.astype(BF16), v2, preferred_element_type=F32)
                new += [mn, l, a]
            return tuple(new)

        init = (jnp.full((tq, 1), NEG_BIG, F32), jnp.zeros((tq, 1), F32), jnp.zeros((tq, LANES), F32)) * 2
        carry = lax.fori_loop(0, qi, functools.partial(step, masked=False), init)
        m0, l0, a0, m1, l1, a1 = step(qi, carry, True)
        o_ref[0, :, cols] = jnp.where(lo_half, a0 / l0, a1 / l1).astype(BF16)


def _fox(ub, col, row):
    b, lp, _ = ub.shape
    tq = SEQ_TILE
    nblk = lp // tq
    return pl.pallas_call(
        _fox_body,
        grid=(b, nblk),
        in_specs=[
            pl.BlockSpec((1, tq, GROUP_WIDTH), lambda i, q: (i, q, 0)),
            pl.BlockSpec((1, lp, GROUP_WIDTH), lambda i, q: (i, 0, 1)),
            pl.BlockSpec((1, lp, GROUP_WIDTH), lambda i, q: (i, 0, 2)),
            pl.BlockSpec((1, tq, LANES), lambda i, q: (i, q, 0)),
            pl.BlockSpec((1, nblk, N_HEADS, tq), lambda i, q: (i, 0, SM_F // N_HEADS, 0)),
        ],
        out_specs=pl.BlockSpec((1, tq, GROUP_WIDTH), lambda i, q: (i, q, 0)),
        out_shape=jax.ShapeDtypeStruct((b, lp, GROUP_WIDTH), BF16),
        compiler_params=pltpu.CompilerParams(
            dimension_semantics=("parallel", "arbitrary"), vmem_limit_bytes=_vmem(40)),
        name="fox",
    )(ub, ub, ub, col, row)


def _dsa_body(dq_ref, dqi_ref, ki_ref, c_ref, wcol_ref, wuk_ref, wuv_ref, o_ref,
              key_ref, bias_ref, qlat_ref, *, topk):
    qi = pl.program_id(1)
    tq = dq_ref.shape[1]
    nblk = key_ref.shape[0]
    nkb = jnp.minimum(qi + 2, nblk)
    nt = (((1,), (1,)), ((), ()))
    lane = lax.broadcasted_iota(I32, (tq, LANES), 1)
    lo_half = lane < IDX_DIM
    zero_b = jnp.zeros((tq, LANES), BF16)
    shift = CHUNK - N_META
    lg2 = CHUNK.bit_length() - 1
    qcid = (qi * tq + lax.broadcasted_iota(I32, (tq, LANES), 0) + shift) >> lg2
    kf = float(topk)

    wcol = wcol_ref[0]
    qms, wis = [], []
    for h in range(IDX_HEADS):
        q2 = dqi_ref[0, :, LANES * (h // 2):LANES * (h // 2 + 1)]
        qms.append(jnp.where(lo_half, q2, zero_b) if h % 2 == 0 else jnp.where(lo_half, zero_b, q2))
        wis.append(wcol[:, SM_W + h:SM_W + h + 1])

    def score_step(kb, _):
        kt = ki_ref[0, kb]
        sc = jnp.zeros((tq, LANES), F32)
        for h in range(IDX_HEADS):
            lg = lax.dot_general(qms[h], kt, nt, preferred_element_type=F32)
            sc = sc + wis[h] * jnp.maximum(lg, 0.0)
        bits = pltpu.bitcast(sc, I32)
        bits = jnp.where(bits == INT_MIN, 0, bits)
        skey = bits ^ ((bits >> 31) & 0x7FFFFFFF)
        kcid = (kb * LANES + lane + shift) >> lg2
        key_ref[kb] = jnp.where(kcid <= qcid, skey, INT_MIN)
        return 0

    lax.fori_loop(0, nkb, score_step, 0)

    def bit_step(i, u):
        uc = u | lax.shift_left(jnp.int32(1), 31 - i)
        sc = uc ^ INT_MIN

        def cnt_step(kb, acc):
            return acc + jnp.where(key_ref[kb] >= sc, 1.0, 0.0)

        acc = lax.fori_loop(0, nkb, cnt_step, jnp.zeros((tq, LANES), F32))
        cnt = jnp.sum(acc, axis=1, keepdims=True)
        return jnp.where(cnt >= kf, uc, u)

    thr = lax.fori_loop(0, 32, bit_step, jnp.zeros((tq, LANES), I32)) ^ INT_MIN

    def gt_step(kb, acc):
        return acc + jnp.where(key_ref[kb] > thr, 1.0, 0.0)

    ngt = jnp.sum(lax.fori_loop(0, nkb, gt_step, jnp.zeros((tq, LANES), F32)), axis=1, keepdims=True)
    room = kf - ngt
    incl = jnp.where(lax.broadcasted_iota(I32, (LANES, LANES), 0) <= lax.broadcasted_iota(I32, (LANES, LANES), 1),
                     1.0, 0.0).astype(BF16)

    def mask_step(kb, seen):
        key = key_ref[kb]
        eqf = jnp.where(key == thr, 1.0, 0.0)
        rank = jnp.dot(eqf.astype(BF16), incl, preferred_element_type=F32) + seen
        tie = jnp.where(rank <= room, eqf, 0.0)
        sel = jnp.where(key > thr, 1.0, tie)
        sel = jnp.where(key == INT_MIN, 0.0, sel)
        bias_ref[kb] = jnp.where(sel > 0.5, 0.0, -jnp.inf)
        return seen + jnp.sum(eqf, axis=1, keepdims=True)

    lax.fori_loop(0, nkb, mask_step, jnp.zeros((tq, 1), F32))

    for h in range(N_HEADS):
        dq2 = dq_ref[0, :, LANES * (h // 2):LANES * (h // 2 + 1)]
        ql = jnp.dot(dq2, wuk_ref[h], preferred_element_type=F32) * (HEAD_DIM ** -0.5)
        qlat_ref[h] = ql.astype(BF16)

    for pr in range(N_HEADS // 2):
        out = jnp.zeros((tq, LANES), F32)
        for hh in range(2):
            h = 2 * pr + hh
            ql = qlat_ref[h]

            def att_step(kb, carry, ql=ql):
                m, l, a = carry
                ct = c_ref[0, kb]
                s = lax.dot_general(ql, ct, nt, preferred_element_type=F32) + bias_ref[kb]
                mn = jnp.maximum(m, jnp.max(s, axis=1, keepdims=True))
                p = jnp.exp(s - mn)
                al = jnp.exp(m - mn)
                l = al * l + jnp.sum(p, axis=1, keepdims=True)
                a = al * a + jnp.dot(p.astype(BF16), ct, preferred_element_type=F32)
                return mn, l, a

            init = (jnp.full((tq, 1), NEG_BIG, F32), jnp.zeros((tq, 1), F32), jnp.zeros((tq, LANES), F32))
            _, l, a = lax.fori_loop(0, nkb, att_step, init)
            out = out + jnp.dot((a / l).astype(BF16), wuv_ref[h], preferred_element_type=F32)
        o_ref[0, :, LANES * pr:LANES * (pr + 1)] = out.astype(BF16)


def _dsa(ub, ki4, c4, col, wuk_pad, wuv_pad, *, topk):
    b, lp, _ = ub.shape
    tq = SEQ_TILE
    nblk = lp // tq
    return pl.pallas_call(
        functools.partial(_dsa_body, topk=topk),
        grid=(b, nblk),
        in_specs=[
            pl.BlockSpec((1, tq, GROUP_WIDTH), lambda i, q: (i, q, 3)),
            pl.BlockSpec((1, tq, IDX_HEADS * IDX_DIM), lambda i, q: (i, q, 8)),
            pl.BlockSpec((1, nblk, tq, LANES), lambda i, q: (i, 0, 0, 18)),
            pl.BlockSpec((1, nblk, tq, DSA_LATENT), lambda i, q: (i, 0, 0, 0)),
            pl.BlockSpec((1, tq, LANES), lambda i, q: (i, q, 0)),
            pl.BlockSpec((N_HEADS, LANES, DSA_LATENT), lambda i, q: (0, 0, 0)),
            pl.BlockSpec((N_HEADS, DSA_LATENT, LANES), lambda i, q: (0, 0, 0)),
        ],
        out_specs=pl.BlockSpec((1, tq, GROUP_WIDTH), lambda i, q: (i, q, 0)),
        out_shape=jax.ShapeDtypeStruct((b, lp, GROUP_WIDTH), BF16),
        scratch_shapes=[
            pltpu.VMEM((nblk, tq, LANES), I32),
            pltpu.VMEM((nblk, tq, LANES), F32),
            pltpu.VMEM((N_HEADS, tq, DSA_LATENT), BF16),
        ],
        compiler_params=pltpu.CompilerParams(
            dimension_semantics=("parallel", "arbitrary"), vmem_limit_bytes=_vmem(40)),
        name="dsa",
    )(ub, ub, ki4, c4, col, wuk_pad, wuv_pad)


def _outproj_body(ya_ref, yb_ref, yc_ref, yd_ref, x_ref, w_ref, g_ref, o_ref):
    acc = jnp.dot(ya_ref[...], w_ref[0], preferred_element_type=F32)
    acc = acc + jnp.dot(yb_ref[...], w_ref[1], preferred_element_type=F32)
    acc = acc + jnp.dot(yc_ref[...], w_ref[2], preferred_element_type=F32)
    acc = acc + jnp.dot(yd_ref[...], w_ref[3], preferred_element_type=F32)
    ms = jnp.mean(acc * acc, axis=-1, keepdims=True)
    o_ref[...] = x_ref[...] + acc * lax.rsqrt(ms + EPS) * g_ref[...]


def _out_proj(ys, x2d, w4, g, *, tm):
    m, d = x2d.shape
    gw = GROUP_WIDTH
    yspec = pl.BlockSpec((tm, gw), lambda i: (i, 0))
    return pl.pallas_call(
        _outproj_body,
        grid=(m // tm,),
        in_specs=[yspec, yspec, yspec, yspec,
                  pl.BlockSpec((tm, d), lambda i: (i, 0)),
                  pl.BlockSpec((4, gw, d), lambda i: (0, 0, 0)),
                  pl.BlockSpec((1, d), lambda i: (0, 0))],
        out_specs=pl.BlockSpec((tm, d), lambda i: (i, 0)),
        out_shape=jax.ShapeDtypeStruct((m, d), F32),
        compiler_params=pltpu.CompilerParams(
            dimension_semantics=("parallel",), vmem_limit_bytes=_vmem(48)),
        name="out_proj",
    )(*ys, x2d, w4, g)


def _ffn_body(x_ref, gpre_ref, wg_ref, wu_ref, cw_ref, cb_ref, wd_ref, gpost_ref, o_ref,
              xn_ref, acc_ref, gbuf_ref, carry_ref, *, tiles_per_seq):
    i = pl.program_id(0)
    f = pl.program_id(1)
    nf = pl.num_programs(1)
    tm = x_ref.shape[0]

    @pl.when(f == 0)
    def _():
        x = x_ref[...]
        ms = jnp.mean(x * x, axis=-1, keepdims=True)
        xn_ref[...] = (x * lax.rsqrt(ms + EPS) * gpre_ref[...]).astype(BF16)

    @pl.when(i % tiles_per_seq == 0)
    def _():
        carry_ref[f] = jnp.zeros(carry_ref.shape[1:], F32)

    xn = xn_ref[...]
    g = jnp.dot(xn, wg_ref[...], preferred_element_type=F32)
    u = jnp.dot(xn, wu_ref[...], preferred_element_type=F32)
    gbuf_ref[0:8, :] = carry_ref[f]
    gbuf_ref[8:8 + tm, :] = g
    conv = (cb_ref[...]
            + cw_ref[0:1, :] * gbuf_ref[6:6 + tm, :]
            + cw_ref[1:2, :] * gbuf_ref[7:7 + tm, :]
            + cw_ref[2:3, :] * g)
    carry_ref[f] = g[tm - 8:tm, :]
    a = (_silu(conv) * u).astype(BF16)
    part = jnp.dot(a, wd_ref[...], preferred_element_type=F32)

    @pl.when(f == 0)
    def _():
        acc_ref[...] = part

    @pl.when(f > 0)
    def _():
        acc_ref[...] += part

    @pl.when(f == nf - 1)
    def _():
        y = acc_ref[...]
        ms = jnp.mean(y * y, axis=-1, keepdims=True)
        o_ref[...] = x_ref[...] + y * lax.rsqrt(ms + EPS) * gpost_ref[...]


def _ffn(x2d, g_pre, w_gate, w_up, conv_w, conv_b, w_down, g_post, *, tm, tf, tiles_per_seq):
    m, d = x2d.shape
    fdim = w_gate.shape[1]
    nf = fdim // tf
    return pl.pallas_call(
        functools.partial(_ffn_body, tiles_per_seq=tiles_per_seq),
        grid=(m // tm, nf),
        in_specs=[
            pl.BlockSpec((tm, d), lambda i, f: (i, 0)),
            pl.BlockSpec((1, d), lambda i, f: (0, 0)),
            pl.BlockSpec((d, tf), lambda i, f: (0, f)),
            pl.BlockSpec((d, tf), lambda i, f: (0, f)),
            pl.BlockSpec((FFN_CONV, tf), lambda i, f: (0, f)),
            pl.BlockSpec((1, tf), lambda i, f: (0, f)),
            pl.BlockSpec((tf, d), lambda i, f: (f, 0)),
            pl.BlockSpec((1, d), lambda i, f: (0, 0)),
        ],
        out_specs=pl.BlockSpec((tm, d), lambda i, f: (i, 0)),
        out_shape=jax.ShapeDtypeStruct((m, d), F32),
        scratch_shapes=[
            pltpu.VMEM((tm, d), BF16),
            pltpu.VMEM((tm, d), F32),
            pltpu.VMEM((8 + tm, tf), F32),
            pltpu.VMEM((nf, 8, tf), F32),
        ],
        compiler_params=pltpu.CompilerParams(
            dimension_semantics=("arbitrary", "arbitrary"), vmem_limit_bytes=_vmem(56)),
        name="ffn",
    )(x2d, g_pre, w_gate, w_up, conv_w, conv_b, w_down, g_post)


def _permute_w_in(w_in):
    d = w_in.shape[0]
    gw = GROUP_WIDTH
    sizes = (gw, SSD_XBC, N_HEADS, gw, 3 * gw, N_HEADS, gw, DSA_LATENT, IDX_HEADS * IDX_DIM, IDX_DIM, IDX_HEADS)
    offs = [0]
    for s in sizes:
        offs.append(offs[-1] + s)
    z, xbc, dt, pool, qkv, fl, dq, dc, dqi, dki, dwi = (w_in[:, offs[k]:offs[k + 1]] for k in range(len(sizes)))
    zeros = lambda n: jnp.zeros((d, n), w_in.dtype)
    small = jnp.concatenate([dt, fl, dwi, zeros(SM_A - SM_W - IDX_HEADS), dt, zeros(LANES - SM_A - N_HEADS)], axis=1)
    a = jnp.concatenate([xbc, z, pool, dc, small, zeros(A_COLS - 2 * gw - SSD_XBC - DSA_LATENT - LANES)], axis=1)
    bcols = jnp.concatenate([qkv, dq, dqi, dki, dki], axis=1)
    bcols = jnp.concatenate([bcols, zeros(B_COLS - bcols.shape[1])], axis=1)
    return jnp.concatenate([a, bcols], axis=1).astype(BF16)


def _lane_vec(pieces):
    v = jnp.zeros((LANES,), F32)
    for off, val in pieces:
        v = v.at[off:off + val.shape[0]].set(val.astype(F32))
    return v[None, :]


def _pad_head_weights(w_uk, w_uv):
    h, r, d = w_uk.shape
    uk = jnp.zeros((h, 2 * d, r), F32)
    uv = jnp.zeros((h, r, 2 * d), F32)
    for i in range(h):
        o = d * (i % 2)
        uk = uk.at[i, o:o + d, :].set(w_uk[i].T)
        uv = uv.at[i, :, o:o + d].set(w_uv[i])
    return uk.astype(BF16), uv.astype(BF16)


def _tile_sizes(b, lp):
    m = b * lp
    tm_proj = next(t for t in (1024, 512, 256, 128) if m % t == 0)
    tm_out = next(t for t in (512, 256, 128) if m % t == 0)
    tm_ffn = next(t for t in (528, 384, 320, 256, 128) if lp % t == 0)
    return tm_proj, tm_out, tm_ffn


def _layer(h, p, *, topk):
    b, lp, d = h.shape
    m = b * lp
    nblk = lp // SEQ_TILE
    tm_proj, tm_out, tm_ffn = _tile_sizes(b, lp)
    row = lambda v: v.astype(F32)[None, :]

    ua, ub = _in_proj(h.reshape(m, d), row(p["norm_mix_pre"]), _permute_w_in(p["w_in"]), tm=tm_proj, tn=640)
    ua = ua.reshape(b, lp, A_COLS)
    ub = ub.reshape(b, lp, B_COLS)

    bias_vec = _lane_vec([(SM_DT, p["ssd_dt_bias"]), (SM_F, p["fox_f_bias"]), (SM_A, p["ssd_dt_bias"])])
    wscale = jnp.full((IDX_HEADS,), (IDX_HEADS ** -0.5) * (IDX_DIM ** -0.5), F32)
    mul_vec = _lane_vec([(SM_W, wscale), (SM_A, -jnp.exp(p["ssd_a_log"].astype(F32)))])
    col, rowt, c = _prep(ua, bias_vec, mul_vec, row(p["dsa_kv_norm"]))

    dskip_full = jnp.repeat(p["ssd_d"].astype(F32), HEAD_DIM)[None, :]
    y_a = _ssd(ua, col, rowt, p["ssd_conv_w"].astype(F32), row(p["ssd_conv_b"]), dskip_full, row(p["ssd_norm"]))
    y_b = _pool(ua, p["pool_w"].astype(BF16), row(p["pool_scale"]))
    y_c = _fox(ub, col, rowt)
    wuk_pad, wuv_pad = _pad_head_weights(p["dsa_w_uk"], p["dsa_w_uv"])
    y_d = _dsa(ub, ub.reshape(b, nblk, SEQ_TILE, B_COLS), c.reshape(b, nblk, SEQ_TILE, DSA_LATENT), col,
               wuk_pad, wuv_pad, topk=topk)

    ys = [y.reshape(m, GROUP_WIDTH) for y in (y_a, y_b, y_c, y_d)]
    x1 = _out_proj(ys, h.reshape(m, d), p["w_out"].astype(BF16).reshape(4, GROUP_WIDTH, d),
                   row(p["norm_mix_post"]), tm=tm_out)
    x2 = _ffn(x1, row(p["norm_ffn_pre"]), p["ffn_w_gate"].astype(BF16), p["ffn_w_up"].astype(BF16),
              p["ffn_conv_w"].astype(F32), row(p["ffn_conv_b"]), p["ffn_w_down"].astype(BF16),
              row(p["norm_ffn_post"]), tm=tm_ffn, tf=512, tiles_per_seq=lp // tm_ffn)
    return x2.reshape(b, lp, d)


def kernel(x, meta_tokens, norm_mix_pre, norm_mix_post, norm_ffn_pre, norm_ffn_post, w_in, ssd_conv_w, ssd_conv_b, ssd_dt_bias, ssd_a_log, ssd_d, ssd_norm, pool_w, pool_scale, fox_f_bias, dsa_kv_norm, dsa_w_uk, dsa_w_uv, w_out, ffn_w_gate, ffn_w_up, ffn_conv_w, ffn_conv_b, ffn_w_down):
    bsz, seq, d = x.shape
    n = N_META + seq
    lp = -(-n // SEQ_TILE) * SEQ_TILE
    topk = min(DSA_TOPK_MAX, seq // 4)
    meta = jnp.broadcast_to(meta_tokens.astype(x.dtype)[None], (bsz, N_META, d))
    h = jnp.concatenate([meta, x, jnp.zeros((bsz, lp - n, d), x.dtype)], axis=1)
    stacked = dict(norm_mix_pre=norm_mix_pre, norm_mix_post=norm_mix_post, norm_ffn_pre=norm_ffn_pre,
                   norm_ffn_post=norm_ffn_post, w_in=w_in, ssd_conv_w=ssd_conv_w, ssd_conv_b=ssd_conv_b,
                   ssd_dt_bias=ssd_dt_bias, ssd_a_log=ssd_a_log, ssd_d=ssd_d, ssd_norm=ssd_norm,
                   pool_w=pool_w, pool_scale=pool_scale, fox_f_bias=fox_f_bias, dsa_kv_norm=dsa_kv_norm,
                   dsa_w_uk=dsa_w_uk, dsa_w_uv=dsa_w_uv, w_out=w_out, ffn_w_gate=ffn_w_gate,
                   ffn_w_up=ffn_w_up, ffn_conv_w=ffn_conv_w, ffn_conv_b=ffn_conv_b, ffn_w_down=ffn_w_down)
    for i in range(norm_mix_pre.shape[0]):
        h = _layer(h, {k: v[i] for k, v in stacked.items()}, topk=topk)
    return h[:, N_META:n]
```

```python
import functools

import jax
import jax.numpy as jnp
from jax import lax
from jax.experimental import pallas as pl
from jax.experimental.pallas import tpu as pltpu

F32 = jnp.float32
BF16 = jnp.bfloat16
I32 = jnp.int32

EPS = 1e-6
N_META = 16
CHUNK = 64
HEAD_DIM = 64
GROUP_WIDTH = 512
N_HEADS = 8
SSD_GROUPS = 2
SSD_STATE = 128
SSD_CONV = 4
SSD_XBC = GROUP_WIDTH + 2 * SSD_GROUPS * SSD_STATE
POOL_WINDOWS = (2, 4, 8, 16)
DSA_LATENT = 128
IDX_HEADS = 4
IDX_DIM = 64
DSA_TOPK_MAX = 256
FFN_CONV = 3

LANES = 128
SEQ_TILE = 128
INT_MIN = -(2 ** 31)
NEG_BIG = -1e30

A_COLS = 2560
B_COLS = 2560
SM_DT = 0
SM_F = 8
SM_W = 16
SM_A = 24


def _vmem(mb):
    return int(mb * 1024 * 1024)


def _softplus_parts(x):
    t = jnp.log1p(jnp.exp(-jnp.abs(x)))
    return jnp.maximum(x, 0.0) + t, jnp.minimum(x, 0.0) - t


def _silu(x):
    return x / (1.0 + jnp.exp(-x))


def _inproj_body(x_ref, g_ref, w_ref, oa_ref, ob_ref, xn_ref, *, n_a):
    j = pl.program_id(1)

    @pl.when(j == 0)
    def _():
        x = x_ref[...]
        ms = jnp.mean(x * x, axis=-1, keepdims=True)
        xn_ref[...] = (x * lax.rsqrt(ms + EPS) * g_ref[...]).astype(BF16)

    acc = jnp.dot(xn_ref[...], w_ref[...], preferred_element_type=F32)

    @pl.when(j < n_a)
    def _():
        oa_ref[...] = acc

    @pl.when(j >= n_a)
    def _():
        ob_ref[...] = acc.astype(BF16)


def _in_proj(x2d, g, w_perm, *, tm, tn):
    m, d = x2d.shape
    n_a, n_b = A_COLS // tn, B_COLS // tn
    return pl.pallas_call(
        functools.partial(_inproj_body, n_a=n_a),
        grid=(m // tm, n_a + n_b),
        in_specs=[
            pl.BlockSpec((tm, d), lambda i, j: (i, 0)),
            pl.BlockSpec((1, d), lambda i, j: (0, 0)),
            pl.BlockSpec((d, tn), lambda i, j: (0, j)),
        ],
        out_specs=[
            pl.BlockSpec((tm, tn), lambda i, j: (i, jnp.minimum(j, n_a - 1))),
            pl.BlockSpec((tm, tn), lambda i, j: (i, jnp.maximum(j - n_a, 0))),
        ],
        out_shape=[
            jax.ShapeDtypeStruct((m, A_COLS), F32),
            jax.ShapeDtypeStruct((m, B_COLS), BF16),
        ],
        scratch_shapes=[pltpu.VMEM((tm, d), BF16)],
        compiler_params=pltpu.CompilerParams(
            dimension_semantics=("parallel", "arbitrary"), vmem_limit_bytes=_vmem(48)),
        name="in_proj",
    )(x2d, g, w_perm)


def _prep_body(sm_ref, dc_ref, bias_ref, mul_ref, kvg_ref, col_ref, row_ref, c_ref, carry_ref):
    t = pl.program_id(1)

    @pl.when(t == 0)
    def _():
        carry_ref[...] = jnp.zeros_like(carry_ref)

    tt = sm_ref.shape[1]
    s = sm_ref[0]
    lane = lax.broadcasted_iota(I32, (tt, LANES), 1)
    is_dt = lane < SM_F
    is_f = (lane >= SM_F) & (lane < SM_W)
    is_a = (lane >= SM_A) & (lane < SM_A + N_HEADS)
    sp, ls = _softplus_parts(s + bias_ref[...])
    v = jnp.where(is_dt, sp, jnp.where(is_f, ls, jnp.where(is_a, sp, s) * mul_ref[...]))
    ri = lax.broadcasted_iota(I32, (tt, tt), 0)
    ci = lax.broadcasted_iota(I32, (tt, tt), 1)
    tril = jnp.where(ci <= ri, 1.0, 0.0).astype(F32)
    local = jnp.dot(tril, v, precision=lax.Precision.HIGHEST, preferred_element_type=F32)
    out = jnp.where(is_f, local + carry_ref[...], jnp.where(is_a, local, v))
    col_ref[0] = out
    row_ref[0, 0] = out.T
    carry_ref[...] = jnp.where(is_f[0:1], out[tt - 1:tt, :], 0.0)

    dc = dc_ref[0]
    ms = jnp.mean(dc * dc, axis=-1, keepdims=True)
    c_ref[0] = (dc * lax.rsqrt(ms + EPS) * kvg_ref[...]).astype(BF16)


def _prep(ua, bias_vec, mul_vec, kv_g):
    b, lp, _ = ua.shape
    tt = SEQ_TILE
    nblk = lp // tt
    return pl.pallas_call(
        _prep_body,
        grid=(b, nblk),
        in_specs=[
            pl.BlockSpec((1, tt, LANES), lambda i, t: (i, t, 17)),
            pl.BlockSpec((1, tt, LANES), lambda i, t: (i, t, 16)),
            pl.BlockSpec((1, LANES), lambda i, t: (0, 0)),
            pl.BlockSpec((1, LANES), lambda i, t: (0, 0)),
            pl.BlockSpec((1, LANES), lambda i, t: (0, 0)),
        ],
        out_specs=[
            pl.BlockSpec((1, tt, LANES), lambda i, t: (i, t, 0)),
            pl.BlockSpec((1, 1, LANES, tt), lambda i, t: (i, t, 0, 0)),
            pl.BlockSpec((1, tt, LANES), lambda i, t: (i, t, 0)),
        ],
        out_shape=[
            jax.ShapeDtypeStruct((b, lp, LANES), F32),
            jax.ShapeDtypeStruct((b, nblk, LANES, tt), F32),
            jax.ShapeDtypeStruct((b, lp, DSA_LATENT), BF16),
        ],
        scratch_shapes=[pltpu.VMEM((1, LANES), F32)],
        compiler_params=pltpu.CompilerParams(dimension_semantics=("parallel", "arbitrary")),
        name="prep",
    )(ua, ua, bias_vec, mul_vec, kv_g)


def _expand_heads(colv, base, lo_half):
    parts = []
    for p in range(N_HEADS // 2):
        a = colv[:, base + 2 * p:base + 2 * p + 1]
        b = colv[:, base + 2 * p + 1:base + 2 * p + 2]
        parts.append(jnp.where(lo_half, a, b))
    return jnp.concatenate(parts, axis=1)


def _ssd_body(xbc_ref, z_ref, col_ref, row_ref, cw_ref, cb_ref, dsk_ref, ng_ref, y_ref,
              xpad_ref, st_ref):
    c = pl.program_id(1)
    ll = xbc_ref.shape[1]
    gw = GROUP_WIDTH
    ns = SSD_STATE
    hpg = N_HEADS // SSD_GROUPS
    gcols = hpg * HEAD_DIM

    @pl.when(c == 0)
    def _():
        xpad_ref[0:8, :] = jnp.zeros((8, SSD_XBC), F32)
        st_ref[...] = jnp.zeros_like(st_ref)

    x = xbc_ref[0]
    xpad_ref[8:8 + ll, :] = x
    conv = (cb_ref[...]
            + cw_ref[0:1, :] * xpad_ref[5:5 + ll, :]
            + cw_ref[1:2, :] * xpad_ref[6:6 + ll, :]
            + cw_ref[2:3, :] * xpad_ref[7:7 + ll, :]
            + cw_ref[3:4, :] * x)
    xpad_ref[0:8, :] = x[ll - 8:ll, :]
    act = _silu(conv)
    xs = act[:, 0:gw]
    bm = act[:, gw:gw + SSD_GROUPS * ns]
    cm = act[:, gw + SSD_GROUPS * ns:]

    colv = col_ref[0]
    rowv = row_ref[0, 0]
    lo_half = lax.broadcasted_iota(I32, (ll, LANES), 1) < HEAD_DIM
    dt_full = _expand_heads(colv, SM_DT, lo_half)
    acs_full = _expand_heads(colv, SM_A, lo_half)
    acs_last = acs_full[ll - 1:ll, :]
    dte_full = jnp.exp(acs_last - acs_full)
    dfs_full = jnp.exp(acs_full)
    xdt = xs * dt_full
    xdt_b = xdt.astype(BF16)
    xdte_b = (xdt * dte_full).astype(BF16)
    cm_b = cm.astype(BF16)
    ri = lax.broadcasted_iota(I32, (ll, ll), 0)
    ci = lax.broadcasted_iota(I32, (ll, ll), 1)
    tril = ci <= ri

    ys = []
    for g in range(SSD_GROUPS):
        bg = bm[:, ns * g:ns * (g + 1)]
        bg_b = bg.astype(BF16)
        bgt_b = bg.T.astype(BF16)
        cg_b = cm_b[:, ns * g:ns * (g + 1)]
        cb = lax.dot_general(cg_b, bg_b, (((1,), (1,)), ((), ())), preferred_element_type=F32)
        sg = st_ref[g]
        yoff = (jnp.dot(cg_b, sg.astype(BF16), preferred_element_type=F32)
                * dfs_full[:, gcols * g:gcols * (g + 1)])
        parts = []
        for pr in range(hpg // 2):
            xpair = xdt_b[:, gcols * g + LANES * pr:gcols * g + LANES * (pr + 1)]
            res = []
            for hh in range(2):
                h = hpg * g + 2 * pr + hh
                seg = colv[:, SM_A + h:SM_A + h + 1] - rowv[SM_A + h:SM_A + h + 1, :]
                lm = jnp.exp(jnp.where(tril, seg, -jnp.inf))
                res.append(jnp.dot((cb * lm).astype(BF16), xpair, preferred_element_type=F32))
            parts.append(jnp.where(lo_half, res[0], res[1]))
        ydiag = jnp.concatenate(parts, axis=1)
        decay = jnp.exp(acs_last[:, gcols * g:gcols * (g + 1)])
        st_ref[g] = decay * sg + jnp.dot(bgt_b, xdte_b[:, gcols * g:gcols * (g + 1)],
                                         preferred_element_type=F32)
        ys.append(ydiag + yoff)

    y = jnp.concatenate(ys, axis=1) + dsk_ref[...] * xs
    gz = y * _silu(z_ref[0])
    outs = []
    for g in range(SSD_GROUPS):
        gg = gz[:, gcols * g:gcols * (g + 1)]
        outs.append(gg * lax.rsqrt(jnp.mean(gg * gg, axis=-1, keepdims=True) + EPS))
    y_ref[0] = (jnp.concatenate(outs, axis=1) * ng_ref[...]).astype(BF16)


def _ssd(ua, col, row, conv_w, conv_b, dskip_full, norm_g):
    b, lp, _ = ua.shape
    ll = SEQ_TILE
    nblk = lp // ll
    hpg = N_HEADS // SSD_GROUPS
    return pl.pallas_call(
        _ssd_body,
        grid=(b, nblk),
        in_specs=[
            pl.BlockSpec((1, ll, SSD_XBC), lambda i, c: (i, c, 0)),
            pl.BlockSpec((1, ll, GROUP_WIDTH), lambda i, c: (i, c, 2)),
            pl.BlockSpec((1, ll, LANES), lambda i, c: (i, c, 0)),
            pl.BlockSpec((1, 1, LANES, ll), lambda i, c: (i, c, 0, 0)),
            pl.BlockSpec((SSD_CONV, SSD_XBC), lambda i, c: (0, 0)),
            pl.BlockSpec((1, SSD_XBC), lambda i, c: (0, 0)),
            pl.BlockSpec((1, GROUP_WIDTH), lambda i, c: (0, 0)),
            pl.BlockSpec((1, GROUP_WIDTH), lambda i, c: (0, 0)),
        ],
        out_specs=pl.BlockSpec((1, ll, GROUP_WIDTH), lambda i, c: (i, c, 0)),
        out_shape=jax.ShapeDtypeStruct((b, lp, GROUP_WIDTH), BF16),
        scratch_shapes=[
            pltpu.VMEM((8 + ll, SSD_XBC), F32),
            pltpu.VMEM((SSD_GROUPS, SSD_STATE, hpg * HEAD_DIM), F32),
        ],
        compiler_params=pltpu.CompilerParams(dimension_semantics=("parallel", "arbitrary")),
        name="ssd",
    )(ua, ua, col, row, conv_w, conv_b, dskip_full, norm_g)


def _pool_body(u_ref, w_ref, sc_ref, y_ref, buf_ref):
    t = pl.program_id(1)
    tt = u_ref.shape[1]
    hist = max(POOL_WINDOWS)
    gd = GROUP_WIDTH // len(POOL_WINDOWS)

    @pl.when(t == 0)
    def _():
        buf_ref[0:hist, :] = jnp.zeros((hist, GROUP_WIDTH), F32)

    u = u_ref[0]
    buf_ref[hist:hist + tt, :] = u
    count = (t * tt + 1 + lax.broadcasted_iota(I32, (tt, 1), 0)).astype(F32)
    outs = []
    for gi, win in enumerate(POOL_WINDOWS):
        ug = u[:, gd * gi:gd * (gi + 1)]
        acc = ug
        for k in range(1, win):
            acc = acc + buf_ref[hist - k:hist - k + tt, gd * gi:gd * (gi + 1)]
        pooled = acc / jnp.minimum(count, float(win)) - ug
        outs.append(jnp.dot(pooled.astype(BF16), w_ref[gi], preferred_element_type=F32))
    y_ref[0] = (jnp.concatenate(outs, axis=1) * sc_ref[...]).astype(BF16)
    buf_ref[0:hist, :] = u[tt - hist:tt, :]


def _pool(ua, w, scale):
    b, lp, _ = ua.shape
    tt = SEQ_TILE
    ng = len(POOL_WINDOWS)
    gd = GROUP_WIDTH // ng
    return pl.pallas_call(
        _pool_body,
        grid=(b, lp // tt),
        in_specs=[
            pl.BlockSpec((1, tt, GROUP_WIDTH), lambda i, t: (i, t, 3)),
            pl.BlockSpec((ng, gd, gd), lambda i, t: (0, 0, 0)),
            pl.BlockSpec((1, GROUP_WIDTH), lambda i, t: (0, 0)),
        ],
        out_specs=pl.BlockSpec((1, tt, GROUP_WIDTH), lambda i, t: (i, t, 0)),
        out_shape=jax.ShapeDtypeStruct((b, lp, GROUP_WIDTH), BF16),
        scratch_shapes=[pltpu.VMEM((max(POOL_WINDOWS) + tt, GROUP_WIDTH), F32)],
        compiler_params=pltpu.CompilerParams(dimension_semantics=("parallel", "arbitrary")),
        name="pool",
    )(ua, w, scale)


def _fox_body(q_ref, k_ref, v_ref, fc_ref, fr_ref, o_ref, qm_ref, acc_ref, m_ref, *, sub):
    qi = pl.program_id(1)
    tq = q_ref.shape[1]
    tk = tq
    lo_half = lax.broadcasted_iota(I32, (tq, LANES), 1) < HEAD_DIM
    causal = (lax.broadcasted_iota(I32, (tq, tk), 1) <= lax.broadcasted_iota(I32, (tq, tk), 0))
    nt = (((1,), (1,)), ((), ()))
    zero_b = jnp.zeros((tq, LANES), BF16)
    ones_b = jnp.ones((tk, LANES), BF16)

    for pr in range(N_HEADS // 2):
        q2 = q_ref[0, :, LANES * pr:LANES * (pr + 1)] * jnp.asarray(HEAD_DIM ** -0.5, BF16)
        qm_ref[2 * pr] = jnp.where(lo_half, q2, zero_b)
        qm_ref[2 * pr + 1] = jnp.where(lo_half, zero_b, q2)
    acc_ref[...] = jnp.zeros_like(acc_ref)
    m_ref[...] = jnp.full(m_ref.shape, NEG_BIG, F32)
    fref = fc_ref[0, 0:1, :]

    def step(kb, masked):
        ks = pl.multiple_of(kb * tk, tk)
        fk = jnp.concatenate([fr_ref[0, kb * sub + j] for j in range(sub)], axis=1)
        for pr in range(N_HEADS // 2):
            cols = slice(LANES * pr, LANES * (pr + 1))
            k2 = k_ref[0, pl.ds(ks, tk), cols]
            vext = jnp.concatenate([v_ref[0, pl.ds(ks, tk), cols], ones_b], axis=1)
            for hh in range(2):
                h = 2 * pr + hh
                s = lax.dot_general(qm_ref[h], k2, nt, preferred_element_type=F32)
                t = s - (fk[h:h + 1, :] - fref[:, SM_F + h:SM_F + h + 1])
                if masked:
                    t = jnp.where(causal, t, -jnp.inf)
                m_old = m_ref[h]
                mn = jnp.maximum(m_old, jnp.max(t, axis=1, keepdims=True))
                p = jnp.exp(t - mn[:, 0:1])
                al = jnp.exp(m_old - mn)
                al2 = jnp.concatenate([al, al], axis=1)
                acc_ref[h] = al2 * acc_ref[h] + jnp.dot(p.astype(BF16), vext, preferred_element_type=F32)
                m_ref[h] = mn

    def body(kb, carry):
        step(kb, False)
        return carry

    lax.fori_loop(0, qi, body, 0)
    step(qi, True)
    for pr in range(N_HEADS // 2):
        a0 = acc_ref[2 * pr]
        a1 = acc_ref[2 * pr + 1]
        o_ref[0, :, LANES * pr:LANES * (pr + 1)] = jnp.where(
            lo_half, a0[:, :LANES] / a0[:, LANES:], a1[:, :LANES] / a1[:, LANES:]).astype(BF16)


def _fox(ub, col, row, *, sub):
    b, lp, _ = ub.shape
    tq = sub * SEQ_TILE
    nblk = lp // SEQ_TILE
    return pl.pallas_call(
        functools.partial(_fox_body, sub=sub),
        grid=(b, lp // tq),
        in_specs=[
            pl.BlockSpec((1, tq, GROUP_WIDTH), lambda i, q: (i, q, 0)),
            pl.BlockSpec((1, lp, GROUP_WIDTH), lambda i, q: (i, 0, 1)),
            pl.BlockSpec((1, lp, GROUP_WIDTH), lambda i, q: (i, 0, 2)),
            pl.BlockSpec((1, tq, LANES), lambda i, q: (i, q, 0)),
            pl.BlockSpec((1, nblk, N_HEADS, SEQ_TILE), lambda i, q: (i, 0, SM_F // N_HEADS, 0)),
        ],
        out_specs=pl.BlockSpec((1, tq, GROUP_WIDTH), lambda i, q: (i, q, 0)),
        out_shape=jax.ShapeDtypeStruct((b, lp, GROUP_WIDTH), BF16),
        scratch_shapes=[
            pltpu.VMEM((N_HEADS, tq, LANES), BF16),
            pltpu.VMEM((N_HEADS, tq, 2 * LANES), F32),
            pltpu.VMEM((N_HEADS, tq, LANES), F32),
        ],
        compiler_params=pltpu.CompilerParams(
            dimension_semantics=("parallel", "arbitrary"), vmem_limit_bytes=_vmem(48)),
        name="fox",
    )(ub, ub, ub, col, row)


def _dsa_body(dq_ref, dqi_ref, ki_ref, c_ref, wcol_ref, wuk_ref, wuv_ref, o_ref,
              key_ref, bias_ref, qm_ref, qlat_ref, acc_ref, m_ref, *, topk):
    qi = pl.program_id(1)
    tq = dq_ref.shape[1]
    tk = tq
    sub = tk // LANES
    ngrp = jnp.minimum(qi + 2, key_ref.shape[0])
    nt = (((1,), (1,)), ((), ()))
    lo_half = lax.broadcasted_iota(I32, (tq, LANES), 1) < IDX_DIM
    zero_b = jnp.zeros((tq, LANES), BF16)
    shift = CHUNK - N_META
    lg2 = CHUNK.bit_length() - 1
    qcid = (qi * tq + lax.broadcasted_iota(I32, (tq, 1), 0) + shift) >> lg2
    klane = lax.broadcasted_iota(I32, (1, tk), 1)
    kf = float(topk)

    def lane_tiles_sum(w):
        return functools.reduce(lambda a, b: a + b, [w[:, LANES * j:LANES * (j + 1)] for j in range(sub)])

    for h in range(IDX_HEADS):
        q2 = dqi_ref[0, :, LANES * (h // 2):LANES * (h // 2 + 1)]
        qm_ref[h] = jnp.where(lo_half, q2, zero_b) if h % 2 == 0 else jnp.where(lo_half, zero_b, q2)
    wcol = wcol_ref[0]
    wis = [wcol[:, SM_W + h:SM_W + h + 1] for h in range(IDX_HEADS)]

    def score_step(g, carry):
        ks = pl.multiple_of(g * tk, tk)
        kt = ki_ref[0, pl.ds(ks, tk), :]
        sc = None
        for h in range(IDX_HEADS):
            lg = lax.dot_general(qm_ref[h], kt, nt, preferred_element_type=F32)
            term = wis[h] * jnp.maximum(lg, 0.0)
            sc = term if sc is None else sc + term
        bits = pltpu.bitcast(sc, I32)
        bits = jnp.where(bits == INT_MIN, 0, bits)
        skey = bits ^ ((bits >> 31) & 0x7FFFFFFF)
        kcid = (ks + klane + shift) >> lg2
        key_ref[g] = jnp.where(kcid <= qcid, skey, INT_MIN)
        return carry

    lax.fori_loop(0, ngrp, score_step, 0)

    def bit_step(i, u):
        uc = u | lax.shift_left(jnp.int32(1), 31 - i)
        sc = uc ^ INT_MIN

        def cnt_step(g, acc):
            key = key_ref[g]
            for j in range(sub):
                acc = acc + jnp.where(key[:, LANES * j:LANES * (j + 1)] >= sc, 1.0, 0.0)
            return acc

        acc = lax.fori_loop(0, ngrp, cnt_step, jnp.zeros((tq, LANES), F32))
        cnt = jnp.sum(acc, axis=1, keepdims=True)
        return jnp.where(cnt >= kf, uc, u)

    thr = (lax.fori_loop(0, 32, bit_step, jnp.zeros((tq, LANES), I32)) ^ INT_MIN)[:, 0:1]

    def gt_step(g, acc):
        return acc + lane_tiles_sum(jnp.where(key_ref[g] > thr, 1.0, 0.0))

    ngt = jnp.sum(lax.fori_loop(0, ngrp, gt_step, jnp.zeros((tq, LANES), F32)), axis=1, keepdims=True)
    room = kf - ngt
    incl = jnp.where(lax.broadcasted_iota(I32, (tk, tk), 0) <= lax.broadcasted_iota(I32, (tk, tk), 1),
                     1.0, 0.0).astype(BF16)

    def mask_step(g, seen):
        key = key_ref[g]
        eqf = jnp.where(key == thr, 1.0, 0.0)
        rank = jnp.dot(eqf.astype(BF16), incl, preferred_element_type=F32) + seen
        tie = jnp.where(rank <= room, eqf, 0.0)
        sel = jnp.where(key > thr, 1.0, tie)
        sel = jnp.where(key == INT_MIN, 0.0, sel)
        bias_ref[g] = jnp.where(sel > 0.5, 0.0, -jnp.inf)
        return seen + jnp.sum(eqf, axis=1, keepdims=True)

    lax.fori_loop(0, ngrp, mask_step, jnp.zeros((tq, 1), F32))

    for h in range(N_HEADS):
        dq2 = dq_ref[0, :, LANES * (h // 2):LANES * (h // 2 + 1)]
        ql = jnp.dot(dq2, wuk_ref[h], preferred_element_type=F32) * (HEAD_DIM ** -0.5)
        qlat_ref[h] = ql.astype(BF16)
    acc_ref[...] = jnp.zeros_like(acc_ref)
    m_ref[...] = jnp.full(m_ref.shape, NEG_BIG, F32)
    ones_b = jnp.ones((tk, LANES), BF16)

    def att_step(g, carry):
        ks = pl.multiple_of(g * tk, tk)
        ct = c_ref[0, pl.ds(ks, tk), :]
        cext = jnp.concatenate([ct, ones_b], axis=1)
        bias = bias_ref[g]
        for h in range(N_HEADS):
            s = lax.dot_general(qlat_ref[h], ct, nt, preferred_element_type=F32) + bias
            m_old = m_ref[h]
            mn = jnp.maximum(m_old, jnp.max(s, axis=1, keepdims=True))
            p = jnp.exp(s - mn[:, 0:1])
            al = jnp.exp(m_old - mn)
            acc_ref[h] = (jnp.concatenate([al, al], axis=1) * acc_ref[h]
                          + jnp.dot(p.astype(BF16), cext, preferred_element_type=F32))
            m_ref[h] = mn
        return carry

    lax.fori_loop(0, ngrp, att_step, 0)

    for pr in range(N_HEADS // 2):
        out = None
        for hh in range(2):
            a = acc_ref[2 * pr + hh]
            olat = (a[:, :DSA_LATENT] / a[:, DSA_LATENT:]).astype(BF16)
            part = jnp.dot(olat, wuv_ref[2 * pr + hh], preferred_element_type=F32)
            out = part if out is None else out + part
        o_ref[0, :, LANES * pr:LANES * (pr + 1)] = out.astype(BF16)


def _dsa(ub, c, col, wuk_pad, wuv_pad, *, topk, sub):
    b, lp, _ = ub.shape
    tq = sub * SEQ_TILE
    ngrp = lp // tq
    return pl.pallas_call(
        functools.partial(_dsa_body, topk=topk),
        grid=(b, ngrp),
        in_specs=[
            pl.BlockSpec((1, tq, GROUP_WIDTH), lambda i, q: (i, q, 3)),
            pl.BlockSpec((1, tq, IDX_HEADS * IDX_DIM), lambda i, q: (i, q, 8)),
            pl.BlockSpec((1, lp, LANES), lambda i, q: (i, 0, 18)),
            pl.BlockSpec((1, lp, DSA_LATENT), lambda i, q: (i, 0, 0)),
            pl.BlockSpec((1, tq, LANES), lambda i, q: (i, q, 0)),
            pl.BlockSpec((N_HEADS, LANES, DSA_LATENT), lambda i, q: (0, 0, 0)),
            pl.BlockSpec((N_HEADS, DSA_LATENT, LANES), lambda i, q: (0, 0, 0)),
        ],
        out_specs=pl.BlockSpec((1, tq, GROUP_WIDTH), lambda i, q: (i, q, 0)),
        out_shape=jax.ShapeDtypeStruct((b, lp, GROUP_WIDTH), BF16),
        scratch_shapes=[
            pltpu.VMEM((ngrp, tq, tq), I32),
            pltpu.VMEM((ngrp, tq, tq), F32),
            pltpu.VMEM((IDX_HEADS, tq, LANES), BF16),
            pltpu.VMEM((N_HEADS, tq, DSA_LATENT), BF16),
            pltpu.VMEM((N_HEADS, tq, 2 * DSA_LATENT), F32),
            pltpu.VMEM((N_HEADS, tq, LANES), F32),
        ],
        compiler_params=pltpu.CompilerParams(
            dimension_semantics=("parallel", "arbitrary"), vmem_limit_bytes=_vmem(48)),
        name="dsa",
    )(ub, ub, ub, c, col, wuk_pad, wuv_pad)


def _outproj_body(ya_ref, yb_ref, yc_ref, yd_ref, x_ref, w_ref, g_ref, o_ref):
    acc = jnp.dot(ya_ref[...], w_ref[0], preferred_element_type=F32)
    acc = acc + jnp.dot(yb_ref[...], w_ref[1], preferred_element_type=F32)
    acc = acc + jnp.dot(yc_ref[...], w_ref[2], preferred_element_type=F32)
    acc = acc + jnp.dot(yd_ref[...], w_ref[3], preferred_element_type=F32)
    ms = jnp.mean(acc * acc, axis=-1, keepdims=True)
    o_ref[...] = x_ref[...] + acc * lax.rsqrt(ms + EPS) * g_ref[...]


def _out_proj(ys, x2d, w4, g, *, tm):
    m, d = x2d.shape
    gw = GROUP_WIDTH
    yspec = pl.BlockSpec((tm, gw), lambda i: (i, 0))
    return pl.pallas_call(
        _outproj_body,
        grid=(m // tm,),
        in_specs=[yspec, yspec, yspec, yspec,
                  pl.BlockSpec((tm, d), lambda i: (i, 0)),
                  pl.BlockSpec((4, gw, d), lambda i: (0, 0, 0)),
                  pl.BlockSpec((1, d), lambda i: (0, 0))],
        out_specs=pl.BlockSpec((tm, d), lambda i: (i, 0)),
        out_shape=jax.ShapeDtypeStruct((m, d), F32),
        compiler_params=pltpu.CompilerParams(
            dimension_semantics=("parallel",), vmem_limit_bytes=_vmem(48)),
        name="out_proj",
    )(*ys, x2d, w4, g)


def _ffn_body(x_ref, gpre_ref, wg_ref, wu_ref, cw_ref, cb_ref, wd_ref, gpost_ref, o_ref,
              xn_ref, acc_ref, gbuf_ref, carry_ref, *, tiles_per_seq):
    i = pl.program_id(0)
    f = pl.program_id(1)
    nf = pl.num_programs(1)
    tm = x_ref.shape[0]

    @pl.when(f == 0)
    def _():
        x = x_ref[...]
        ms = jnp.mean(x * x, axis=-1, keepdims=True)
        xn_ref[...] = (x * lax.rsqrt(ms + EPS) * gpre_ref[...]).astype(BF16)

    @pl.when(i % tiles_per_seq == 0)
    def _():
        carry_ref[f] = jnp.zeros(carry_ref.shape[1:], F32)

    xn = xn_ref[...]
    g = jnp.dot(xn, wg_ref[...], preferred_element_type=F32)
    u = jnp.dot(xn, wu_ref[...], preferred_element_type=F32)
    gbuf_ref[0:8, :] = carry_ref[f]
    gbuf_ref[8:8 + tm, :] = g
    conv = (cb_ref[...]
            + cw_ref[0:1, :] * gbuf_ref[6:6 + tm, :]
            + cw_ref[1:2, :] * gbuf_ref[7:7 + tm, :]
            + cw_ref[2:3, :] * g)
    carry_ref[f] = g[tm - 8:tm, :]
    a = (_silu(conv) * u).astype(BF16)
    part = jnp.dot(a, wd_ref[...], preferred_element_type=F32)

    @pl.when(f == 0)
    def _():
        acc_ref[...] = part

    @pl.when(f > 0)
    def _():
        acc_ref[...] += part

    @pl.when(f == nf - 1)
    def _():
        y = acc_ref[...]
        ms = jnp.mean(y * y, axis=-1, keepdims=True)
        o_ref[...] = x_ref[...] + y * lax.rsqrt(ms + EPS) * gpost_ref[...]


def _ffn(x2d, g_pre, w_gate, w_up, conv_w, conv_b, w_down, g_post, *, tm, tf, tiles_per_seq):
    m, d = x2d.shape
    fdim = w_gate.shape[1]
    nf = fdim // tf
    return pl.pallas_call(
        functools.partial(_ffn_body, tiles_per_seq=tiles_per_seq),
        grid=(m // tm, nf),
        in_specs=[
            pl.BlockSpec((tm, d), lambda i, f: (i, 0)),
            pl.BlockSpec((1, d), lambda i, f: (0, 0)),
            pl.BlockSpec((d, tf), lambda i, f: (0, f)),
            pl.BlockSpec((d, tf), lambda i, f: (0, f)),
            pl.BlockSpec((FFN_CONV, tf), lambda i, f: (0, f)),
            pl.BlockSpec((1, tf), lambda i, f: (0, f)),
            pl.BlockSpec((tf, d), lambda i, f: (f, 0)),
            pl.BlockSpec((1, d), lambda i, f: (0, 0)),
        ],
        out_specs=pl.BlockSpec((tm, d), lambda i, f: (i, 0)),
        out_shape=jax.ShapeDtypeStruct((m, d), F32),
        scratch_shapes=[
            pltpu.VMEM((tm, d), BF16),
            pltpu.VMEM((tm, d), F32),
            pltpu.VMEM((8 + tm, tf), F32),
            pltpu.VMEM((nf, 8, tf), F32),
        ],
        compiler_params=pltpu.CompilerParams(
            dimension_semantics=("arbitrary", "arbitrary"), vmem_limit_bytes=_vmem(56)),
        name="ffn",
    )(x2d, g_pre, w_gate, w_up, conv_w, conv_b, w_down, g_post)


def _permute_w_in(w_in):
    d = w_in.shape[0]
    gw = GROUP_WIDTH
    sizes = (gw, SSD_XBC, N_HEADS, gw, 3 * gw, N_HEADS, gw, DSA_LATENT, IDX_HEADS * IDX_DIM, IDX_DIM, IDX_HEADS)
    offs = [0]
    for s in sizes:
        offs.append(offs[-1] + s)
    z, xbc, dt, pool, qkv, fl, dq, dc, dqi, dki, dwi = (w_in[:, offs[k]:offs[k + 1]] for k in range(len(sizes)))
    zeros = lambda n: jnp.zeros((d, n), w_in.dtype)
    small = jnp.concatenate([dt, fl, dwi, zeros(SM_A - SM_W - IDX_HEADS), dt, zeros(LANES - SM_A - N_HEADS)], axis=1)
    a = jnp.concatenate([xbc, z, pool, dc, small, zeros(A_COLS - 2 * gw - SSD_XBC - DSA_LATENT - LANES)], axis=1)
    bcols = jnp.concatenate([qkv, dq, dqi, dki, dki], axis=1)
    bcols = jnp.concatenate([bcols, zeros(B_COLS - bcols.shape[1])], axis=1)
    return jnp.concatenate([a, bcols], axis=1).astype(BF16)


def _lane_vec(pieces):
    v = jnp.zeros((LANES,), F32)
    for off, val in pieces:
        v = v.at[off:off + val.shape[0]].set(val.astype(F32))
    return v[None, :]


def _pad_head_weights(w_uk, w_uv):
    h, r, d = w_uk.shape
    uk = jnp.zeros((h, 2 * d, r), F32)
    uv = jnp.zeros((h, r, 2 * d), F32)
    for i in range(h):
        o = d * (i % 2)
        uk = uk.at[i, o:o + d, :].set(w_uk[i].T)
        uv = uv.at[i, :, o:o + d].set(w_uv[i])
    return uk.astype(BF16), uv.astype(BF16)


def _tile_sizes(b, lp):
    m = b * lp
    tm_proj = next(t for t in (1024, 512, 256, 128) if m % t == 0)
    tm_out = next(t for t in (512, 256, 128) if m % t == 0)
    tm_ffn = next(t for t in (528, 384, 320, 256, 128) if lp % t == 0)
    att_sub = 3 if lp % (3 * SEQ_TILE) == 0 else 1
    return tm_proj, tm_out, tm_ffn, att_sub


def _layer(h, p, *, topk):
    b, lp, d = h.shape
    m = b * lp
    tm_proj, tm_out, tm_ffn, att_sub = _tile_sizes(b, lp)
    row = lambda v: v.astype(F32)[None, :]

    ua, ub = _in_proj(h.reshape(m, d), row(p["norm_mix_pre"]), _permute_w_in(p["w_in"]), tm=tm_proj, tn=640)
    ua = ua.reshape(b, lp, A_COLS)
    ub = ub.reshape(b, lp, B_COLS)

    bias_vec = _lane_vec([(SM_DT, p["ssd_dt_bias"]), (SM_F, p["fox_f_bias"]), (SM_A, p["ssd_dt_bias"])])
    wscale = jnp.full((IDX_HEADS,), (IDX_HEADS ** -0.5) * (IDX_DIM ** -0.5), F32)
    mul_vec = _lane_vec([(SM_W, wscale), (SM_A, -jnp.exp(p["ssd_a_log"].astype(F32)))])
    col, rowt, c = _prep(ua, bias_vec, mul_vec, row(p["dsa_kv_norm"]))

    dskip_full = jnp.repeat(p["ssd_d"].astype(F32), HEAD_DIM)[None, :]
    y_a = _ssd(ua, col, rowt, p["ssd_conv_w"].astype(F32), row(p["ssd_conv_b"]), dskip_full, row(p["ssd_norm"]))
    y_b = _pool(ua, p["pool_w"].astype(BF16), row(p["pool_scale"]))
    y_c = _fox(ub, col, rowt, sub=att_sub)
    wuk_pad, wuv_pad = _pad_head_weights(p["dsa_w_uk"], p["dsa_w_uv"])
    y_d = _dsa(ub, c, col, wuk_pad, wuv_pad, topk=topk, sub=att_sub)

    ys = [y.reshape(m, GROUP_WIDTH) for y in (y_a, y_b, y_c, y_d)]
    x1 = _out_proj(ys, h.reshape(m, d), p["w_out"].astype(BF16).reshape(4, GROUP_WIDTH, d),
                   row(p["norm_mix_post"]), tm=tm_out)
    x2 = _ffn(x1, row(p["norm_ffn_pre"]), p["ffn_w_gate"].astype(BF16), p["ffn_w_up"].astype(BF16),
              p["ffn_conv_w"].astype(F32), row(p["ffn_conv_b"]), p["ffn_w_down"].astype(BF16),
              row(p["norm_ffn_post"]), tm=tm_ffn, tf=512, tiles_per_seq=lp // tm_ffn)
    return x2.reshape(b, lp, d)


def kernel(x, meta_tokens, norm_mix_pre, norm_mix_post, norm_ffn_pre, norm_ffn_post, w_in, ssd_conv_w, ssd_conv_b, ssd_dt_bias, ssd_a_log, ssd_d, ssd_norm, pool_w, pool_scale, fox_f_bias, dsa_kv_norm, dsa_w_uk, dsa_w_uv, w_out, ffn_w_gate, ffn_w_up, ffn_conv_w, ffn_conv_b, ffn_w_down):
    bsz, seq, d = x.shape
    n = N_META + seq
    lp = -(-n // SEQ_TILE) * SEQ_TILE
    topk = min(DSA_TOPK_MAX, seq // 4)
    meta = jnp.broadcast_to(meta_tokens.astype(x.dtype)[None], (bsz, N_META, d))
    h = jnp.concatenate([meta, x, jnp.zeros((bsz, lp - n, d), x.dtype)], axis=1)
    stacked = dict(norm_mix_pre=norm_mix_pre, norm_mix_post=norm_mix_post, norm_ffn_pre=norm_ffn_pre,
                   norm_ffn_post=norm_ffn_post, w_in=w_in, ssd_conv_w=ssd_conv_w, ssd_conv_b=ssd_conv_b,
                   ssd_dt_bias=ssd_dt_bias, ssd_a_log=ssd_a_log, ssd_d=ssd_d, ssd_norm=ssd_norm,
                   pool_w=pool_w, pool_scale=pool_scale, fox_f_bias=fox_f_bias, dsa_kv_norm=dsa_kv_norm,
                   dsa_w_uk=dsa_w_uk, dsa_w_uv=dsa_w_uv, w_out=w_out, ffn_w_gate=ffn_w_gate,
                   ffn_w_up=ffn_w_up, ffn_conv_w=ffn_conv_w, ffn_conv_b=ffn_conv_b, ffn_w_down=ffn_w_down)
    for i in range(norm_mix_pre.shape[0]):
        h = _layer(h, {k: v[i] for k, v in stacked.items()}, topk=topk)
    return h[:, N_META:n]
```

```python
import functools

import jax
import jax.numpy as jnp
from jax import lax
from jax.experimental import pallas as pl
from jax.experimental.pallas import tpu as pltpu

F32 = jnp.float32
BF16 = jnp.bfloat16
I32 = jnp.int32

EPS = 1e-6
N_META = 16
CHUNK = 64
HEAD_DIM = 64
GROUP_WIDTH = 512
N_HEADS = 8
SSD_GROUPS = 2
SSD_STATE = 128
SSD_CONV = 4
SSD_XBC = GROUP_WIDTH + 2 * SSD_GROUPS * SSD_STATE
POOL_WINDOWS = (2, 4, 8, 16)
DSA_LATENT = 128
IDX_HEADS = 4
IDX_DIM = 64
DSA_TOPK_MAX = 256
FFN_CONV = 3

LANES = 128
SEQ_TILE = 128
INT_MIN = -(2 ** 31)
NEG_BIG = -1e30

A_COLS = 2560
B_COLS = 2560
SM_DT = 0
SM_F = 8
SM_W = 16
SM_A = 24
CT_ROWS = DSA_LATENT + 16
ACC_ROWS = 2 * HEAD_DIM + 16


def _vmem(mb):
    return int(mb * 1024 * 1024)


def _softplus_parts(x):
    t = jnp.log1p(jnp.exp(-jnp.abs(x)))
    return jnp.maximum(x, 0.0) + t, jnp.minimum(x, 0.0) - t


def _silu(x):
    return x / (1.0 + jnp.exp(-x))


def _cast_body(x_ref, o_ref):
    o_ref[...] = x_ref[...].astype(BF16)


def _to_bf16(w, *, tr=256):
    shape = w.shape
    w2 = w.reshape(-1, shape[-1])
    r, c = w2.shape
    out = pl.pallas_call(
        _cast_body,
        grid=(r // tr,),
        in_specs=[pl.BlockSpec((tr, c), lambda i: (i, 0))],
        out_specs=pl.BlockSpec((tr, c), lambda i: (i, 0)),
        out_shape=jax.ShapeDtypeStruct((r, c), BF16),
        compiler_params=pltpu.CompilerParams(dimension_semantics=("parallel",), vmem_limit_bytes=_vmem(40)),
        name="to_bf16",
    )(w2)
    return out.reshape(shape)


def _inproj_body(x_ref, g_ref, w_ref, oa_ref, ob_ref, xn_ref, *, n_a):
    j = pl.program_id(1)

    @pl.when(j == 0)
    def _():
        x = x_ref[...]
        ms = jnp.mean(x * x, axis=-1, keepdims=True)
        xn_ref[...] = (x * lax.rsqrt(ms + EPS) * g_ref[...]).astype(BF16)

    acc = jnp.dot(xn_ref[...], w_ref[...], preferred_element_type=F32)

    @pl.when(j < n_a)
    def _():
        oa_ref[...] = acc

    @pl.when(j >= n_a)
    def _():
        ob_ref[...] = acc.astype(BF16)


def _in_proj(x2d, g, w_perm, *, layer, tm, tn):
    m, d = x2d.shape
    n_a, n_b = A_COLS // tn, B_COLS // tn
    return pl.pallas_call(
        functools.partial(_inproj_body, n_a=n_a),
        grid=(m // tm, n_a + n_b),
        in_specs=[
            pl.BlockSpec((tm, d), lambda i, j: (i, 0)),
            pl.BlockSpec((1, d), lambda i, j: (0, 0)),
            pl.BlockSpec((None, d, tn), lambda i, j: (layer, 0, j)),
        ],
        out_specs=[
            pl.BlockSpec((tm, tn), lambda i, j: (i, jnp.minimum(j, n_a - 1))),
            pl.BlockSpec((tm, tn), lambda i, j: (i, jnp.maximum(j - n_a, 0))),
        ],
        out_shape=[
            jax.ShapeDtypeStruct((m, A_COLS), F32),
            jax.ShapeDtypeStruct((m, B_COLS), BF16),
        ],
        scratch_shapes=[pltpu.VMEM((tm, d), BF16)],
        compiler_params=pltpu.CompilerParams(
            dimension_semantics=("parallel", "arbitrary"), vmem_limit_bytes=_vmem(48)),
        name="in_proj",
    )(x2d, g, w_perm)


def _prep_body(sm_ref, dc_ref, bias_ref, mul_ref, kvg_ref, col_ref, row_ref, c_ref, ct_ref, fkp_ref,
               carry_ref):
    t = pl.program_id(1)

    @pl.when(t == 0)
    def _():
        carry_ref[...] = jnp.zeros_like(carry_ref)

    tt = sm_ref.shape[1]
    s = sm_ref[0]
    lane = lax.broadcasted_iota(I32, (tt, LANES), 1)
    is_dt = lane < SM_F
    is_f = (lane >= SM_F) & (lane < SM_W)
    is_a = (lane >= SM_A) & (lane < SM_A + N_HEADS)
    sp, ls = _softplus_parts(s + bias_ref[...])
    v = jnp.where(is_dt, sp, jnp.where(is_f, ls, jnp.where(is_a, sp, s) * mul_ref[...]))
    ri = lax.broadcasted_iota(I32, (tt, tt), 0)
    ci = lax.broadcasted_iota(I32, (tt, tt), 1)
    tril = jnp.where(ci <= ri, 1.0, 0.0).astype(F32)
    local = jnp.dot(tril, v, precision=lax.Precision.HIGHEST, preferred_element_type=F32)
    out = jnp.where(is_f, local + carry_ref[...], jnp.where(is_a, local, v))
    col_ref[0] = out
    row_ref[0, 0] = out.T
    carry_ref[...] = jnp.where(is_f[0:1], out[tt - 1:tt, :], 0.0)

    f0 = jnp.where(is_f, out, 0.0)
    hi = f0.astype(BF16).astype(F32)
    r1 = f0 - hi
    mid = r1.astype(BF16).astype(F32)
    lo = (r1 - mid).astype(BF16).astype(F32)
    fkp_ref[0] = (hi + pltpu.roll(mid, N_HEADS, axis=1) + pltpu.roll(lo, 2 * N_HEADS, axis=1)).astype(BF16)

    dc = dc_ref[0]
    ms = jnp.mean(dc * dc, axis=-1, keepdims=True)
    cn = dc * lax.rsqrt(ms + EPS) * kvg_ref[...]
    c_ref[0] = cn.astype(BF16)
    ct_ref[0, 0] = jnp.concatenate([cn.T, jnp.ones((CT_ROWS - DSA_LATENT, tt), F32)], axis=0).astype(BF16)


def _prep(ua, bias_vec, mul_vec, kv_g):
    b, lp, _ = ua.shape
    tt = SEQ_TILE
    nblk = lp // tt
    return pl.pallas_call(
        _prep_body,
        grid=(b, nblk),
        in_specs=[
            pl.BlockSpec((1, tt, LANES), lambda i, t: (i, t, 17)),
            pl.BlockSpec((1, tt, LANES), lambda i, t: (i, t, 16)),
            pl.BlockSpec((1, LANES), lambda i, t: (0, 0)),
            pl.BlockSpec((1, LANES), lambda i, t: (0, 0)),
            pl.BlockSpec((1, LANES), lambda i, t: (0, 0)),
        ],
        out_specs=[
            pl.BlockSpec((1, tt, LANES), lambda i, t: (i, t, 0)),
            pl.BlockSpec((1, 1, LANES, tt), lambda i, t: (i, t, 0, 0)),
            pl.BlockSpec((1, tt, LANES), lambda i, t: (i, t, 0)),
            pl.BlockSpec((1, 1, CT_ROWS, tt), lambda i, t: (i, t, 0, 0)),
            pl.BlockSpec((1, tt, LANES), lambda i, t: (i, t, 0)),
        ],
        out_shape=[
            jax.ShapeDtypeStruct((b, lp, LANES), F32),
            jax.ShapeDtypeStruct((b, nblk, LANES, tt), F32),
            jax.ShapeDtypeStruct((b, lp, DSA_LATENT), BF16),
            jax.ShapeDtypeStruct((b, nblk, CT_ROWS, tt), BF16),
            jax.ShapeDtypeStruct((b, lp, LANES), BF16),
        ],
        scratch_shapes=[pltpu.VMEM((1, LANES), F32)],
        compiler_params=pltpu.CompilerParams(dimension_semantics=("parallel", "arbitrary")),
        name="prep",
    )(ua, ua, bias_vec, mul_vec, kv_g)


def _expand_heads(colv, base, lo_half):
    parts = []
    for p in range(N_HEADS // 2):
        a = colv[:, base + 2 * p:base + 2 * p + 1]
        b = colv[:, base + 2 * p + 1:base + 2 * p + 2]
        parts.append(jnp.where(lo_half, a, b))
    return jnp.concatenate(parts, axis=1)


def _ssd_body(xbc_ref, z_ref, col_ref, row_ref, cw_ref, cb_ref, dsk_ref, ng_ref, y_ref,
              xpad_ref, st_ref):
    c = pl.program_id(1)
    ll = xbc_ref.shape[1]
    gw = GROUP_WIDTH
    ns = SSD_STATE
    hpg = N_HEADS // SSD_GROUPS
    gcols = hpg * HEAD_DIM

    @pl.when(c == 0)
    def _():
        xpad_ref[0:8, :] = jnp.zeros((8, SSD_XBC), F32)
        st_ref[...] = jnp.zeros_like(st_ref)

    x = xbc_ref[0]
    xpad_ref[8:8 + ll, :] = x
    conv = (cb_ref[...]
            + cw_ref[0:1, :] * xpad_ref[5:5 + ll, :]
            + cw_ref[1:2, :] * xpad_ref[6:6 + ll, :]
            + cw_ref[2:3, :] * xpad_ref[7:7 + ll, :]
            + cw_ref[3:4, :] * x)
    xpad_ref[0:8, :] = x[ll - 8:ll, :]
    act = _silu(conv)
    xs = act[:, 0:gw]
    bm = act[:, gw:gw + SSD_GROUPS * ns]
    cm = act[:, gw + SSD_GROUPS * ns:]

    colv = col_ref[0]
    rowv = row_ref[0, 0]
    lo_half = lax.broadcasted_iota(I32, (ll, LANES), 1) < HEAD_DIM
    dt_full = _expand_heads(colv, SM_DT, lo_half)
    acs_full = _expand_heads(colv, SM_A, lo_half)
    acs_last = acs_full[ll - 1:ll, :]
    dte_full = jnp.exp(acs_last - acs_full)
    dfs_full = jnp.exp(acs_full)
    xdt = xs * dt_full
    xdt_b = xdt.astype(BF16)
    xdte_b = (xdt * dte_full).astype(BF16)
    cm_b = cm.astype(BF16)
    ri = lax.broadcasted_iota(I32, (ll, ll), 0)
    ci = lax.broadcasted_iota(I32, (ll, ll), 1)
    tril = ci <= ri

    ys = []
    for g in range(SSD_GROUPS):
        bg = bm[:, ns * g:ns * (g + 1)]
        bg_b = bg.astype(BF16)
        bgt_b = bg.T.astype(BF16)
        cg_b = cm_b[:, ns * g:ns * (g + 1)]
        cb = lax.dot_general(cg_b, bg_b, (((1,), (1,)), ((), ())), preferred_element_type=F32)
        sg = st_ref[g]
        yoff = (jnp.dot(cg_b, sg.astype(BF16), preferred_element_type=F32)
                * dfs_full[:, gcols * g:gcols * (g + 1)])
        parts = []
        for pr in range(hpg // 2):
            xpair = xdt_b[:, gcols * g + LANES * pr:gcols * g + LANES * (pr + 1)]
            res = []
            for hh in range(2):
                h = hpg * g + 2 * pr + hh
                seg = colv[:, SM_A + h:SM_A + h + 1] - rowv[SM_A + h:SM_A + h + 1, :]
                lm = jnp.exp(jnp.where(tril, seg, -jnp.inf))
                res.append(jnp.dot((cb * lm).astype(BF16), xpair, preferred_element_type=F32))
            parts.append(jnp.where(lo_half, res[0], res[1]))
        ydiag = jnp.concatenate(parts, axis=1)
        decay = jnp.exp(acs_last[:, gcols * g:gcols * (g + 1)])
        st_ref[g] = decay * sg + jnp.dot(bgt_b, xdte_b[:, gcols * g:gcols * (g + 1)],
                                         preferred_element_type=F32)
        ys.append(ydiag + yoff)

    y = jnp.concatenate(ys, axis=1) + dsk_ref[...] * xs
    gz = y * _silu(z_ref[0])
    outs = []
    for g in range(SSD_GROUPS):
        gg = gz[:, gcols * g:gcols * (g + 1)]
        outs.append(gg * lax.rsqrt(jnp.mean(gg * gg, axis=-1, keepdims=True) + EPS))
    y_ref[0] = (jnp.concatenate(outs, axis=1) * ng_ref[...]).astype(BF16)


def _ssd(ua, col, row, conv_w, conv_b, dskip_full, norm_g):
    b, lp, _ = ua.shape
    ll = SEQ_TILE
    nblk = lp // ll
    hpg = N_HEADS // SSD_GROUPS
    return pl.pallas_call(
        _ssd_body,
        grid=(b, nblk),
        in_specs=[
            pl.BlockSpec((1, ll, SSD_XBC), lambda i, c: (i, c, 0)),
            pl.BlockSpec((1, ll, GROUP_WIDTH), lambda i, c: (i, c, 2)),
            pl.BlockSpec((1, ll, LANES), lambda i, c: (i, c, 0)),
            pl.BlockSpec((1, 1, LANES, ll), lambda i, c: (i, c, 0, 0)),
            pl.BlockSpec((SSD_CONV, SSD_XBC), lambda i, c: (0, 0)),
            pl.BlockSpec((1, SSD_XBC), lambda i, c: (0, 0)),
            pl.BlockSpec((1, GROUP_WIDTH), lambda i, c: (0, 0)),
            pl.BlockSpec((1, GROUP_WIDTH), lambda i, c: (0, 0)),
        ],
        out_specs=pl.BlockSpec((1, ll, GROUP_WIDTH), lambda i, c: (i, c, 0)),
        out_shape=jax.ShapeDtypeStruct((b, lp, GROUP_WIDTH), BF16),
        scratch_shapes=[
            pltpu.VMEM((8 + ll, SSD_XBC), F32),
            pltpu.VMEM((SSD_GROUPS, SSD_STATE, hpg * HEAD_DIM), F32),
        ],
        compiler_params=pltpu.CompilerParams(dimension_semantics=("parallel", "arbitrary")),
        name="ssd",
    )(ua, ua, col, row, conv_w, conv_b, dskip_full, norm_g)


def _pool_body(u_ref, w_ref, sc_ref, y_ref, buf_ref):
    t = pl.program_id(1)
    tt = u_ref.shape[1]
    hist = max(POOL_WINDOWS)
    gd = GROUP_WIDTH // len(POOL_WINDOWS)

    @pl.when(t == 0)
    def _():
        buf_ref[0:hist, :] = jnp.zeros((hist, GROUP_WIDTH), F32)

    u = u_ref[0]
    buf_ref[hist:hist + tt, :] = u
    count = (t * tt + 1 + lax.broadcasted_iota(I32, (tt, 1), 0)).astype(F32)
    outs = []
    for gi, win in enumerate(POOL_WINDOWS):
        ug = u[:, gd * gi:gd * (gi + 1)]
        acc = ug
        for k in range(1, win):
            acc = acc + buf_ref[hist - k:hist - k + tt, gd * gi:gd * (gi + 1)]
        pooled = acc / jnp.minimum(count, float(win)) - ug
        outs.append(jnp.dot(pooled.astype(BF16), w_ref[gi], preferred_element_type=F32))
    y_ref[0] = (jnp.concatenate(outs, axis=1) * sc_ref[...]).astype(BF16)
    buf_ref[0:hist, :] = u[tt - hist:tt, :]


def _pool(ua, w, scale):
    b, lp, _ = ua.shape
    tt = SEQ_TILE
    ng = len(POOL_WINDOWS)
    gd = GROUP_WIDTH // ng
    return pl.pallas_call(
        _pool_body,
        grid=(b, lp // tt),
        in_specs=[
            pl.BlockSpec((1, tt, GROUP_WIDTH), lambda i, t: (i, t, 3)),
            pl.BlockSpec((ng, gd, gd), lambda i, t: (0, 0, 0)),
            pl.BlockSpec((1, GROUP_WIDTH), lambda i, t: (0, 0)),
        ],
        out_specs=pl.BlockSpec((1, tt, GROUP_WIDTH), lambda i, t: (i, t, 0)),
        out_shape=jax.ShapeDtypeStruct((b, lp, GROUP_WIDTH), BF16),
        scratch_shapes=[pltpu.VMEM((max(POOL_WINDOWS) + tt, GROUP_WIDTH), F32)],
        compiler_params=pltpu.CompilerParams(dimension_semantics=("parallel", "arbitrary")),
        name="pool",
    )(ua, w, scale)


def _fox_body(q_ref, k_ref, v_ref, fkp_ref, o_ref, qa_ref, vl_ref, acc_ref, m_ref):
    qi = pl.program_id(1)
    tq = q_ref.shape[1]
    tk = tq
    nt = (((1,), (1,)), ((), ()))
    npair = N_HEADS // 2
    lane = lax.broadcasted_iota(I32, (tq, LANES), 1)
    lo_half = lane < HEAD_DIM
    zero_b = jnp.zeros((tq, LANES), BF16)

    @pl.when(qi == 0)
    def _():
        nkt = vl_ref.shape[0]
        rsel = lax.broadcasted_iota(I32, (ACC_ROWS - 2 * HEAD_DIM, 2 * tk), 0)
        csel = lax.broadcasted_iota(I32, (ACC_ROWS - 2 * HEAD_DIM, 2 * tk), 1)
        ones_rows = jnp.where((rsel == 0) & (csel < tk) | (rsel == 1) & (csel >= tk), 1.0, 0.0).astype(BF16)
        zpad = jnp.zeros((HEAD_DIM, tk), BF16)

        def fill(kt, carry):
            ks = pl.multiple_of(kt * tk, tk)
            for pr in range(npair):
                vt = v_ref[0, pl.ds(ks, tk), LANES * pr:LANES * (pr + 1)].astype(F32).T.astype(BF16)
                top = jnp.concatenate([vt[:HEAD_DIM], zpad], axis=1)
                bot = jnp.concatenate([zpad, vt[HEAD_DIM:]], axis=1)
                vl_ref[kt, pr] = jnp.concatenate([top, bot, ones_rows], axis=0)
            return carry

        lax.fori_loop(0, nkt, fill, 0)

    for pr in range(npair):
        q2 = q_ref[0, :, LANES * pr:LANES * (pr + 1)] * jnp.asarray(HEAD_DIM ** -0.5, BF16)
        for hh in range(2):
            h = 2 * pr + hh
            qm = jnp.where(lo_half, q2, zero_b) if hh == 0 else jnp.where(lo_half, zero_b, q2)
            pick = (lane == SM_F + h) | (lane == SM_F + N_HEADS + h) | (lane == SM_F + 2 * N_HEADS + h)
            qa_ref[h] = jnp.concatenate([qm, jnp.where(pick, -1.0, 0.0).astype(BF16)], axis=1)
    acc_ref[...] = jnp.zeros_like(acc_ref)
    m_ref[...] = jnp.full(m_ref.shape, NEG_BIG, F32)
    causal_t = (lax.broadcasted_iota(I32, (tk, tq), 0) <= lax.broadcasted_iota(I32, (tk, tq), 1))
    tail_row = lax.broadcasted_iota(I32, (ACC_ROWS - 2 * HEAD_DIM, tq), 0)

    def step(kt, masked):
        ks = pl.multiple_of(kt * tk, tk)
        fkp = fkp_ref[0, pl.ds(ks, tk), :]
        for pr in range(npair):
            ka = jnp.concatenate([k_ref[0, pl.ds(ks, tk), LANES * pr:LANES * (pr + 1)], fkp], axis=1)
            ps, als = [], []
            for hh in range(2):
                h = 2 * pr + hh
                st = lax.dot_general(ka, qa_ref[h], nt, preferred_element_type=F32)
                if masked:
                    st = jnp.where(causal_t, st, -jnp.inf)
                m_old = m_ref[h]
                mn = jnp.maximum(m_old, jnp.max(st, axis=0, keepdims=True))
                ps.append(jnp.exp(st - mn).astype(BF16))
                als.append(jnp.exp(m_old - mn))
                m_ref[h] = mn
            upd = jnp.dot(vl_ref[kt, pr], jnp.concatenate(ps, axis=0), preferred_element_type=F32)
            scale = jnp.concatenate([jnp.broadcast_to(als[0], (HEAD_DIM, tq)),
                                     jnp.broadcast_to(als[1], (HEAD_DIM, tq)),
                                     jnp.where(tail_row == 0, als[0], als[1])], axis=0)
            acc_ref[pr] = scale * acc_ref[pr] + upd

    def body(kt, carry):
        step(kt, False)
        return carry

    lax.fori_loop(0, qi, body, 0)
    step(qi, True)
    for pr in range(npair):
        a = acc_ref[pr]
        o = jnp.concatenate([a[:HEAD_DIM] / a[2 * HEAD_DIM:2 * HEAD_DIM + 1],
                             a[HEAD_DIM:2 * HEAD_DIM] / a[2 * HEAD_DIM + 1:2 * HEAD_DIM + 2]], axis=0)
        o_ref[0, :, LANES * pr:LANES * (pr + 1)] = o.T.astype(BF16)


def _fox(ub, fkp, *, sub):
    b, lp, _ = ub.shape
    tq = sub * SEQ_TILE
    return pl.pallas_call(
        _fox_body,
        grid=(b, lp // tq),
        in_specs=[
            pl.BlockSpec((1, tq, GROUP_WIDTH), lambda i, q: (i, q, 0)),
            pl.BlockSpec((1, lp, GROUP_WIDTH), lambda i, q: (i, 0, 1)),
            pl.BlockSpec((1, lp, GROUP_WIDTH), lambda i, q: (i, 0, 2)),
            pl.BlockSpec((1, lp, LANES), lambda i, q: (i, 0, 0)),
        ],
        out_specs=pl.BlockSpec((1, tq, GROUP_WIDTH), lambda i, q: (i, q, 0)),
        out_shape=jax.ShapeDtypeStruct((b, lp, GROUP_WIDTH), BF16),
        scratch_shapes=[
            pltpu.VMEM((N_HEADS, tq, 2 * LANES), BF16),
            pltpu.VMEM((lp // tq, N_HEADS // 2, ACC_ROWS, 2 * tq), BF16),
            pltpu.VMEM((N_HEADS // 2, ACC_ROWS, tq), F32),
            pltpu.VMEM((N_HEADS, 1, tq), F32),
        ],
        compiler_params=pltpu.CompilerParams(
            dimension_semantics=("parallel", "arbitrary"), vmem_limit_bytes=_vmem(48)),
        name="fox",
    )(ub, ub, ub, fkp)


def _dsa_body(dq_ref, dqi_ref, ki_ref, c_ref, ct_ref, wr_ref, wuk_ref, wuvt_ref, o_ref,
              key_ref, bias_ref, qm_ref, qlat_ref, acc_ref, m_ref, *, topk, sub):
    qi = pl.program_id(1)
    tq = dq_ref.shape[1]
    tk = tq
    ngrp = jnp.minimum(qi + 2, key_ref.shape[0])
    nt = (((1,), (1,)), ((), ()))
    lo_half = lax.broadcasted_iota(I32, (tq, LANES), 1) < IDX_DIM
    zero_b = jnp.zeros((tq, LANES), BF16)
    shift = CHUNK - N_META
    lg2 = CHUNK.bit_length() - 1
    qcid = (qi * tq + lax.broadcasted_iota(I32, (tk, tq), 1) + shift) >> lg2
    krow = lax.broadcasted_iota(I32, (tk, tq), 0)
    kf = float(topk)

    def fold8(w):
        return jnp.sum(w.reshape(tk // 8, 8, tq), axis=0)

    for h in range(IDX_HEADS):
        q2 = dqi_ref[0, :, LANES * (h // 2):LANES * (h // 2 + 1)]
        qm_ref[tq * h:tq * (h + 1), :] = (jnp.where(lo_half, q2, zero_b) if h % 2 == 0
                                           else jnp.where(lo_half, zero_b, q2))
    wrows = jnp.concatenate([wr_ref[0, j] for j in range(sub)], axis=1)

    def score_step(g, carry):
        ks = pl.multiple_of(g * tk, tk)
        kt = ki_ref[0, pl.ds(ks, tk), :]
        lg_all = lax.dot_general(kt, qm_ref[...], nt, preferred_element_type=F32)
        sc = None
        for h in range(IDX_HEADS):
            term = wrows[h:h + 1, :] * jnp.maximum(lg_all[:, tq * h:tq * (h + 1)], 0.0)
            sc = term if sc is None else sc + term
        bits = pltpu.bitcast(sc, I32)
        bits = jnp.where(bits == INT_MIN, 0, bits)
        skey = bits ^ ((bits >> 31) & 0x7FFFFFFF)
        kcid = (ks + krow + shift) >> lg2
        key_ref[g] = jnp.where(kcid <= qcid, skey, INT_MIN)
        return carry

    lax.fori_loop(0, ngrp, score_step, 0)

    def bit_step(i, u):
        uc = u | lax.shift_left(jnp.int32(1), 31 - i)
        sc = uc ^ INT_MIN

        def cnt_step(g, acc):
            return acc + fold8(jnp.where(key_ref[g] >= sc, 1.0, 0.0))

        acc = lax.fori_loop(0, ngrp, cnt_step, jnp.zeros((8, tq), F32))
        cnt = jnp.sum(acc, axis=0, keepdims=True)
        return jnp.where(cnt >= kf, uc, u)

    thr = lax.fori_loop(0, 32, bit_step, jnp.zeros((1, tq), I32)) ^ INT_MIN

    def gt_step(g, acc):
        return acc + fold8(jnp.where(key_ref[g] > thr, 1.0, 0.0))

    ngt = jnp.sum(lax.fori_loop(0, ngrp, gt_step, jnp.zeros((8, tq), F32)), axis=0, keepdims=True)
    room = kf - ngt
    incl = jnp.where(lax.broadcasted_iota(I32, (tk, tk), 1) <= lax.broadcasted_iota(I32, (tk, tk), 0),
                     1.0, 0.0).astype(BF16)

    def mask_step(g, seen):
        key = key_ref[g]
        eqf = jnp.where(key == thr, 1.0, 0.0)
        rank = jnp.dot(incl, eqf.astype(BF16), preferred_element_type=F32) + seen
        tie = jnp.where(rank <= room, eqf, 0.0)
        sel = jnp.where(key > thr, 1.0, tie)
        sel = jnp.where(key == INT_MIN, 0.0, sel)
        bias_ref[g] = jnp.where(sel > 0.5, 0.0, -jnp.inf)
        return seen + jnp.sum(fold8(eqf), axis=0, keepdims=True)

    lax.fori_loop(0, ngrp, mask_step, jnp.zeros((1, tq), F32))

    for h in range(N_HEADS):
        dq2 = dq_ref[0, :, LANES * (h // 2):LANES * (h // 2 + 1)]
        ql = jnp.dot(dq2, wuk_ref[h], preferred_element_type=F32) * (HEAD_DIM ** -0.5)
        qlat_ref[tq * h:tq * (h + 1), :] = ql.astype(BF16)
    acc_ref[...] = jnp.zeros_like(acc_ref)
    m_ref[...] = jnp.full(m_ref.shape, NEG_BIG, F32)

    def att_step(g, carry):
        ks = pl.multiple_of(g * tk, tk)
        ck = c_ref[0, pl.ds(ks, tk), :]
        cx = jnp.concatenate([ct_ref[0, g * sub + j] for j in range(sub)], axis=1)
        bias = bias_ref[g]
        st_all = lax.dot_general(ck, qlat_ref[...], nt, preferred_element_type=F32)
        ps, als = [], []
        for h in range(N_HEADS):
            st = st_all[:, tq * h:tq * (h + 1)] + bias
            m_old = m_ref[h]
            mn = jnp.maximum(m_old, jnp.max(st, axis=0, keepdims=True))
            ps.append(jnp.exp(st - mn).astype(BF16))
            als.append(jnp.exp(m_old - mn))
            m_ref[h] = mn
        upd = jnp.dot(cx, jnp.concatenate(ps, axis=1), preferred_element_type=F32)
        acc_ref[...] = jnp.concatenate(als, axis=1) * acc_ref[...] + upd
        return carry

    lax.fori_loop(0, ngrp, att_step, 0)

    for pr in range(N_HEADS // 2):
        outs = []
        for hh in range(2):
            a = acc_ref[:, tq * (2 * pr + hh):tq * (2 * pr + hh + 1)]
            olat = (a[:DSA_LATENT] / a[DSA_LATENT:DSA_LATENT + 1]).astype(BF16)
            outs.append(jnp.dot(wuvt_ref[2 * pr + hh], olat, preferred_element_type=F32))
        o_ref[0, :, LANES * pr:LANES * (pr + 1)] = jnp.concatenate(outs, axis=0).T.astype(BF16)


def _dsa(ub, c, ct, rowt, wuk_pad, wuv_t, *, topk, sub):
    b, lp, _ = ub.shape
    tq = sub * SEQ_TILE
    ngrp = lp // tq
    nblk = lp // SEQ_TILE
    return pl.pallas_call(
        functools.partial(_dsa_body, topk=topk, sub=sub),
        grid=(b, ngrp),
        in_specs=[
            pl.BlockSpec((1, tq, GROUP_WIDTH), lambda i, q: (i, q, 3)),
            pl.BlockSpec((1, tq, IDX_HEADS * IDX_DIM), lambda i, q: (i, q, 8)),
            pl.BlockSpec((1, lp, LANES), lambda i, q: (i, 0, 18)),
            pl.BlockSpec((1, lp, DSA_LATENT), lambda i, q: (i, 0, 0)),
            pl.BlockSpec((1, nblk, CT_ROWS, SEQ_TILE), lambda i, q: (i, 0, 0, 0)),
            pl.BlockSpec((1, sub, N_HEADS, SEQ_TILE), lambda i, q: (i, q, SM_W // N_HEADS, 0)),
            pl.BlockSpec((N_HEADS, LANES, DSA_LATENT), lambda i, q: (0, 0, 0)),
            pl.BlockSpec((N_HEADS, HEAD_DIM, DSA_LATENT), lambda i, q: (0, 0, 0)),
        ],
        out_specs=pl.BlockSpec((1, tq, GROUP_WIDTH), lambda i, q: (i, q, 0)),
        out_shape=jax.ShapeDtypeStruct((b, lp, GROUP_WIDTH), BF16),
        scratch_shapes=[
            pltpu.VMEM((ngrp, tq, tq), I32),
            pltpu.VMEM((ngrp, tq, tq), F32),
            pltpu.VMEM((IDX_HEADS * tq, LANES), BF16),
            pltpu.VMEM((N_HEADS * tq, DSA_LATENT), BF16),
            pltpu.VMEM((CT_ROWS, N_HEADS * tq), F32),
            pltpu.VMEM((N_HEADS, 1, tq), F32),
        ],
        compiler_params=pltpu.CompilerParams(
            dimension_semantics=("parallel", "arbitrary"), vmem_limit_bytes=_vmem(48)),
        name="dsa",
    )(ub, ub, ub, c, ct, rowt, wuk_pad, wuv_t)


def _outproj_body(ya_ref, yb_ref, yc_ref, yd_ref, x_ref, w_ref, g_ref, o_ref):
    acc = jnp.dot(ya_ref[...], w_ref[0], preferred_element_type=F32)
    acc = acc + jnp.dot(yb_ref[...], w_ref[1], preferred_element_type=F32)
    acc = acc + jnp.dot(yc_ref[...], w_ref[2], preferred_element_type=F32)
    acc = acc + jnp.dot(yd_ref[...], w_ref[3], preferred_element_type=F32)
    ms = jnp.mean(acc * acc, axis=-1, keepdims=True)
    o_ref[...] = x_ref[...] + acc * lax.rsqrt(ms + EPS) * g_ref[...]


def _out_proj(ys, x2d, w4, g, *, layer, tm):
    m, d = x2d.shape
    gw = GROUP_WIDTH
    yspec = pl.BlockSpec((tm, gw), lambda i: (i, 0))
    return pl.pallas_call(
        _outproj_body,
        grid=(m // tm,),
        in_specs=[yspec, yspec, yspec, yspec,
                  pl.BlockSpec((tm, d), lambda i: (i, 0)),
                  pl.BlockSpec((None, 4, gw, d), lambda i: (layer, 0, 0, 0)),
                  pl.BlockSpec((1, d), lambda i: (0, 0))],
        out_specs=pl.BlockSpec((tm, d), lambda i: (i, 0)),
        out_shape=jax.ShapeDtypeStruct((m, d), F32),
        compiler_params=pltpu.CompilerParams(
            dimension_semantics=("parallel",), vmem_limit_bytes=_vmem(48)),
        name="out_proj",
    )(*ys, x2d, w4, g)


def _ffn_body(x_ref, gpre_ref, wg_ref, wu_ref, cw_ref, cb_ref, wd_ref, gpost_ref, o_ref,
              xn_ref, acc_ref, gbuf_ref, carry_ref, *, tiles_per_seq):
    i = pl.program_id(0)
    f = pl.program_id(1)
    nf = pl.num_programs(1)
    tm = x_ref.shape[0]

    @pl.when(f == 0)
    def _():
        x = x_ref[...]
        ms = jnp.mean(x * x, axis=-1, keepdims=True)
        xn_ref[...] = (x * lax.rsqrt(ms + EPS) * gpre_ref[...]).astype(BF16)

    @pl.when(i % tiles_per_seq == 0)
    def _():
        carry_ref[f] = jnp.zeros(carry_ref.shape[1:], F32)

    xn = xn_ref[...]
    g = jnp.dot(xn, wg_ref[...], preferred_element_type=F32)
    u = jnp.dot(xn, wu_ref[...], preferred_element_type=F32)
    gbuf_ref[0:8, :] = carry_ref[f]
    gbuf_ref[8:8 + tm, :] = g
    conv = (cb_ref[...]
            + cw_ref[0:1, :] * gbuf_ref[6:6 + tm, :]
            + cw_ref[1:2, :] * gbuf_ref[7:7 + tm, :]
            + cw_ref[2:3, :] * g)
    carry_ref[f] = g[tm - 8:tm, :]
    a = (_silu(conv) * u).astype(BF16)
    part = jnp.dot(a, wd_ref[...], preferred_element_type=F32)

    @pl.when(f == 0)
    def _():
        acc_ref[...] = part

    @pl.when(f > 0)
    def _():
        acc_ref[...] += part

    @pl.when(f == nf - 1)
    def _():
        y = acc_ref[...]
        ms = jnp.mean(y * y, axis=-1, keepdims=True)
        o_ref[...] = x_ref[...] + y * lax.rsqrt(ms + EPS) * gpost_ref[...]


def _ffn(x2d, g_pre, w_gate, w_up, conv_w, conv_b, w_down, g_post, *, layer, tm, tf, tiles_per_seq):
    m, d = x2d.shape
    fdim = w_gate.shape[-1]
    nf = fdim // tf
    return pl.pallas_call(
        functools.partial(_ffn_body, tiles_per_seq=tiles_per_seq),
        grid=(m // tm, nf),
        in_specs=[
            pl.BlockSpec((tm, d), lambda i, f: (i, 0)),
            pl.BlockSpec((1, d), lambda i, f: (0, 0)),
            pl.BlockSpec((None, d, tf), lambda i, f: (layer, 0, f)),
            pl.BlockSpec((None, d, tf), lambda i, f: (layer, 0, f)),
            pl.BlockSpec((FFN_CONV, tf), lambda i, f: (0, f)),
            pl.BlockSpec((1, tf), lambda i, f: (0, f)),
            pl.BlockSpec((None, tf, d), lambda i, f: (layer, f, 0)),
            pl.BlockSpec((1, d), lambda i, f: (0, 0)),
        ],
        out_specs=pl.BlockSpec((tm, d), lambda i, f: (i, 0)),
        out_shape=jax.ShapeDtypeStruct((m, d), F32),
        scratch_shapes=[
            pltpu.VMEM((tm, d), BF16),
            pltpu.VMEM((tm, d), F32),
            pltpu.VMEM((8 + tm, tf), F32),
            pltpu.VMEM((nf, 8, tf), F32),
        ],
        compiler_params=pltpu.CompilerParams(
            dimension_semantics=("arbitrary", "arbitrary"), vmem_limit_bytes=_vmem(56)),
        name="ffn",
    )(x2d, g_pre, w_gate, w_up, conv_w, conv_b, w_down, g_post)


def _permute_w_in(w_in):
    gw = GROUP_WIDTH
    sizes = (gw, SSD_XBC, N_HEADS, gw, 3 * gw, N_HEADS, gw, DSA_LATENT, IDX_HEADS * IDX_DIM, IDX_DIM, IDX_HEADS)
    offs = [0]
    for s in sizes:
        offs.append(offs[-1] + s)
    z, xbc, dt, pool, qkv, fl, dq, dc, dqi, dki, dwi = (w_in[..., offs[k]:offs[k + 1]] for k in range(len(sizes)))
    zeros = lambda n: jnp.zeros(w_in.shape[:-1] + (n,), w_in.dtype)
    small = jnp.concatenate([dt, fl, dwi, zeros(SM_A - SM_W - IDX_HEADS), dt, zeros(LANES - SM_A - N_HEADS)], axis=-1)
    a = jnp.concatenate([xbc, z, pool, dc, small, zeros(A_COLS - 2 * gw - SSD_XBC - DSA_LATENT - LANES)], axis=-1)
    bcols = jnp.concatenate([qkv, dq, dqi, dki, dki], axis=-1)
    bcols = jnp.concatenate([bcols, zeros(B_COLS - bcols.shape[-1])], axis=-1)
    return jnp.concatenate([a, bcols], axis=-1)


def _lane_vec(pieces):
    v = jnp.zeros((LANES,), F32)
    for off, val in pieces:
        v = v.at[off:off + val.shape[0]].set(val.astype(F32))
    return v[None, :]


def _pad_head_weights(w_uk, w_uv):
    h, r, d = w_uk.shape
    uk = jnp.zeros((h, 2 * d, r), F32)
    for i in range(h):
        o = d * (i % 2)
        uk = uk.at[i, o:o + d, :].set(w_uk[i].T)
    return uk.astype(BF16), jnp.swapaxes(w_uv, 1, 2).astype(BF16)


def _tile_sizes(b, lp):
    m = b * lp
    tm_proj = next(t for t in (1024, 512, 256, 128) if m % t == 0)
    tm_out = next(t for t in (512, 256, 128) if m % t == 0)
    tm_ffn = next(t for t in (528, 384, 320, 256, 128) if lp % t == 0)
    att_sub = 3 if lp % (3 * SEQ_TILE) == 0 else 1
    return tm_proj, tm_out, tm_ffn, att_sub


def _layer(h, p, big, layer, *, topk):
    b, lp, d = h.shape
    m = b * lp
    tm_proj, tm_out, tm_ffn, att_sub = _tile_sizes(b, lp)
    row = lambda v: v.astype(F32)[None, :]

    ua, ub = _in_proj(h.reshape(m, d), row(p["norm_mix_pre"]), big["w_in"], layer=layer, tm=tm_proj, tn=640)
    ua = ua.reshape(b, lp, A_COLS)
    ub = ub.reshape(b, lp, B_COLS)

    bias_vec = _lane_vec([(SM_DT, p["ssd_dt_bias"]), (SM_F, p["fox_f_bias"]), (SM_A, p["ssd_dt_bias"])])
    wscale = jnp.full((IDX_HEADS,), (IDX_HEADS ** -0.5) * (IDX_DIM ** -0.5), F32)
    mul_vec = _lane_vec([(SM_W, wscale), (SM_A, -jnp.exp(p["ssd_a_log"].astype(F32)))])
    col, rowt, c, ct, fkp = _prep(ua, bias_vec, mul_vec, row(p["dsa_kv_norm"]))

    dskip_full = jnp.repeat(p["ssd_d"].astype(F32), HEAD_DIM)[None, :]
    y_a = _ssd(ua, col, rowt, p["ssd_conv_w"].astype(F32), row(p["ssd_conv_b"]), dskip_full, row(p["ssd_norm"]))
    y_b = _pool(ua, p["pool_w"].astype(BF16), row(p["pool_scale"]))
    y_c = _fox(ub, fkp, sub=att_sub)
    wuk_pad, wuv_t = _pad_head_weights(p["dsa_w_uk"], p["dsa_w_uv"])
    y_d = _dsa(ub, c, ct, rowt, wuk_pad, wuv_t, topk=topk, sub=att_sub)

    ys = [y.reshape(m, GROUP_WIDTH) for y in (y_a, y_b, y_c, y_d)]
    x1 = _out_proj(ys, h.reshape(m, d), big["w_out"], row(p["norm_mix_post"]), layer=layer, tm=tm_out)
    x2 = _ffn(x1, row(p["norm_ffn_pre"]), big["ffn_w_gate"], big["ffn_w_up"],
              p["ffn_conv_w"].astype(F32), row(p["ffn_conv_b"]), big["ffn_w_down"],
              row(p["norm_ffn_post"]), layer=layer, tm=tm_ffn, tf=512, tiles_per_seq=lp // tm_ffn)
    return x2.reshape(b, lp, d)


def _stack_big_weights(w_in, w_out, ffn_w_gate, ffn_w_up, ffn_w_down):
    depth, d, _ = w_out.shape
    return dict(w_in=_to_bf16(_permute_w_in(w_in)),
                w_out=_to_bf16(w_out).reshape(depth, 4, GROUP_WIDTH, d),
                ffn_w_gate=_to_bf16(ffn_w_gate), ffn_w_up=_to_bf16(ffn_w_up), ffn_w_down=_to_bf16(ffn_w_down))


def kernel(x, meta_tokens, norm_mix_pre, norm_mix_post, norm_ffn_pre, norm_ffn_post, w_in, ssd_conv_w, ssd_conv_b, ssd_dt_bias, ssd_a_log, ssd_d, ssd_norm, pool_w, pool_scale, fox_f_bias, dsa_kv_norm, dsa_w_uk, dsa_w_uv, w_out, ffn_w_gate, ffn_w_up, ffn_conv_w, ffn_conv_b, ffn_w_down):
    bsz, seq, d = x.shape
    n = N_META + seq
    lp = -(-n // SEQ_TILE) * SEQ_TILE
    topk = min(DSA_TOPK_MAX, seq // 4)
    meta = jnp.broadcast_to(meta_tokens.astype(x.dtype)[None], (bsz, N_META, d))
    h = jnp.concatenate([meta, x, jnp.zeros((bsz, lp - n, d), x.dtype)], axis=1)
    small = dict(norm_mix_pre=norm_mix_pre, norm_mix_post=norm_mix_post, norm_ffn_pre=norm_ffn_pre,
                 norm_ffn_post=norm_ffn_post, ssd_conv_w=ssd_conv_w, ssd_conv_b=ssd_conv_b,
                 ssd_dt_bias=ssd_dt_bias, ssd_a_log=ssd_a_log, ssd_d=ssd_d, ssd_norm=ssd_norm,
                 pool_w=pool_w, pool_scale=pool_scale, fox_f_bias=fox_f_bias, dsa_kv_norm=dsa_kv_norm,
                 dsa_w_uk=dsa_w_uk, dsa_w_uv=dsa_w_uv, ffn_conv_w=ffn_conv_w, ffn_conv_b=ffn_conv_b)
    big = _stack_big_weights(w_in, w_out, ffn_w_gate, ffn_w_up, ffn_w_down)
    for i in range(norm_mix_pre.shape[0]):
        h = _layer(h, {k: v[i] for k, v in small.items()}, big, i, topk=topk)
    return h[:, N_META:n]
```

```python
import functools

import jax
import jax.numpy as jnp
from jax import lax
from jax.experimental import pallas as pl
from jax.experimental.pallas import tpu as pltpu

F32 = jnp.float32
BF16 = jnp.bfloat16
I32 = jnp.int32

EPS = 1e-6
N_META = 16
CHUNK = 64
HEAD_DIM = 64
GROUP_WIDTH = 512
N_HEADS = 8
SSD_GROUPS = 2
SSD_STATE = 128
SSD_CONV = 4
SSD_XBC = GROUP_WIDTH + 2 * SSD_GROUPS * SSD_STATE
POOL_WINDOWS = (2, 4, 8, 16)
DSA_LATENT = 128
IDX_HEADS = 4
IDX_DIM = 64
DSA_TOPK_MAX = 256
FFN_CONV = 3

LANES = 128
SEQ_TILE = 128
INT_MIN = -(2 ** 31)
NEG_BIG = -1e30

A_COLS = 2560
B_COLS = 2560
SM_DT = 0
SM_F = 8
SM_W = 16
SM_A = 24
CT_ROWS = DSA_LATENT + 16
ACC_ROWS = 2 * HEAD_DIM + 16
TAIL = N_META


def _vmem(mb):
    return int(mb * 1024 * 1024)


def _softplus_parts(x):
    t = jnp.log1p(jnp.exp(-jnp.abs(x)))
    return jnp.maximum(x, 0.0) + t, jnp.minimum(x, 0.0) - t


def _silu(x):
    return x / (1.0 + jnp.exp(-x))


def _cast_body(x_ref, o_ref):
    o_ref[...] = x_ref[...].astype(BF16)


def _to_bf16(w, *, tr=256):
    shape = w.shape
    w2 = w.reshape(-1, shape[-1])
    r, c = w2.shape
    out = pl.pallas_call(
        _cast_body,
        grid=(r // tr,),
        in_specs=[pl.BlockSpec((tr, c), lambda i: (i, 0))],
        out_specs=pl.BlockSpec((tr, c), lambda i: (i, 0)),
        out_shape=jax.ShapeDtypeStruct((r, c), BF16),
        compiler_params=pltpu.CompilerParams(dimension_semantics=("parallel",), vmem_limit_bytes=_vmem(40)),
        name="to_bf16",
    )(w2)
    return out.reshape(shape)


def _inproj_body(x_ref, g_ref, w_ref, oa_ref, ob_ref, xn_ref, *, n_a):
    j = pl.program_id(1)

    @pl.when(j == 0)
    def _():
        x = x_ref[...]
        ms = jnp.mean(x * x, axis=-1, keepdims=True)
        xn_ref[...] = (x * lax.rsqrt(ms + EPS) * g_ref[...]).astype(BF16)

    acc = jnp.dot(xn_ref[...], w_ref[...], preferred_element_type=F32)

    @pl.when(j < n_a)
    def _():
        oa_ref[...] = acc

    @pl.when(j >= n_a)
    def _():
        ob_ref[...] = acc.astype(BF16)


def _in_proj(x2d, g, w_perm, *, layer, tm, tn):
    m, d = x2d.shape
    n_a, n_b = A_COLS // tn, B_COLS // tn
    return pl.pallas_call(
        functools.partial(_inproj_body, n_a=n_a),
        grid=(m // tm, n_a + n_b),
        in_specs=[
            pl.BlockSpec((tm, d), lambda i, j: (i, 0)),
            pl.BlockSpec((1, d), lambda i, j: (0, 0)),
            pl.BlockSpec((None, d, tn), lambda i, j: (layer, 0, j)),
        ],
        out_specs=[
            pl.BlockSpec((tm, tn), lambda i, j: (i, jnp.minimum(j, n_a - 1))),
            pl.BlockSpec((tm, tn), lambda i, j: (i, jnp.maximum(j - n_a, 0))),
        ],
        out_shape=[
            jax.ShapeDtypeStruct((m, A_COLS), F32),
            jax.ShapeDtypeStruct((m, B_COLS), BF16),
        ],
        scratch_shapes=[pltpu.VMEM((tm, d), BF16)],
        compiler_params=pltpu.CompilerParams(
            dimension_semantics=("parallel", "arbitrary"), vmem_limit_bytes=_vmem(48)),
        name="in_proj",
    )(x2d, g, w_perm)


def _prep_body(sm_ref, dc_ref, bias_ref, mul_ref, kvg_ref, col_ref, row_ref, c_ref, ct_ref, fkp_ref,
               carry_ref):
    t = pl.program_id(1)

    @pl.when(t == 0)
    def _():
        carry_ref[...] = jnp.zeros_like(carry_ref)

    tt = sm_ref.shape[1]
    s = sm_ref[0]
    lane = lax.broadcasted_iota(I32, (tt, LANES), 1)
    is_dt = lane < SM_F
    is_f = (lane >= SM_F) & (lane < SM_W)
    is_a = (lane >= SM_A) & (lane < SM_A + N_HEADS)
    sp, ls = _softplus_parts(s + bias_ref[...])
    v = jnp.where(is_dt, sp, jnp.where(is_f, ls, jnp.where(is_a, sp, s) * mul_ref[...]))
    ri = lax.broadcasted_iota(I32, (tt, tt), 0)
    ci = lax.broadcasted_iota(I32, (tt, tt), 1)
    tril = jnp.where(ci <= ri, 1.0, 0.0).astype(F32)
    local = jnp.dot(tril, v, precision=lax.Precision.HIGHEST, preferred_element_type=F32)
    out = jnp.where(is_f, local + carry_ref[...], jnp.where(is_a, local, v))
    col_ref[0] = out
    row_ref[0, 0] = out.T
    carry_ref[...] = jnp.where(is_f[0:1], out[tt - 1:tt, :], 0.0)

    f0 = jnp.where(is_f, out, 0.0)
    hi = f0.astype(BF16).astype(F32)
    r1 = f0 - hi
    mid = r1.astype(BF16).astype(F32)
    lo = (r1 - mid).astype(BF16).astype(F32)
    fkp_ref[0] = (hi + pltpu.roll(mid, N_HEADS, axis=1) + pltpu.roll(lo, 2 * N_HEADS, axis=1)).astype(BF16)

    dc = dc_ref[0]
    ms = jnp.mean(dc * dc, axis=-1, keepdims=True)
    cn = dc * lax.rsqrt(ms + EPS) * kvg_ref[...]
    c_ref[0] = cn.astype(BF16)
    ct_ref[0, 0] = jnp.concatenate([cn.T, jnp.ones((CT_ROWS - DSA_LATENT, tt), F32)], axis=0).astype(BF16)


def _prep(ua, bias_vec, mul_vec, kv_g):
    b, lp, _ = ua.shape
    tt = SEQ_TILE
    nblk = lp // tt
    return pl.pallas_call(
        _prep_body,
        grid=(b, nblk),
        in_specs=[
            pl.BlockSpec((1, tt, LANES), lambda i, t: (i, t, 17)),
            pl.BlockSpec((1, tt, LANES), lambda i, t: (i, t, 16)),
            pl.BlockSpec((1, LANES), lambda i, t: (0, 0)),
            pl.BlockSpec((1, LANES), lambda i, t: (0, 0)),
            pl.BlockSpec((1, LANES), lambda i, t: (0, 0)),
        ],
        out_specs=[
            pl.BlockSpec((1, tt, LANES), lambda i, t: (i, t, 0)),
            pl.BlockSpec((1, 1, LANES, tt), lambda i, t: (i, t, 0, 0)),
            pl.BlockSpec((1, tt, LANES), lambda i, t: (i, t, 0)),
            pl.BlockSpec((1, 1, CT_ROWS, tt), lambda i, t: (i, t, 0, 0)),
            pl.BlockSpec((1, tt, LANES), lambda i, t: (i, t, 0)),
        ],
        out_shape=[
            jax.ShapeDtypeStruct((b, lp, LANES), F32),
            jax.ShapeDtypeStruct((b, nblk, LANES, tt), F32),
            jax.ShapeDtypeStruct((b, lp, DSA_LATENT), BF16),
            jax.ShapeDtypeStruct((b, nblk, CT_ROWS, tt), BF16),
            jax.ShapeDtypeStruct((b, lp, LANES), BF16),
        ],
        scratch_shapes=[pltpu.VMEM((1, LANES), F32)],
        compiler_params=pltpu.CompilerParams(dimension_semantics=("parallel", "arbitrary")),
        name="prep",
    )(ua, ua, bias_vec, mul_vec, kv_g)


def _expand_heads(colv, base, lo_half):
    parts = []
    for p in range(N_HEADS // 2):
        a = colv[:, base + 2 * p:base + 2 * p + 1]
        b = colv[:, base + 2 * p + 1:base + 2 * p + 2]
        parts.append(jnp.where(lo_half, a, b))
    return jnp.concatenate(parts, axis=1)


def _ssd_body(xbc_ref, z_ref, col_ref, row_ref, cw_ref, cb_ref, dsk_ref, ng_ref, y_ref,
              xpad_ref, st_ref):
    c = pl.program_id(1)
    ll = xbc_ref.shape[1]
    gw = GROUP_WIDTH
    ns = SSD_STATE
    hpg = N_HEADS // SSD_GROUPS
    gcols = hpg * HEAD_DIM

    @pl.when(c == 0)
    def _():
        xpad_ref[0:8, :] = jnp.zeros((8, SSD_XBC), F32)
        st_ref[...] = jnp.zeros_like(st_ref)

    x = xbc_ref[0]
    xpad_ref[8:8 + ll, :] = x
    conv = (cb_ref[...]
            + cw_ref[0:1, :] * xpad_ref[5:5 + ll, :]
            + cw_ref[1:2, :] * xpad_ref[6:6 + ll, :]
            + cw_ref[2:3, :] * xpad_ref[7:7 + ll, :]
            + cw_ref[3:4, :] * x)
    xpad_ref[0:8, :] = x[ll - 8:ll, :]
    act = _silu(conv)
    xs = act[:, 0:gw]
    bm = act[:, gw:gw + SSD_GROUPS * ns]
    cm = act[:, gw + SSD_GROUPS * ns:]

    colv = col_ref[0]
    rowv = row_ref[0, 0]
    lo_half = lax.broadcasted_iota(I32, (ll, LANES), 1) < HEAD_DIM
    dt_full = _expand_heads(colv, SM_DT, lo_half)
    acs_full = _expand_heads(colv, SM_A, lo_half)
    acs_last = acs_full[ll - 1:ll, :]
    dte_full = jnp.exp(acs_last - acs_full)
    dfs_full = jnp.exp(acs_full)
    xdt = xs * dt_full
    xdt_b = xdt.astype(BF16)
    xdte_b = (xdt * dte_full).astype(BF16)
    cm_b = cm.astype(BF16)
    ri = lax.broadcasted_iota(I32, (ll, ll), 0)
    ci = lax.broadcasted_iota(I32, (ll, ll), 1)
    tril = ci <= ri

    ys = []
    for g in range(SSD_GROUPS):
        bg = bm[:, ns * g:ns * (g + 1)]
        bg_b = bg.astype(BF16)
        bgt_b = bg.T.astype(BF16)
        cg_b = cm_b[:, ns * g:ns * (g + 1)]
        cb = lax.dot_general(cg_b, bg_b, (((1,), (1,)), ((), ())), preferred_element_type=F32)
        sg = st_ref[g]
        yoff = (jnp.dot(cg_b, sg.astype(BF16), preferred_element_type=F32)
                * dfs_full[:, gcols * g:gcols * (g + 1)])
        parts = []
        for pr in range(hpg // 2):
            xpair = xdt_b[:, gcols * g + LANES * pr:gcols * g + LANES * (pr + 1)]
            res = []
            for hh in range(2):
                h = hpg * g + 2 * pr + hh
                seg = colv[:, SM_A + h:SM_A + h + 1] - rowv[SM_A + h:SM_A + h + 1, :]
                lm = jnp.exp(jnp.where(tril, seg, -jnp.inf))
                res.append(jnp.dot((cb * lm).astype(BF16), xpair, preferred_element_type=F32))
            parts.append(jnp.where(lo_half, res[0], res[1]))
        ydiag = jnp.concatenate(parts, axis=1)
        decay = jnp.exp(acs_last[:, gcols * g:gcols * (g + 1)])
        st_ref[g] = decay * sg + jnp.dot(bgt_b, xdte_b[:, gcols * g:gcols * (g + 1)],
                                         preferred_element_type=F32)
        ys.append(ydiag + yoff)

    y = jnp.concatenate(ys, axis=1) + dsk_ref[...] * xs
    gz = y * _silu(z_ref[0])
    outs = []
    for g in range(SSD_GROUPS):
        gg = gz[:, gcols * g:gcols * (g + 1)]
        outs.append(gg * lax.rsqrt(jnp.mean(gg * gg, axis=-1, keepdims=True) + EPS))
    y_ref[0] = (jnp.concatenate(outs, axis=1) * ng_ref[...]).astype(BF16)


def _ssd(ua, col, row, conv_w, conv_b, dskip_full, norm_g):
    b, lp, _ = ua.shape
    ll = SEQ_TILE
    nblk = lp // ll
    hpg = N_HEADS // SSD_GROUPS
    return pl.pallas_call(
        _ssd_body,
        grid=(b, nblk),
        in_specs=[
            pl.BlockSpec((1, ll, SSD_XBC), lambda i, c: (i, c, 0)),
            pl.BlockSpec((1, ll, GROUP_WIDTH), lambda i, c: (i, c, 2)),
            pl.BlockSpec((1, ll, LANES), lambda i, c: (i, c, 0)),
            pl.BlockSpec((1, 1, LANES, ll), lambda i, c: (i, c, 0, 0)),
            pl.BlockSpec((SSD_CONV, SSD_XBC), lambda i, c: (0, 0)),
            pl.BlockSpec((1, SSD_XBC), lambda i, c: (0, 0)),
            pl.BlockSpec((1, GROUP_WIDTH), lambda i, c: (0, 0)),
            pl.BlockSpec((1, GROUP_WIDTH), lambda i, c: (0, 0)),
        ],
        out_specs=pl.BlockSpec((1, ll, GROUP_WIDTH), lambda i, c: (i, c, 0)),
        out_shape=jax.ShapeDtypeStruct((b, lp, GROUP_WIDTH), BF16),
        scratch_shapes=[
            pltpu.VMEM((8 + ll, SSD_XBC), F32),
            pltpu.VMEM((SSD_GROUPS, SSD_STATE, hpg * HEAD_DIM), F32),
        ],
        compiler_params=pltpu.CompilerParams(dimension_semantics=("parallel", "arbitrary")),
        name="ssd",
    )(ua, ua, col, row, conv_w, conv_b, dskip_full, norm_g)


def _pool_body(u_ref, w_ref, sc_ref, y_ref, buf_ref):
    t = pl.program_id(1)
    tt = u_ref.shape[1]
    hist = max(POOL_WINDOWS)
    gd = GROUP_WIDTH // len(POOL_WINDOWS)

    @pl.when(t == 0)
    def _():
        buf_ref[0:hist, :] = jnp.zeros((hist, GROUP_WIDTH), F32)

    u = u_ref[0]
    buf_ref[hist:hist + tt, :] = u
    count = (t * tt + 1 + lax.broadcasted_iota(I32, (tt, 1), 0)).astype(F32)
    outs = []
    for gi, win in enumerate(POOL_WINDOWS):
        ug = u[:, gd * gi:gd * (gi + 1)]
        acc = ug
        for k in range(1, win):
            acc = acc + buf_ref[hist - k:hist - k + tt, gd * gi:gd * (gi + 1)]
        pooled = acc / jnp.minimum(count, float(win)) - ug
        outs.append(jnp.dot(pooled.astype(BF16), w_ref[gi], preferred_element_type=F32))
    y_ref[0] = (jnp.concatenate(outs, axis=1) * sc_ref[...]).astype(BF16)
    buf_ref[0:hist, :] = u[tt - hist:tt, :]


def _pool(ua, w, scale):
    b, lp, _ = ua.shape
    tt = SEQ_TILE
    ng = len(POOL_WINDOWS)
    gd = GROUP_WIDTH // ng
    return pl.pallas_call(
        _pool_body,
        grid=(b, lp // tt),
        in_specs=[
            pl.BlockSpec((1, tt, GROUP_WIDTH), lambda i, t: (i, t, 3)),
            pl.BlockSpec((ng, gd, gd), lambda i, t: (0, 0, 0)),
            pl.BlockSpec((1, GROUP_WIDTH), lambda i, t: (0, 0)),
        ],
        out_specs=pl.BlockSpec((1, tt, GROUP_WIDTH), lambda i, t: (i, t, 0)),
        out_shape=jax.ShapeDtypeStruct((b, lp, GROUP_WIDTH), BF16),
        scratch_shapes=[pltpu.VMEM((max(POOL_WINDOWS) + tt, GROUP_WIDTH), F32)],
        compiler_params=pltpu.CompilerParams(dimension_semantics=("parallel", "arbitrary")),
        name="pool",
    )(ua, w, scale)


def _fox_body(q_ref, k_ref, v_ref, fkp_ref, o_ref, qa_ref, vl_ref, acc_ref, m_ref):
    qi = pl.program_id(1)
    tq = q_ref.shape[1]
    tk = tq
    nt = (((1,), (1,)), ((), ()))
    npair = N_HEADS // 2
    lane = lax.broadcasted_iota(I32, (tq, LANES), 1)
    lo_half = lane < HEAD_DIM
    zero_b = jnp.zeros((tq, LANES), BF16)

    @pl.when(qi == 0)
    def _():
        nkt = vl_ref.shape[0]
        rsel = lax.broadcasted_iota(I32, (ACC_ROWS - 2 * HEAD_DIM, 2 * tk), 0)
        csel = lax.broadcasted_iota(I32, (ACC_ROWS - 2 * HEAD_DIM, 2 * tk), 1)
        ones_rows = jnp.where((rsel == 0) & (csel < tk) | (rsel == 1) & (csel >= tk), 1.0, 0.0).astype(BF16)
        zpad = jnp.zeros((HEAD_DIM, tk), BF16)

        def fill(kt, carry):
            ks = pl.multiple_of(kt * tk, tk)
            for pr in range(npair):
                vt = v_ref[0, pl.ds(ks, tk), LANES * pr:LANES * (pr + 1)].astype(F32).T.astype(BF16)
                top = jnp.concatenate([vt[:HEAD_DIM], zpad], axis=1)
                bot = jnp.concatenate([zpad, vt[HEAD_DIM:]], axis=1)
                vl_ref[kt, pr] = jnp.concatenate([top, bot, ones_rows], axis=0)
            return carry

        lax.fori_loop(0, nkt, fill, 0)

    for pr in range(npair):
        q2 = q_ref[0, :, LANES * pr:LANES * (pr + 1)] * jnp.asarray(HEAD_DIM ** -0.5, BF16)
        for hh in range(2):
            h = 2 * pr + hh
            qm = jnp.where(lo_half, q2, zero_b) if hh == 0 else jnp.where(lo_half, zero_b, q2)
            pick = (lane == SM_F + h) | (lane == SM_F + N_HEADS + h) | (lane == SM_F + 2 * N_HEADS + h)
            qa_ref[h] = jnp.concatenate([qm, jnp.where(pick, -1.0, 0.0).astype(BF16)], axis=1)
    acc_ref[...] = jnp.zeros_like(acc_ref)
    m_ref[...] = jnp.full(m_ref.shape, NEG_BIG, F32)
    causal_t = (lax.broadcasted_iota(I32, (tk, tq), 0) <= lax.broadcasted_iota(I32, (tk, tq), 1))
    tail_row = lax.broadcasted_iota(I32, (ACC_ROWS - 2 * HEAD_DIM, tq), 0)

    def step(kt, masked):
        ks = pl.multiple_of(kt * tk, tk)
        fkp = fkp_ref[0, pl.ds(ks, tk), :]
        for pr in range(npair):
            ka = jnp.concatenate([k_ref[0, pl.ds(ks, tk), LANES * pr:LANES * (pr + 1)], fkp], axis=1)
            ps, als = [], []
            for hh in range(2):
                h = 2 * pr + hh
                st = lax.dot_general(ka, qa_ref[h], nt, preferred_element_type=F32)
                if masked:
                    st = jnp.where(causal_t, st, -jnp.inf)
                m_old = m_ref[h]
                mn = jnp.maximum(m_old, jnp.max(st, axis=0, keepdims=True))
                ps.append(jnp.exp(st - mn).astype(BF16))
                als.append(jnp.exp(m_old - mn))
                m_ref[h] = mn
            upd = jnp.dot(vl_ref[kt, pr], jnp.concatenate(ps, axis=0), preferred_element_type=F32)
            scale = jnp.concatenate([jnp.broadcast_to(als[0], (HEAD_DIM, tq)),
                                     jnp.broadcast_to(als[1], (HEAD_DIM, tq)),
                                     jnp.where(tail_row == 0, als[0], als[1])], axis=0)
            acc_ref[pr] = scale * acc_ref[pr] + upd

    def body(kt, carry):
        step(kt, False)
        return carry

    lax.fori_loop(0, qi, body, 0)
    step(qi, True)
    for pr in range(npair):
        a = acc_ref[pr]
        o = jnp.concatenate([a[:HEAD_DIM] / a[2 * HEAD_DIM:2 * HEAD_DIM + 1],
                             a[HEAD_DIM:2 * HEAD_DIM] / a[2 * HEAD_DIM + 1:2 * HEAD_DIM + 2]], axis=0)
        o_ref[0, :, LANES * pr:LANES * (pr + 1)] = o.T.astype(BF16)


def _fox(ub, fkp, *, sub):
    b, lp, _ = ub.shape
    tq = sub * SEQ_TILE
    return pl.pallas_call(
        _fox_body,
        grid=(b, lp // tq),
        in_specs=[
            pl.BlockSpec((1, tq, GROUP_WIDTH), lambda i, q: (i, q, 0)),
            pl.BlockSpec((1, lp, GROUP_WIDTH), lambda i, q: (i, 0, 1)),
            pl.BlockSpec((1, lp, GROUP_WIDTH), lambda i, q: (i, 0, 2)),
            pl.BlockSpec((1, lp, LANES), lambda i, q: (i, 0, 0)),
        ],
        out_specs=pl.BlockSpec((1, tq, GROUP_WIDTH), lambda i, q: (i, q, 0)),
        out_shape=jax.ShapeDtypeStruct((b, lp, GROUP_WIDTH), BF16),
        scratch_shapes=[
            pltpu.VMEM((N_HEADS, tq, 2 * LANES), BF16),
            pltpu.VMEM((lp // tq, N_HEADS // 2, ACC_ROWS, 2 * tq), BF16),
            pltpu.VMEM((N_HEADS // 2, ACC_ROWS, tq), F32),
            pltpu.VMEM((N_HEADS, 1, tq), F32),
        ],
        compiler_params=pltpu.CompilerParams(
            dimension_semantics=("parallel", "arbitrary"), vmem_limit_bytes=_vmem(48)),
        name="fox",
    )(ub, ub, ub, fkp)


def _dsa_body(dq_ref, dqi_ref, ki_ref, c_ref, ct_ref, wr_ref, wuk_ref, wuvt_ref, o_ref,
              key_ref, bias_ref, qm_ref, qlat_ref, acc_ref, m_ref, *, topk, sub):
    qi = pl.program_id(1)
    tq = dq_ref.shape[1]
    tk = tq
    ngrp = qi + 1
    has_tail = qi + 1 < key_ref.shape[0]
    tblk = jnp.minimum(sub * (qi + 1), ct_ref.shape[1] - 1)
    t0 = pl.multiple_of(tblk * SEQ_TILE, SEQ_TILE)
    nt = (((1,), (1,)), ((), ()))
    lo_half = lax.broadcasted_iota(I32, (tq, LANES), 1) < IDX_DIM
    zero_b = jnp.zeros((tq, LANES), BF16)
    shift = CHUNK - N_META
    lg2 = CHUNK.bit_length() - 1
    kf = float(topk)

    def fold8(w):
        return jnp.sum(w.reshape(w.shape[0] // 8, 8, tq), axis=0)

    for h in range(IDX_HEADS):
        q2 = dqi_ref[0, :, LANES * (h // 2):LANES * (h // 2 + 1)]
        qm_ref[tq * h:tq * (h + 1), :] = (jnp.where(lo_half, q2, zero_b) if h % 2 == 0
                                           else jnp.where(lo_half, zero_b, q2))
    wrows = jnp.concatenate([wr_ref[0, j] for j in range(sub)], axis=1)

    def keys_of(kt, ks, rows):
        lg_all = lax.dot_general(kt, qm_ref[...], nt, preferred_element_type=F32)
        sc = None
        for h in range(IDX_HEADS):
            term = wrows[h:h + 1, :] * jnp.maximum(lg_all[:, tq * h:tq * (h + 1)], 0.0)
            sc = term if sc is None else sc + term
        bits = pltpu.bitcast(sc, I32)
        bits = jnp.where(bits == INT_MIN, 0, bits)
        skey = bits ^ ((bits >> 31) & 0x7FFFFFFF)
        kcid = (ks + lax.broadcasted_iota(I32, (rows, tq), 0) + shift) >> lg2
        qcid = (qi * tq + lax.broadcasted_iota(I32, (rows, tq), 1) + shift) >> lg2
        return skey, kcid <= qcid

    def score_step(g, carry):
        ks = pl.multiple_of(g * tk, tk)
        skey, adm = keys_of(ki_ref[0, pl.ds(ks, tk), :], ks, tk)
        key_ref[g] = jnp.where(adm, skey, INT_MIN)
        return carry

    lax.fori_loop(0, ngrp, score_step, 0)
    skey, adm = keys_of(ki_ref[0, pl.ds(t0, TAIL), :], t0, TAIL)
    tkey = jnp.where(adm & has_tail, skey, INT_MIN)

    def bit_step(i, u):
        uc = u | lax.shift_left(jnp.int32(1), 31 - i)
        sc = uc ^ INT_MIN

        def cnt_step(g, acc):
            return acc + fold8(jnp.where(key_ref[g] >= sc, 1.0, 0.0))

        acc = lax.fori_loop(0, ngrp, cnt_step, fold8(jnp.where(tkey >= sc, 1.0, 0.0)))
        cnt = jnp.sum(acc, axis=0, keepdims=True)
        return jnp.where(cnt >= kf, uc, u)

    thr = lax.fori_loop(0, 32, bit_step, jnp.zeros((1, tq), I32)) ^ INT_MIN

    def gt_step(g, acc):
        return acc + fold8(jnp.where(key_ref[g] > thr, 1.0, 0.0))

    ngt = jnp.sum(lax.fori_loop(0, ngrp, gt_step, fold8(jnp.where(tkey > thr, 1.0, 0.0))),
                  axis=0, keepdims=True)
    room = kf - ngt
    incl = jnp.where(lax.broadcasted_iota(I32, (tk, tk), 1) <= lax.broadcasted_iota(I32, (tk, tk), 0),
                     1.0, 0.0).astype(BF16)
    incl_tail = jnp.where(lax.broadcasted_iota(I32, (TAIL, TAIL), 1) <= lax.broadcasted_iota(I32, (TAIL, TAIL), 0),
                          1.0, 0.0).astype(BF16)

    def bias_of(key, seen):
        rows = key.shape[0]
        eqf = jnp.where(key == thr, 1.0, 0.0)
        rank = jnp.dot(incl if rows == tk else incl_tail, eqf.astype(BF16), preferred_element_type=F32) + seen
        tie = jnp.where(rank <= room, eqf, 0.0)
        sel = jnp.where(key > thr, 1.0, tie)
        sel = jnp.where(key == INT_MIN, 0.0, sel)
        return jnp.where(sel > 0.5, 0.0, -jnp.inf), seen + jnp.sum(fold8(eqf), axis=0, keepdims=True)

    def mask_step(g, seen):
        bias_ref[g], seen = bias_of(key_ref[g], seen)
        return seen

    seen = lax.fori_loop(0, ngrp, mask_step, jnp.zeros((1, tq), F32))
    tbias, _ = bias_of(tkey, seen)
    tbias_blk = jnp.concatenate([tbias, jnp.full((SEQ_TILE - TAIL, tq), -jnp.inf, F32)], axis=0)

    for h in range(N_HEADS):
        dq2 = dq_ref[0, :, LANES * (h // 2):LANES * (h // 2 + 1)]
        ql = jnp.dot(dq2, wuk_ref[h], preferred_element_type=F32) * (HEAD_DIM ** -0.5)
        qlat_ref[tq * h:tq * (h + 1), :] = ql.astype(BF16)
    acc_ref[...] = jnp.zeros_like(acc_ref)
    m_ref[...] = jnp.full(m_ref.shape, NEG_BIG, F32)

    def attend(ck, cx, bias):
        st_all = lax.dot_general(ck, qlat_ref[...], nt, preferred_element_type=F32)
        ps, als = [], []
        for h in range(N_HEADS):
            st = st_all[:, tq * h:tq * (h + 1)] + bias
            m_old = m_ref[h]
            mn = jnp.maximum(m_old, jnp.max(st, axis=0, keepdims=True))
            ps.append(jnp.exp(st - mn).astype(BF16))
            als.append(jnp.exp(m_old - mn))
            m_ref[h] = mn
        upd = jnp.dot(cx, jnp.concatenate(ps, axis=1), preferred_element_type=F32)
        acc_ref[...] = jnp.concatenate(als, axis=1) * acc_ref[...] + upd

    def att_step(g, carry):
        ks = pl.multiple_of(g * tk, tk)
        cx = jnp.concatenate([ct_ref[0, g * sub + j] for j in range(sub)], axis=1)
        attend(c_ref[0, pl.ds(ks, tk), :], cx, bias_ref[g])
        return carry

    lax.fori_loop(0, ngrp, att_step, 0)
    attend(c_ref[0, pl.ds(t0, SEQ_TILE), :], ct_ref[0, tblk], tbias_blk)

    for pr in range(N_HEADS // 2):
        outs = []
        for hh in range(2):
            a = acc_ref[:, tq * (2 * pr + hh):tq * (2 * pr + hh + 1)]
            olat = (a[:DSA_LATENT] / a[DSA_LATENT:DSA_LATENT + 1]).astype(BF16)
            outs.append(jnp.dot(wuvt_ref[2 * pr + hh], olat, preferred_element_type=F32))
        o_ref[0, :, LANES * pr:LANES * (pr + 1)] = jnp.concatenate(outs, axis=0).T.astype(BF16)


def _dsa(ub, c, ct, rowt, wuk_pad, wuv_t, *, topk, sub):
    b, lp, _ = ub.shape
    tq = sub * SEQ_TILE
    ngrp = lp // tq
    nblk = lp // SEQ_TILE
    return pl.pallas_call(
        functools.partial(_dsa_body, topk=topk, sub=sub),
        grid=(b, ngrp),
        in_specs=[
            pl.BlockSpec((1, tq, GROUP_WIDTH), lambda i, q: (i, q, 3)),
            pl.BlockSpec((1, tq, IDX_HEADS * IDX_DIM), lambda i, q: (i, q, 8)),
            pl.BlockSpec((1, lp, LANES), lambda i, q: (i, 0, 18)),
            pl.BlockSpec((1, lp, DSA_LATENT), lambda i, q: (i, 0, 0)),
            pl.BlockSpec((1, nblk, CT_ROWS, SEQ_TILE), lambda i, q: (i, 0, 0, 0)),
            pl.BlockSpec((1, sub, N_HEADS, SEQ_TILE), lambda i, q: (i, q, SM_W // N_HEADS, 0)),
            pl.BlockSpec((N_HEADS, LANES, DSA_LATENT), lambda i, q: (0, 0, 0)),
            pl.BlockSpec((N_HEADS, HEAD_DIM, DSA_LATENT), lambda i, q: (0, 0, 0)),
        ],
        out_specs=pl.BlockSpec((1, tq, GROUP_WIDTH), lambda i, q: (i, q, 0)),
        out_shape=jax.ShapeDtypeStruct((b, lp, GROUP_WIDTH), BF16),
        scratch_shapes=[
            pltpu.VMEM((ngrp, tq, tq), I32),
            pltpu.VMEM((ngrp, tq, tq), F32),
            pltpu.VMEM((IDX_HEADS * tq, LANES), BF16),
            pltpu.VMEM((N_HEADS * tq, DSA_LATENT), BF16),
            pltpu.VMEM((CT_ROWS, N_HEADS * tq), F32),
            pltpu.VMEM((N_HEADS, 1, tq), F32),
        ],
        compiler_params=pltpu.CompilerParams(
            dimension_semantics=("parallel", "arbitrary"), vmem_limit_bytes=_vmem(48)),
        name="dsa",
    )(ub, ub, ub, c, ct, rowt, wuk_pad, wuv_t)


def _outproj_body(ya_ref, yb_ref, yc_ref, yd_ref, x_ref, w_ref, g_ref, o_ref):
    acc = jnp.dot(ya_ref[...], w_ref[0], preferred_element_type=F32)
    acc = acc + jnp.dot(yb_ref[...], w_ref[1], preferred_element_type=F32)
    acc = acc + jnp.dot(yc_ref[...], w_ref[2], preferred_element_type=F32)
    acc = acc + jnp.dot(yd_ref[...], w_ref[3], preferred_element_type=F32)
    ms = jnp.mean(acc * acc, axis=-1, keepdims=True)
    o_ref[...] = x_ref[...] + acc * lax.rsqrt(ms + EPS) * g_ref[...]


def _out_proj(ys, x2d, w4, g, *, layer, tm):
    m, d = x2d.shape
    gw = GROUP_WIDTH
    yspec = pl.BlockSpec((tm, gw), lambda i: (i, 0))
    return pl.pallas_call(
        _outproj_body,
        grid=(m // tm,),
        in_specs=[yspec, yspec, yspec, yspec,
                  pl.BlockSpec((tm, d), lambda i: (i, 0)),
                  pl.BlockSpec((None, 4, gw, d), lambda i: (layer, 0, 0, 0)),
                  pl.BlockSpec((1, d), lambda i: (0, 0))],
        out_specs=pl.BlockSpec((tm, d), lambda i: (i, 0)),
        out_shape=jax.ShapeDtypeStruct((m, d), F32),
        compiler_params=pltpu.CompilerParams(
            dimension_semantics=("parallel",), vmem_limit_bytes=_vmem(48)),
        name="out_proj",
    )(*ys, x2d, w4, g)


def _ffn_body(x_ref, gpre_ref, wg_ref, wu_ref, cw_ref, cb_ref, wd_ref, gpost_ref, o_ref,
              xn_ref, acc_ref, gbuf_ref, carry_ref, *, tiles_per_seq):
    i = pl.program_id(0)
    f = pl.program_id(1)
    nf = pl.num_programs(1)
    tm = x_ref.shape[0]

    @pl.when(f == 0)
    def _():
        x = x_ref[...]
        ms = jnp.mean(x * x, axis=-1, keepdims=True)
        xn_ref[...] = (x * lax.rsqrt(ms + EPS) * gpre_ref[...]).astype(BF16)
        acc_ref[...] = jnp.zeros_like(acc_ref)

    @pl.when(i % tiles_per_seq == 0)
    def _():
        carry_ref[f] = jnp.zeros(carry_ref.shape[1:], F32)

    xn = xn_ref[...]
    g = jnp.dot(xn, wg_ref[...], preferred_element_type=F32)
    u = jnp.dot(xn, wu_ref[...], preferred_element_type=F32)
    gbuf_ref[0:8, :] = carry_ref[f]
    gbuf_ref[8:8 + tm, :] = g
    conv = (cb_ref[...]
            + cw_ref[0:1, :] * gbuf_ref[6:6 + tm, :]
            + cw_ref[1:2, :] * gbuf_ref[7:7 + tm, :]
            + cw_ref[2:3, :] * g)
    carry_ref[f] = g[tm - 8:tm, :]
    a = (_silu(conv) * u).astype(BF16)
    acc_ref[...] += jnp.dot(a, wd_ref[...], preferred_element_type=F32)

    @pl.when(f == nf - 1)
    def _():
        y = acc_ref[...]
        ms = jnp.mean(y * y, axis=-1, keepdims=True)
        o_ref[...] = x_ref[...] + y * lax.rsqrt(ms + EPS) * gpost_ref[...]


def _ffn(x2d, g_pre, w_gate, w_up, conv_w, conv_b, w_down, g_post, *, layer, tm, tf, tiles_per_seq):
    m, d = x2d.shape
    fdim = w_gate.shape[-1]
    nf = fdim // tf
    return pl.pallas_call(
        functools.partial(_ffn_body, tiles_per_seq=tiles_per_seq),
        grid=(m // tm, nf),
        in_specs=[
            pl.BlockSpec((tm, d), lambda i, f: (i, 0)),
            pl.BlockSpec((1, d), lambda i, f: (0, 0)),
            pl.BlockSpec((None, d, tf), lambda i, f: (layer, 0, f)),
            pl.BlockSpec((None, d, tf), lambda i, f: (layer, 0, f)),
            pl.BlockSpec((FFN_CONV, tf), lambda i, f: (0, f)),
            pl.BlockSpec((1, tf), lambda i, f: (0, f)),
            pl.BlockSpec((None, tf, d), lambda i, f: (layer, f, 0)),
            pl.BlockSpec((1, d), lambda i, f: (0, 0)),
        ],
        out_specs=pl.BlockSpec((tm, d), lambda i, f: (i, 0)),
        out_shape=jax.ShapeDtypeStruct((m, d), F32),
        scratch_shapes=[
            pltpu.VMEM((tm, d), BF16),
            pltpu.VMEM((tm, d), F32),
            pltpu.VMEM((8 + tm, tf), F32),
            pltpu.VMEM((nf, 8, tf), F32),
        ],
        compiler_params=pltpu.CompilerParams(
            dimension_semantics=("arbitrary", "arbitrary"), vmem_limit_bytes=_vmem(56)),
        name="ffn",
    )(x2d, g_pre, w_gate, w_up, conv_w, conv_b, w_down, g_post)


def _permute_w_in(w_in):
    gw = GROUP_WIDTH
    sizes = (gw, SSD_XBC, N_HEADS, gw, 3 * gw, N_HEADS, gw, DSA_LATENT, IDX_HEADS * IDX_DIM, IDX_DIM, IDX_HEADS)
    offs = [0]
    for s in sizes:
        offs.append(offs[-1] + s)
    z, xbc, dt, pool, qkv, fl, dq, dc, dqi, dki, dwi = (w_in[..., offs[k]:offs[k + 1]] for k in range(len(sizes)))
    zeros = lambda n: jnp.zeros(w_in.shape[:-1] + (n,), w_in.dtype)
    small = jnp.concatenate([dt, fl, dwi, zeros(SM_A - SM_W - IDX_HEADS), dt, zeros(LANES - SM_A - N_HEADS)], axis=-1)
    a = jnp.concatenate([xbc, z, pool, dc, small, zeros(A_COLS - 2 * gw - SSD_XBC - DSA_LATENT - LANES)], axis=-1)
    bcols = jnp.concatenate([qkv, dq, dqi, dki, dki], axis=-1)
    bcols = jnp.concatenate([bcols, zeros(B_COLS - bcols.shape[-1])], axis=-1)
    return jnp.concatenate([a, bcols], axis=-1)


def _lane_vec(pieces):
    v = jnp.zeros((LANES,), F32)
    for off, val in pieces:
        v = v.at[off:off + val.shape[0]].set(val.astype(F32))
    return v[None, :]


def _pad_head_weights(w_uk, w_uv):
    h, r, d = w_uk.shape
    uk = jnp.zeros((h, 2 * d, r), F32)
    for i in range(h):
        o = d * (i % 2)
        uk = uk.at[i, o:o + d, :].set(w_uk[i].T)
    return uk.astype(BF16), jnp.swapaxes(w_uv, 1, 2).astype(BF16)


def _tile_sizes(b, lp):
    m = b * lp
    tm_proj = next(t for t in (1024, 512, 256, 128) if m % t == 0)
    tm_out = next(t for t in (512, 256, 128) if m % t == 0)
    tm_ffn = next(t for t in (528, 384, 320, 256, 128) if lp % t == 0)
    att_sub = 3 if lp % (3 * SEQ_TILE) == 0 else 1
    return tm_proj, tm_out, tm_ffn, att_sub


def _layer(h, p, big, layer, *, topk):
    b, lp, d = h.shape
    m = b * lp
    tm_proj, tm_out, tm_ffn, att_sub = _tile_sizes(b, lp)
    row = lambda v: v.astype(F32)[None, :]

    ua, ub = _in_proj(h.reshape(m, d), row(p["norm_mix_pre"]), big["w_in"], layer=layer, tm=tm_proj, tn=640)
    ua = ua.reshape(b, lp, A_COLS)
    ub = ub.reshape(b, lp, B_COLS)

    bias_vec = _lane_vec([(SM_DT, p["ssd_dt_bias"]), (SM_F, p["fox_f_bias"]), (SM_A, p["ssd_dt_bias"])])
    wscale = jnp.full((IDX_HEADS,), (IDX_HEADS ** -0.5) * (IDX_DIM ** -0.5), F32)
    mul_vec = _lane_vec([(SM_W, wscale), (SM_A, -jnp.exp(p["ssd_a_log"].astype(F32)))])
    col, rowt, c, ct, fkp = _prep(ua, bias_vec, mul_vec, row(p["dsa_kv_norm"]))

    dskip_full = jnp.repeat(p["ssd_d"].astype(F32), HEAD_DIM)[None, :]
    y_a = _ssd(ua, col, rowt, p["ssd_conv_w"].astype(F32), row(p["ssd_conv_b"]), dskip_full, row(p["ssd_norm"]))
    y_b = _pool(ua, p["pool_w"].astype(BF16), row(p["pool_scale"]))
    y_c = _fox(ub, fkp, sub=att_sub)
    wuk_pad, wuv_t = _pad_head_weights(p["dsa_w_uk"], p["dsa_w_uv"])
    y_d = _dsa(ub, c, ct, rowt, wuk_pad, wuv_t, topk=topk, sub=att_sub)

    ys = [y.reshape(m, GROUP_WIDTH) for y in (y_a, y_b, y_c, y_d)]
    x1 = _out_proj(ys, h.reshape(m, d), big["w_out"], row(p["norm_mix_post"]), layer=layer, tm=tm_out)
    x2 = _ffn(x1, row(p["norm_ffn_pre"]), big["ffn_w_gate"], big["ffn_w_up"],
              p["ffn_conv_w"].astype(F32), row(p["ffn_conv_b"]), big["ffn_w_down"],
              row(p["norm_ffn_post"]), layer=layer, tm=tm_ffn, tf=512, tiles_per_seq=lp // tm_ffn)
    return x2.reshape(b, lp, d)


def _stack_big_weights(w_in, w_out, ffn_w_gate, ffn_w_up, ffn_w_down):
    depth, d, _ = w_out.shape
    return dict(w_in=_to_bf16(_permute_w_in(w_in)),
                w_out=_to_bf16(w_out).reshape(depth, 4, GROUP_WIDTH, d),
                ffn_w_gate=_to_bf16(ffn_w_gate), ffn_w_up=_to_bf16(ffn_w_up), ffn_w_down=_to_bf16(ffn_w_down))


def kernel(x, meta_tokens, norm_mix_pre, norm_mix_post, norm_ffn_pre, norm_ffn_post, w_in, ssd_conv_w, ssd_conv_b, ssd_dt_bias, ssd_a_log, ssd_d, ssd_norm, pool_w, pool_scale, fox_f_bias, dsa_kv_norm, dsa_w_uk, dsa_w_uv, w_out, ffn_w_gate, ffn_w_up, ffn_conv_w, ffn_conv_b, ffn_w_down):
    bsz, seq, d = x.shape
    n = N_META + seq
    lp = -(-n // SEQ_TILE) * SEQ_TILE
    topk = min(DSA_TOPK_MAX, seq // 4)
    meta = jnp.broadcast_to(meta_tokens.astype(x.dtype)[None], (bsz, N_META, d))
    h = jnp.concatenate([meta, x, jnp.zeros((bsz, lp - n, d), x.dtype)], axis=1)
    small = dict(norm_mix_pre=norm_mix_pre, norm_mix_post=norm_mix_post, norm_ffn_pre=norm_ffn_pre,
                 norm_ffn_post=norm_ffn_post, ssd_conv_w=ssd_conv_w, ssd_conv_b=ssd_conv_b,
                 ssd_dt_bias=ssd_dt_bias, ssd_a_log=ssd_a_log, ssd_d=ssd_d, ssd_norm=ssd_norm,
                 pool_w=pool_w, pool_scale=pool_scale, fox_f_bias=fox_f_bias, dsa_kv_norm=dsa_kv_norm,
                 dsa_w_uk=dsa_w_uk, dsa_w_uv=dsa_w_uv, ffn_conv_w=ffn_conv_w, ffn_conv_b=ffn_conv_b)
    big = _stack_big_weights(w_in, w_out, ffn_w_gate, ffn_w_up, ffn_w_down)
    for i in range(norm_mix_pre.shape[0]):
        h = _layer(h, {k: v[i] for k, v in small.items()}, big, i, topk=topk)
    return h[:, N_META:n]
```

```python
import functools

import jax
import jax.numpy as jnp
from jax import lax
from jax.experimental import pallas as pl
from jax.experimental.pallas import tpu as pltpu

F32 = jnp.float32
BF16 = jnp.bfloat16
I32 = jnp.int32
I16 = jnp.int16

EPS = 1e-6
N_META = 16
CHUNK = 64
HEAD_DIM = 64
GROUP_WIDTH = 512
N_HEADS = 8
SSD_GROUPS = 2
SSD_STATE = 128
SSD_CONV = 4
SSD_XBC = GROUP_WIDTH + 2 * SSD_GROUPS * SSD_STATE
POOL_WINDOWS = (2, 4, 8, 16)
DSA_LATENT = 128
IDX_HEADS = 4
IDX_DIM = 64
DSA_TOPK_MAX = 256
FFN_CONV = 3

LANES = 128
SEQ_TILE = 128
INT_MIN = -(2 ** 31)
NEG_BIG = -1e30

A_COLS = 2560
B_COLS = 2560
SM_DT = 0
SM_F = 8
SM_W = 16
SM_A = 24
CT_ROWS = DSA_LATENT + 16
ACC_ROWS = 2 * HEAD_DIM + 16
TAIL = N_META


def _vmem(mb):
    return int(mb * 1024 * 1024)


def _softplus_parts(x):
    t = jnp.log1p(jnp.exp(-jnp.abs(x)))
    return jnp.maximum(x, 0.0) + t, jnp.minimum(x, 0.0) - t


def _silu(x):
    return x / (1.0 + jnp.exp(-x))


def _cast_body(x_ref, o_ref):
    o_ref[...] = x_ref[...].astype(BF16)


def _to_bf16(w, *, tr=256):
    shape = w.shape
    w2 = w.reshape(-1, shape[-1])
    r, c = w2.shape
    out = pl.pallas_call(
        _cast_body,
        grid=(r // tr,),
        in_specs=[pl.BlockSpec((tr, c), lambda i: (i, 0))],
        out_specs=pl.BlockSpec((tr, c), lambda i: (i, 0)),
        out_shape=jax.ShapeDtypeStruct((r, c), BF16),
        compiler_params=pltpu.CompilerParams(dimension_semantics=("parallel",), vmem_limit_bytes=_vmem(40)),
        name="to_bf16",
    )(w2)
    return out.reshape(shape)


def _inproj_body(x_ref, g_ref, w_ref, oa_ref, ob_ref, xn_ref, *, n_a):
    j = pl.program_id(1)

    @pl.when(j == 0)
    def _():
        x = x_ref[...]
        ms = jnp.mean(x * x, axis=-1, keepdims=True)
        xn_ref[...] = (x * lax.rsqrt(ms + EPS) * g_ref[...]).astype(BF16)

    acc = jnp.dot(xn_ref[...], w_ref[...], preferred_element_type=F32)

    @pl.when(j < n_a)
    def _():
        oa_ref[...] = acc

    @pl.when(j >= n_a)
    def _():
        ob_ref[...] = acc.astype(BF16)


def _in_proj(x2d, g, w_perm, *, layer, tm, tn):
    m, d = x2d.shape
    n_a, n_b = A_COLS // tn, B_COLS // tn
    return pl.pallas_call(
        functools.partial(_inproj_body, n_a=n_a),
        grid=(m // tm, n_a + n_b),
        in_specs=[
            pl.BlockSpec((tm, d), lambda i, j: (i, 0)),
            pl.BlockSpec((1, d), lambda i, j: (0, 0)),
            pl.BlockSpec((None, d, tn), lambda i, j: (layer, 0, j)),
        ],
        out_specs=[
            pl.BlockSpec((tm, tn), lambda i, j: (i, jnp.minimum(j, n_a - 1))),
            pl.BlockSpec((tm, tn), lambda i, j: (i, jnp.maximum(j - n_a, 0))),
        ],
        out_shape=[
            jax.ShapeDtypeStruct((m, A_COLS), F32),
            jax.ShapeDtypeStruct((m, B_COLS), BF16),
        ],
        scratch_shapes=[pltpu.VMEM((tm, d), BF16)],
        compiler_params=pltpu.CompilerParams(
            dimension_semantics=("parallel", "arbitrary"), vmem_limit_bytes=_vmem(58)),
        name="in_proj",
    )(x2d, g, w_perm)


def _prep_body(sm_ref, dc_ref, bias_ref, mul_ref, kvg_ref, col_ref, row_ref, c_ref, ct_ref, fkp_ref,
               carry_ref):
    t = pl.program_id(1)

    @pl.when(t == 0)
    def _():
        carry_ref[...] = jnp.zeros_like(carry_ref)

    tt = sm_ref.shape[1]
    s = sm_ref[0]
    lane = lax.broadcasted_iota(I32, (tt, LANES), 1)
    is_dt = lane < SM_F
    is_f = (lane >= SM_F) & (lane < SM_W)
    is_a = (lane >= SM_A) & (lane < SM_A + N_HEADS)
    sp, ls = _softplus_parts(s + bias_ref[...])
    v = jnp.where(is_dt, sp, jnp.where(is_f, ls, jnp.where(is_a, sp, s) * mul_ref[...]))
    ri = lax.broadcasted_iota(I32, (tt, tt), 0)
    ci = lax.broadcasted_iota(I32, (tt, tt), 1)
    tril = jnp.where(ci <= ri, 1.0, 0.0).astype(F32)
    local = jnp.dot(tril, v, precision=lax.Precision.HIGHEST, preferred_element_type=F32)
    out = jnp.where(is_f, local + carry_ref[...], jnp.where(is_a, local, v))
    col_ref[0] = out
    row_ref[0, 0] = out.T
    carry_ref[...] = jnp.where(is_f[0:1], out[tt - 1:tt, :], 0.0)

    f0 = jnp.where(is_f, out, 0.0)
    hi = f0.astype(BF16).astype(F32)
    r1 = f0 - hi
    mid = r1.astype(BF16).astype(F32)
    lo = (r1 - mid).astype(BF16).astype(F32)
    fkp_ref[0] = (hi + pltpu.roll(mid, N_HEADS, axis=1) + pltpu.roll(lo, 2 * N_HEADS, axis=1)).astype(BF16)

    dc = dc_ref[0]
    ms = jnp.mean(dc * dc, axis=-1, keepdims=True)
    cn = dc * lax.rsqrt(ms + EPS) * kvg_ref[...]
    c_ref[0] = cn.astype(BF16)
    ct_ref[0, 0] = jnp.concatenate([cn.T, jnp.ones((CT_ROWS - DSA_LATENT, tt), F32)], axis=0).astype(BF16)


def _prep(ua, bias_vec, mul_vec, kv_g):
    b, lp, _ = ua.shape
    tt = SEQ_TILE
    nblk = lp // tt
    return pl.pallas_call(
        _prep_body,
        grid=(b, nblk),
        in_specs=[
            pl.BlockSpec((1, tt, LANES), lambda i, t: (i, t, 17)),
            pl.BlockSpec((1, tt, LANES), lambda i, t: (i, t, 16)),
            pl.BlockSpec((1, LANES), lambda i, t: (0, 0)),
            pl.BlockSpec((1, LANES), lambda i, t: (0, 0)),
            pl.BlockSpec((1, LANES), lambda i, t: (0, 0)),
        ],
        out_specs=[
            pl.BlockSpec((1, tt, LANES), lambda i, t: (i, t, 0)),
            pl.BlockSpec((1, 1, LANES, tt), lambda i, t: (i, t, 0, 0)),
            pl.BlockSpec((1, tt, LANES), lambda i, t: (i, t, 0)),
            pl.BlockSpec((1, 1, CT_ROWS, tt), lambda i, t: (i, t, 0, 0)),
            pl.BlockSpec((1, tt, LANES), lambda i, t: (i, t, 0)),
        ],
        out_shape=[
            jax.ShapeDtypeStruct((b, lp, LANES), F32),
            jax.ShapeDtypeStruct((b, nblk, LANES, tt), F32),
            jax.ShapeDtypeStruct((b, lp, DSA_LATENT), BF16),
            jax.ShapeDtypeStruct((b, nblk, CT_ROWS, tt), BF16),
            jax.ShapeDtypeStruct((b, lp, LANES), BF16),
        ],
        scratch_shapes=[pltpu.VMEM((1, LANES), F32)],
        compiler_params=pltpu.CompilerParams(dimension_semantics=("parallel", "arbitrary")),
        name="prep",
    )(ua, ua, bias_vec, mul_vec, kv_g)


def _expand_heads(colv, base, lo_half):
    parts = []
    for p in range(N_HEADS // 2):
        a = colv[:, base + 2 * p:base + 2 * p + 1]
        b = colv[:, base + 2 * p + 1:base + 2 * p + 2]
        parts.append(jnp.where(lo_half, a, b))
    return jnp.concatenate(parts, axis=1)


def _ssd_body(xbc_ref, z_ref, col_ref, row_ref, cw_ref, cb_ref, dsk_ref, ng_ref, y_ref,
              xpad_ref, st_ref):
    c = pl.program_id(1)
    ll = xbc_ref.shape[1]
    gw = GROUP_WIDTH
    ns = SSD_STATE
    hpg = N_HEADS // SSD_GROUPS
    gcols = hpg * HEAD_DIM

    @pl.when(c == 0)
    def _():
        xpad_ref[0:8, :] = jnp.zeros((8, SSD_XBC), F32)
        st_ref[...] = jnp.zeros_like(st_ref)

    x = xbc_ref[0]
    xpad_ref[8:8 + ll, :] = x
    conv = (cb_ref[...]
            + cw_ref[0:1, :] * xpad_ref[5:5 + ll, :]
            + cw_ref[1:2, :] * xpad_ref[6:6 + ll, :]
            + cw_ref[2:3, :] * xpad_ref[7:7 + ll, :]
            + cw_ref[3:4, :] * x)
    xpad_ref[0:8, :] = x[ll - 8:ll, :]
    act = _silu(conv)
    xs = act[:, 0:gw]
    bm = act[:, gw:gw + SSD_GROUPS * ns]
    cm = act[:, gw + SSD_GROUPS * ns:]

    colv = col_ref[0]
    rowv = row_ref[0, 0]
    lo_half = lax.broadcasted_iota(I32, (ll, LANES), 1) < HEAD_DIM
    dt_full = _expand_heads(colv, SM_DT, lo_half)
    acs_full = _expand_heads(colv, SM_A, lo_half)
    acs_last = acs_full[ll - 1:ll, :]
    dte_full = jnp.exp(acs_last - acs_full)
    dfs_full = jnp.exp(acs_full)
    xdt = xs * dt_full
    xdt_b = xdt.astype(BF16)
    xdte_b = (xdt * dte_full).astype(BF16)
    cm_b = cm.astype(BF16)
    ri = lax.broadcasted_iota(I32, (ll, ll), 0)
    ci = lax.broadcasted_iota(I32, (ll, ll), 1)
    tril = ci <= ri

    ys = []
    for g in range(SSD_GROUPS):
        bg = bm[:, ns * g:ns * (g + 1)]
        bg_b = bg.astype(BF16)
        bgt_b = bg.T.astype(BF16)
        cg_b = cm_b[:, ns * g:ns * (g + 1)]
        cb = lax.dot_general(cg_b, bg_b, (((1,), (1,)), ((), ())), preferred_element_type=F32)
        sg = st_ref[g]
        yoff = (jnp.dot(cg_b, sg.astype(BF16), preferred_element_type=F32)
                * dfs_full[:, gcols * g:gcols * (g + 1)])
        parts = []
        for pr in range(hpg // 2):
            xpair = xdt_b[:, gcols * g + LANES * pr:gcols * g + LANES * (pr + 1)]
            res = []
            for hh in range(2):
                h = hpg * g + 2 * pr + hh
                seg = colv[:, SM_A + h:SM_A + h + 1] - rowv[SM_A + h:SM_A + h + 1, :]
                lm = jnp.exp(jnp.where(tril, seg, -jnp.inf))
                res.append(jnp.dot((cb * lm).astype(BF16), xpair, preferred_element_type=F32))
            parts.append(jnp.where(lo_half, res[0], res[1]))
        ydiag = jnp.concatenate(parts, axis=1)
        decay = jnp.exp(acs_last[:, gcols * g:gcols * (g + 1)])
        st_ref[g] = decay * sg + jnp.dot(bgt_b, xdte_b[:, gcols * g:gcols * (g + 1)],
                                         preferred_element_type=F32)
        ys.append(ydiag + yoff)

    y = jnp.concatenate(ys, axis=1) + dsk_ref[...] * xs
    gz = y * _silu(z_ref[0])
    outs = []
    for g in range(SSD_GROUPS):
        gg = gz[:, gcols * g:gcols * (g + 1)]
        outs.append(gg * lax.rsqrt(jnp.mean(gg * gg, axis=-1, keepdims=True) + EPS))
    y_ref[0] = (jnp.concatenate(outs, axis=1) * ng_ref[...]).astype(BF16)


def _ssd(ua, col, row, conv_w, conv_b, dskip_full, norm_g):
    b, lp, _ = ua.shape
    ll = SEQ_TILE
    nblk = lp // ll
    hpg = N_HEADS // SSD_GROUPS
    return pl.pallas_call(
        _ssd_body,
        grid=(b, nblk),
        in_specs=[
            pl.BlockSpec((1, ll, SSD_XBC), lambda i, c: (i, c, 0)),
            pl.BlockSpec((1, ll, GROUP_WIDTH), lambda i, c: (i, c, 2)),
            pl.BlockSpec((1, ll, LANES), lambda i, c: (i, c, 0)),
            pl.BlockSpec((1, 1, LANES, ll), lambda i, c: (i, c, 0, 0)),
            pl.BlockSpec((SSD_CONV, SSD_XBC), lambda i, c: (0, 0)),
            pl.BlockSpec((1, SSD_XBC), lambda i, c: (0, 0)),
            pl.BlockSpec((1, GROUP_WIDTH), lambda i, c: (0, 0)),
            pl.BlockSpec((1, GROUP_WIDTH), lambda i, c: (0, 0)),
        ],
        out_specs=pl.BlockSpec((1, ll, GROUP_WIDTH), lambda i, c: (i, c, 0)),
        out_shape=jax.ShapeDtypeStruct((b, lp, GROUP_WIDTH), BF16),
        scratch_shapes=[
            pltpu.VMEM((8 + ll, SSD_XBC), F32),
            pltpu.VMEM((SSD_GROUPS, SSD_STATE, hpg * HEAD_DIM), F32),
        ],
        compiler_params=pltpu.CompilerParams(dimension_semantics=("parallel", "arbitrary")),
        name="ssd",
    )(ua, ua, col, row, conv_w, conv_b, dskip_full, norm_g)


def _pool_body(u_ref, w_ref, sc_ref, y_ref, buf_ref):
    t = pl.program_id(1)
    tt = u_ref.shape[1]
    hist = max(POOL_WINDOWS)
    gd = GROUP_WIDTH // len(POOL_WINDOWS)

    @pl.when(t == 0)
    def _():
        buf_ref[0:hist, :] = jnp.zeros((hist, GROUP_WIDTH), F32)

    u = u_ref[0]
    buf_ref[hist:hist + tt, :] = u
    count = (t * tt + 1 + lax.broadcasted_iota(I32, (tt, 1), 0)).astype(F32)
    outs = []
    for gi, win in enumerate(POOL_WINDOWS):
        ug = u[:, gd * gi:gd * (gi + 1)]
        acc = ug
        for k in range(1, win):
            acc = acc + buf_ref[hist - k:hist - k + tt, gd * gi:gd * (gi + 1)]
        pooled = acc / jnp.minimum(count, float(win)) - ug
        outs.append(jnp.dot(pooled.astype(BF16), w_ref[gi], preferred_element_type=F32))
    y_ref[0] = (jnp.concatenate(outs, axis=1) * sc_ref[...]).astype(BF16)
    buf_ref[0:hist, :] = u[tt - hist:tt, :]


def _pool(ua, w, scale, *, sub):
    b, lp, _ = ua.shape
    tt = sub * SEQ_TILE
    ng = len(POOL_WINDOWS)
    gd = GROUP_WIDTH // ng
    return pl.pallas_call(
        _pool_body,
        grid=(b, lp // tt),
        in_specs=[
            pl.BlockSpec((1, tt, GROUP_WIDTH), lambda i, t: (i, t, 3)),
            pl.BlockSpec((ng, gd, gd), lambda i, t: (0, 0, 0)),
            pl.BlockSpec((1, GROUP_WIDTH), lambda i, t: (0, 0)),
        ],
        out_specs=pl.BlockSpec((1, tt, GROUP_WIDTH), lambda i, t: (i, t, 0)),
        out_shape=jax.ShapeDtypeStruct((b, lp, GROUP_WIDTH), BF16),
        scratch_shapes=[pltpu.VMEM((max(POOL_WINDOWS) + tt, GROUP_WIDTH), F32)],
        compiler_params=pltpu.CompilerParams(dimension_semantics=("parallel", "arbitrary")),
        name="pool",
    )(ua, w, scale)


def _fox_body(q_ref, k_ref, v_ref, fkp_ref, o_ref, qa_ref, vl_ref, acc_ref, m_ref):
    qi = pl.program_id(1)
    tq = q_ref.shape[1]
    tk = tq
    nt = (((1,), (1,)), ((), ()))
    npair = N_HEADS // 2
    lane = lax.broadcasted_iota(I32, (tq, LANES), 1)
    lo_half = lane < HEAD_DIM
    zero_b = jnp.zeros((tq, LANES), BF16)

    @pl.when(qi == 0)
    def _():
        nkt = vl_ref.shape[0]
        rsel = lax.broadcasted_iota(I32, (ACC_ROWS - 2 * HEAD_DIM, 2 * tk), 0)
        csel = lax.broadcasted_iota(I32, (ACC_ROWS - 2 * HEAD_DIM, 2 * tk), 1)
        ones_rows = jnp.where((rsel == 0) & (csel < tk) | (rsel == 1) & (csel >= tk), 1.0, 0.0).astype(BF16)
        zpad = jnp.zeros((HEAD_DIM, tk), BF16)

        def fill(kt, carry):
            ks = pl.multiple_of(kt * tk, tk)
            for pr in range(npair):
                vt = v_ref[0, pl.ds(ks, tk), LANES * pr:LANES * (pr + 1)].astype(F32).T.astype(BF16)
                top = jnp.concatenate([vt[:HEAD_DIM], zpad], axis=1)
                bot = jnp.concatenate([zpad, vt[HEAD_DIM:]], axis=1)
                vl_ref[kt, pr] = jnp.concatenate([top, bot, ones_rows], axis=0)
            return carry

        lax.fori_loop(0, nkt, fill, 0)

    for pr in range(npair):
        q2 = q_ref[0, :, LANES * pr:LANES * (pr + 1)] * jnp.asarray(HEAD_DIM ** -0.5, BF16)
        for hh in range(2):
            h = 2 * pr + hh
            qm = jnp.where(lo_half, q2, zero_b) if hh == 0 else jnp.where(lo_half, zero_b, q2)
            pick = (lane == SM_F + h) | (lane == SM_F + N_HEADS + h) | (lane == SM_F + 2 * N_HEADS + h)
            qa_ref[h] = jnp.concatenate([qm, jnp.where(pick, -1.0, 0.0).astype(BF16)], axis=1)
    acc_ref[...] = jnp.zeros_like(acc_ref)
    m_ref[...] = jnp.full(m_ref.shape, NEG_BIG, F32)
    causal_t = (lax.broadcasted_iota(I32, (tk, tq), 0) <= lax.broadcasted_iota(I32, (tk, tq), 1))
    tail_row = lax.broadcasted_iota(I32, (ACC_ROWS - 2 * HEAD_DIM, tq), 0)

    def step(kt, masked):
        ks = pl.multiple_of(kt * tk, tk)
        fkp = fkp_ref[0, pl.ds(ks, tk), :]
        for pr in range(npair):
            ka = jnp.concatenate([k_ref[0, pl.ds(ks, tk), LANES * pr:LANES * (pr + 1)], fkp], axis=1)
            ps, als = [], []
            for hh in range(2):
                h = 2 * pr + hh
                st = lax.dot_general(ka, qa_ref[h], nt, preferred_element_type=F32)
                if masked:
                    st = jnp.where(causal_t, st, -jnp.inf)
                m_old = m_ref[h]
                mn = jnp.maximum(m_old, jnp.max(st, axis=0, keepdims=True))
                ps.append(jnp.exp(st - mn).astype(BF16))
                als.append(jnp.exp(m_old - mn))
                m_ref[h] = mn
            upd = jnp.dot(vl_ref[kt, pr], jnp.concatenate(ps, axis=0), preferred_element_type=F32)
            scale = jnp.concatenate([jnp.broadcast_to(als[0], (HEAD_DIM, tq)),
                                     jnp.broadcast_to(als[1], (HEAD_DIM, tq)),
                                     jnp.where(tail_row == 0, als[0], als[1])], axis=0)
            acc_ref[pr] = scale * acc_ref[pr] + upd

    def body(kt, carry):
        step(kt, False)
        return carry

    lax.fori_loop(0, qi, body, 0)
    step(qi, True)
    for pr in range(npair):
        a = acc_ref[pr]
        o = jnp.concatenate([a[:HEAD_DIM] / a[2 * HEAD_DIM:2 * HEAD_DIM + 1],
                             a[HEAD_DIM:2 * HEAD_DIM] / a[2 * HEAD_DIM + 1:2 * HEAD_DIM + 2]], axis=0)
        o_ref[0, :, LANES * pr:LANES * (pr + 1)] = o.T.astype(BF16)


def _fox(ub, fkp, *, sub):
    b, lp, _ = ub.shape
    tq = sub * SEQ_TILE
    return pl.pallas_call(
        _fox_body,
        grid=(b, lp // tq),
        in_specs=[
            pl.BlockSpec((1, tq, GROUP_WIDTH), lambda i, q: (i, q, 0)),
            pl.BlockSpec((1, lp, GROUP_WIDTH), lambda i, q: (i, 0, 1)),
            pl.BlockSpec((1, lp, GROUP_WIDTH), lambda i, q: (i, 0, 2)),
            pl.BlockSpec((1, lp, LANES), lambda i, q: (i, 0, 0)),
        ],
        out_specs=pl.BlockSpec((1, tq, GROUP_WIDTH), lambda i, q: (i, q, 0)),
        out_shape=jax.ShapeDtypeStruct((b, lp, GROUP_WIDTH), BF16),
        scratch_shapes=[
            pltpu.VMEM((N_HEADS, tq, 2 * LANES), BF16),
            pltpu.VMEM((lp // tq, N_HEADS // 2, ACC_ROWS, 2 * tq), BF16),
            pltpu.VMEM((N_HEADS // 2, ACC_ROWS, tq), F32),
            pltpu.VMEM((N_HEADS, 1, tq), F32),
        ],
        compiler_params=pltpu.CompilerParams(
            dimension_semantics=("parallel", "arbitrary"), vmem_limit_bytes=_vmem(48)),
        name="fox",
    )(ub, ub, ub, fkp)


def _dsa_body(dq_ref, dqi_ref, ki_ref, c_ref, ct_ref, wr_ref, wuk_ref, wuvt_ref, o_ref,
              key_ref, bias_ref, hi_ref, lo_ref, qm_ref, qlat_ref, acc_ref, m_ref, *, topk, sub):
    qi = pl.program_id(1)
    tq = dq_ref.shape[1]
    tk = tq
    ngrp = qi + 1
    has_tail = qi + 1 < key_ref.shape[0]
    tblk = jnp.minimum(sub * (qi + 1), ct_ref.shape[1] - 1)
    t0 = pl.multiple_of(tblk * SEQ_TILE, SEQ_TILE)
    nt = (((1,), (1,)), ((), ()))
    lo_half = lax.broadcasted_iota(I32, (tq, LANES), 1) < IDX_DIM
    zero_b = jnp.zeros((tq, LANES), BF16)
    shift = CHUNK - N_META
    lg2 = CHUNK.bit_length() - 1
    kf = float(topk)

    def fold8(w):
        return jnp.sum(w.reshape(w.shape[0] // 8, 8, tq), axis=0)

    for h in range(IDX_HEADS):
        q2 = dqi_ref[0, :, LANES * (h // 2):LANES * (h // 2 + 1)]
        qm_ref[tq * h:tq * (h + 1), :] = (jnp.where(lo_half, q2, zero_b) if h % 2 == 0
                                           else jnp.where(lo_half, zero_b, q2))
    wrows = jnp.concatenate([wr_ref[0, j] for j in range(sub)], axis=1)

    def keys_of(kt, ks, rows):
        lg_all = lax.dot_general(kt, qm_ref[...], nt, preferred_element_type=F32)
        sc = None
        for h in range(IDX_HEADS):
            term = wrows[h:h + 1, :] * jnp.maximum(lg_all[:, tq * h:tq * (h + 1)], 0.0)
            sc = term if sc is None else sc + term
        bits = pltpu.bitcast(sc, I32)
        bits = jnp.where(bits == INT_MIN, 0, bits)
        skey = bits ^ ((bits >> 31) & 0x7FFFFFFF)
        kcid = (ks + lax.broadcasted_iota(I32, (rows, tq), 0) + shift) >> lg2
        qcid = (qi * tq + lax.broadcasted_iota(I32, (rows, tq), 1) + shift) >> lg2
        return skey, kcid <= qcid

    def score_step(g, carry):
        ks = pl.multiple_of(g * tk, tk)
        skey, adm = keys_of(ki_ref[0, pl.ds(ks, tk), :], ks, tk)
        key = jnp.where(adm, skey, INT_MIN)
        key_ref[g] = key
        hi_ref[g] = (key >> 16).astype(I16)
        lo_ref[g] = ((key & 0xFFFF) - 32768).astype(I16)
        return carry

    lax.fori_loop(0, ngrp, score_step, 0)
    skey, adm = keys_of(ki_ref[0, pl.ds(t0, TAIL), :], t0, TAIL)
    tkey = jnp.where(adm & has_tail, skey, INT_MIN)

    one_b = jnp.ones((), BF16)
    zero_s = jnp.zeros((), BF16)

    def count_ge(x16, s16):
        w = jnp.where(x16 >= s16, one_b, zero_s)
        w3 = w.reshape(x16.shape[0] // 16, 16, tq)
        return functools.reduce(lambda a, b: a + b, [w3[r] for r in range(w3.shape[0])]).astype(F32)

    def bisect16(ref, tail16, need):
        def bit_step(i, u):
            uc = u | lax.shift_left(jnp.int32(1), 15 - i)
            s16 = (uc - 32768).astype(I16)

            def cnt_step(g, acc):
                return acc + count_ge(ref[g], s16)

            acc = lax.fori_loop(0, ngrp, cnt_step, count_ge(tail16, s16))
            cnt = jnp.sum(acc, axis=0, keepdims=True)
            return jnp.where(cnt >= need, uc, u)

        return lax.fori_loop(0, 16, bit_step, jnp.zeros((1, tq), I32))

    thi = tkey >> 16
    tlo = (tkey & 0xFFFF) - 32768
    u_hi = bisect16(hi_ref, thi.astype(I16), kf)
    t_hi = u_hi - 32768
    above16 = (t_hi + 1).astype(I16)

    def above_step(g, acc):
        return acc + count_ge(hi_ref[g], above16)

    n_above = jnp.sum(lax.fori_loop(0, ngrp, above_step, count_ge(thi.astype(I16), above16)), axis=0, keepdims=True)
    n_above = jnp.where(t_hi == 32767, 0.0, n_above)
    t_hi16 = t_hi.astype(I16)

    def narrow_step(g, carry):
        lo_ref[g] = jnp.where(hi_ref[g] == t_hi16, lo_ref[g], jnp.int16(-32768))
        return carry

    lax.fori_loop(0, ngrp, narrow_step, 0)
    tlo16 = jnp.where(thi == t_hi, tlo, -32768).astype(I16)
    u_lo = bisect16(lo_ref, tlo16, kf - n_above)
    thr = t_hi * 65536 + u_lo

    def gt_step(g, acc):
        return acc + fold8(jnp.where(key_ref[g] > thr, 1.0, 0.0))

    ngt = jnp.sum(lax.fori_loop(0, ngrp, gt_step, fold8(jnp.where(tkey > thr, 1.0, 0.0))),
                  axis=0, keepdims=True)
    room = kf - ngt
    incl = jnp.where(lax.broadcasted_iota(I32, (tk, tk), 1) <= lax.broadcasted_iota(I32, (tk, tk), 0),
                     1.0, 0.0).astype(BF16)
    incl_tail = jnp.where(lax.broadcasted_iota(I32, (TAIL, TAIL), 1) <= lax.broadcasted_iota(I32, (TAIL, TAIL), 0),
                          1.0, 0.0).astype(BF16)

    def bias_of(key, seen):
        rows = key.shape[0]
        eqf = jnp.where(key == thr, 1.0, 0.0)
        rank = jnp.dot(incl if rows == tk else incl_tail, eqf.astype(BF16), preferred_element_type=F32) + seen
        tie = jnp.where(rank <= room, eqf, 0.0)
        sel = jnp.where(key > thr, 1.0, tie)
        sel = jnp.where(key == INT_MIN, 0.0, sel)
        return jnp.where(sel > 0.5, 0.0, -jnp.inf), seen + jnp.sum(fold8(eqf), axis=0, keepdims=True)

    def mask_step(g, seen):
        bias_ref[g], seen = bias_of(key_ref[g], seen)
        return seen

    seen = lax.fori_loop(0, ngrp, mask_step, jnp.zeros((1, tq), F32))
    tbias, _ = bias_of(tkey, seen)
    tbias_blk = jnp.concatenate([tbias, jnp.full((SEQ_TILE - TAIL, tq), -jnp.inf, F32)], axis=0)

    for h in range(N_HEADS):
        dq2 = dq_ref[0, :, LANES * (h // 2):LANES * (h // 2 + 1)]
        ql = jnp.dot(dq2, wuk_ref[h], preferred_element_type=F32) * (HEAD_DIM ** -0.5)
        qlat_ref[tq * h:tq * (h + 1), :] = ql.astype(BF16)
    acc_ref[...] = jnp.zeros_like(acc_ref)
    m_ref[...] = jnp.full(m_ref.shape, NEG_BIG, F32)

    def attend(ck, cx, bias):
        st_all = lax.dot_general(ck, qlat_ref[...], nt, preferred_element_type=F32)
        ps, als = [], []
        for h in range(N_HEADS):
            st = st_all[:, tq * h:tq * (h + 1)] + bias
            m_old = m_ref[h]
            mn = jnp.maximum(m_old, jnp.max(st, axis=0, keepdims=True))
            ps.append(jnp.exp(st - mn).astype(BF16))
            als.append(jnp.exp(m_old - mn))
            m_ref[h] = mn
        upd = jnp.dot(cx, jnp.concatenate(ps, axis=1), preferred_element_type=F32)
        acc_ref[...] = jnp.concatenate(als, axis=1) * acc_ref[...] + upd

    def att_step(g, carry):
        ks = pl.multiple_of(g * tk, tk)
        cx = jnp.concatenate([ct_ref[0, g * sub + j] for j in range(sub)], axis=1)
        attend(c_ref[0, pl.ds(ks, tk), :], cx, bias_ref[g])
        return carry

    lax.fori_loop(0, ngrp, att_step, 0)
    attend(c_ref[0, pl.ds(t0, SEQ_TILE), :], ct_ref[0, tblk], tbias_blk)

    for pr in range(N_HEADS // 2):
        outs = []
        for hh in range(2):
            a = acc_ref[:, tq * (2 * pr + hh):tq * (2 * pr + hh + 1)]
            olat = (a[:DSA_LATENT] / a[DSA_LATENT:DSA_LATENT + 1]).astype(BF16)
            outs.append(jnp.dot(wuvt_ref[2 * pr + hh], olat, preferred_element_type=F32))
        o_ref[0, :, LANES * pr:LANES * (pr + 1)] = jnp.concatenate(outs, axis=0).T.astype(BF16)


def _dsa(ub, c, ct, rowt, wuk_pad, wuv_t, *, topk, sub):
    b, lp, _ = ub.shape
    tq = sub * SEQ_TILE
    ngrp = lp // tq
    nblk = lp // SEQ_TILE
    return pl.pallas_call(
        functools.partial(_dsa_body, topk=topk, sub=sub),
        grid=(b, ngrp),
        in_specs=[
            pl.BlockSpec((1, tq, GROUP_WIDTH), lambda i, q: (i, q, 3)),
            pl.BlockSpec((1, tq, IDX_HEADS * IDX_DIM), lambda i, q: (i, q, 8)),
            pl.BlockSpec((1, lp, LANES), lambda i, q: (i, 0, 18)),
            pl.BlockSpec((1, lp, DSA_LATENT), lambda i, q: (i, 0, 0)),
            pl.BlockSpec((1, nblk, CT_ROWS, SEQ_TILE), lambda i, q: (i, 0, 0, 0)),
            pl.BlockSpec((1, sub, N_HEADS, SEQ_TILE), lambda i, q: (i, q, SM_W // N_HEADS, 0)),
            pl.BlockSpec((N_HEADS, LANES, DSA_LATENT), lambda i, q: (0, 0, 0)),
            pl.BlockSpec((N_HEADS, HEAD_DIM, DSA_LATENT), lambda i, q: (0, 0, 0)),
        ],
        out_specs=pl.BlockSpec((1, tq, GROUP_WIDTH), lambda i, q: (i, q, 0)),
        out_shape=jax.ShapeDtypeStruct((b, lp, GROUP_WIDTH), BF16),
        scratch_shapes=[
            pltpu.VMEM((ngrp, tq, tq), I32),
            pltpu.VMEM((ngrp, tq, tq), F32),
            pltpu.VMEM((ngrp, tq, tq), I16),
            pltpu.VMEM((ngrp, tq, tq), I16),
            pltpu.VMEM((IDX_HEADS * tq, LANES), BF16),
            pltpu.VMEM((N_HEADS * tq, DSA_LATENT), BF16),
            pltpu.VMEM((CT_ROWS, N_HEADS * tq), F32),
            pltpu.VMEM((N_HEADS, 1, tq), F32),
        ],
        compiler_params=pltpu.CompilerParams(
            dimension_semantics=("parallel", "arbitrary"), vmem_limit_bytes=_vmem(48)),
        name="dsa",
    )(ub, ub, ub, c, ct, rowt, wuk_pad, wuv_t)


def _outproj_body(ya_ref, yb_ref, yc_ref, yd_ref, x_ref, w_ref, g_ref, o_ref):
    acc = jnp.dot(ya_ref[...], w_ref[0], preferred_element_type=F32)
    acc = acc + jnp.dot(yb_ref[...], w_ref[1], preferred_element_type=F32)
    acc = acc + jnp.dot(yc_ref[...], w_ref[2], preferred_element_type=F32)
    acc = acc + jnp.dot(yd_ref[...], w_ref[3], preferred_element_type=F32)
    ms = jnp.mean(acc * acc, axis=-1, keepdims=True)
    o_ref[...] = x_ref[...] + acc * lax.rsqrt(ms + EPS) * g_ref[...]


def _out_proj(ys, x2d, w4, g, *, layer, tm):
    m, d = x2d.shape
    gw = GROUP_WIDTH
    yspec = pl.BlockSpec((tm, gw), lambda i: (i, 0))
    return pl.pallas_call(
        _outproj_body,
        grid=(m // tm,),
        in_specs=[yspec, yspec, yspec, yspec,
                  pl.BlockSpec((tm, d), lambda i: (i, 0)),
                  pl.BlockSpec((None, 4, gw, d), lambda i: (layer, 0, 0, 0)),
                  pl.BlockSpec((1, d), lambda i: (0, 0))],
        out_specs=pl.BlockSpec((tm, d), lambda i: (i, 0)),
        out_shape=jax.ShapeDtypeStruct((m, d), F32),
        compiler_params=pltpu.CompilerParams(
            dimension_semantics=("parallel",), vmem_limit_bytes=_vmem(48)),
        name="out_proj",
    )(*ys, x2d, w4, g)


def _ffn_body(x_ref, gpre_ref, wg_ref, wu_ref, cw_ref, cb_ref, wd_ref, gpost_ref, o_ref,
              xn_ref, acc_ref, gbuf_ref, carry_ref, *, tiles_per_seq):
    i = pl.program_id(0)
    f = pl.program_id(1)
    nf = pl.num_programs(1)
    tm = x_ref.shape[0]

    @pl.when(f == 0)
    def _():
        x = x_ref[...]
        ms = jnp.mean(x * x, axis=-1, keepdims=True)
        xn_ref[...] = (x * lax.rsqrt(ms + EPS) * gpre_ref[...]).astype(BF16)
        acc_ref[...] = jnp.zeros_like(acc_ref)

    @pl.when(i % tiles_per_seq == 0)
    def _():
        carry_ref[f] = jnp.zeros(carry_ref.shape[1:], F32)

    xn = xn_ref[...]
    g = jnp.dot(xn, wg_ref[...], preferred_element_type=F32)
    u = jnp.dot(xn, wu_ref[...], preferred_element_type=F32)
    gbuf_ref[0:8, :] = carry_ref[f]
    gbuf_ref[8:8 + tm, :] = g
    conv = (cb_ref[...]
            + cw_ref[0:1, :] * gbuf_ref[6:6 + tm, :]
            + cw_ref[1:2, :] * gbuf_ref[7:7 + tm, :]
            + cw_ref[2:3, :] * g)
    carry_ref[f] = g[tm - 8:tm, :]
    a = (_silu(conv) * u).astype(BF16)
    acc_ref[...] += jnp.dot(a, wd_ref[...], preferred_element_type=F32)

    @pl.when(f == nf - 1)
    def _():
        y = acc_ref[...]
        ms = jnp.mean(y * y, axis=-1, keepdims=True)
        o_ref[...] = x_ref[...] + y * lax.rsqrt(ms + EPS) * gpost_ref[...]


def _ffn(x2d, g_pre, w_gate, w_up, conv_w, conv_b, w_down, g_post, *, layer, tm, tf, tiles_per_seq):
    m, d = x2d.shape
    fdim = w_gate.shape[-1]
    nf = fdim // tf
    return pl.pallas_call(
        functools.partial(_ffn_body, tiles_per_seq=tiles_per_seq),
        grid=(m // tm, nf),
        in_specs=[
            pl.BlockSpec((tm, d), lambda i, f: (i, 0)),
            pl.BlockSpec((1, d), lambda i, f: (0, 0)),
            pl.BlockSpec((None, d, tf), lambda i, f: (layer, 0, f)),
            pl.BlockSpec((None, d, tf), lambda i, f: (layer, 0, f)),
            pl.BlockSpec((FFN_CONV, tf), lambda i, f: (0, f)),
            pl.BlockSpec((1, tf), lambda i, f: (0, f)),
            pl.BlockSpec((None, tf, d), lambda i, f: (layer, f, 0)),
            pl.BlockSpec((1, d), lambda i, f: (0, 0)),
        ],
        out_specs=pl.BlockSpec((tm, d), lambda i, f: (i, 0)),
        out_shape=jax.ShapeDtypeStruct((m, d), F32),
        scratch_shapes=[
            pltpu.VMEM((tm, d), BF16),
            pltpu.VMEM((tm, d), F32),
            pltpu.VMEM((8 + tm, tf), F32),
            pltpu.VMEM((nf, 8, tf), F32),
        ],
        compiler_params=pltpu.CompilerParams(
            dimension_semantics=("arbitrary", "arbitrary"), vmem_limit_bytes=_vmem(56)),
        name="ffn",
    )(x2d, g_pre, w_gate, w_up, conv_w, conv_b, w_down, g_post)


def _permute_w_in(w_in):
    gw = GROUP_WIDTH
    sizes = (gw, SSD_XBC, N_HEADS, gw, 3 * gw, N_HEADS, gw, DSA_LATENT, IDX_HEADS * IDX_DIM, IDX_DIM, IDX_HEADS)
    offs = [0]
    for s in sizes:
        offs.append(offs[-1] + s)
    z, xbc, dt, pool, qkv, fl, dq, dc, dqi, dki, dwi = (w_in[..., offs[k]:offs[k + 1]] for k in range(len(sizes)))
    zeros = lambda n: jnp.zeros(w_in.shape[:-1] + (n,), w_in.dtype)
    small = jnp.concatenate([dt, fl, dwi, zeros(SM_A - SM_W - IDX_HEADS), dt, zeros(LANES - SM_A - N_HEADS)], axis=-1)
    a = jnp.concatenate([xbc, z, pool, dc, small, zeros(A_COLS - 2 * gw - SSD_XBC - DSA_LATENT - LANES)], axis=-1)
    bcols = jnp.concatenate([qkv, dq, dqi, dki, dki], axis=-1)
    bcols = jnp.concatenate([bcols, zeros(B_COLS - bcols.shape[-1])], axis=-1)
    return jnp.concatenate([a, bcols], axis=-1)


def _lane_vec(pieces):
    v = jnp.zeros((LANES,), F32)
    for off, val in pieces:
        v = v.at[off:off + val.shape[0]].set(val.astype(F32))
    return v[None, :]


def _pad_head_weights(w_uk, w_uv):
    h, r, d = w_uk.shape
    uk = jnp.zeros((h, 2 * d, r), F32)
    for i in range(h):
        o = d * (i % 2)
        uk = uk.at[i, o:o + d, :].set(w_uk[i].T)
    return uk.astype(BF16), jnp.swapaxes(w_uv, 1, 2).astype(BF16)


def _tile_sizes(b, lp):
    m = b * lp
    tm_proj = next(t for t in (1024, 512, 256, 128) if m % t == 0)
    tm_out = next(t for t in (512, 256, 128) if m % t == 0)
    tm_ffn = next(t for t in (528, 384, 320, 256, 128) if lp % t == 0)
    att_sub = 3 if lp % (3 * SEQ_TILE) == 0 else 1
    return tm_proj, tm_out, tm_ffn, att_sub


def _layer(h, p, big, layer, *, topk):
    b, lp, d = h.shape
    m = b * lp
    tm_proj, tm_out, tm_ffn, att_sub = _tile_sizes(b, lp)
    row = lambda v: v.astype(F32)[None, :]

    ua, ub = _in_proj(h.reshape(m, d), row(p["norm_mix_pre"]), big["w_in"], layer=layer, tm=tm_proj, tn=1280)
    ua = ua.reshape(b, lp, A_COLS)
    ub = ub.reshape(b, lp, B_COLS)

    bias_vec = _lane_vec([(SM_DT, p["ssd_dt_bias"]), (SM_F, p["fox_f_bias"]), (SM_A, p["ssd_dt_bias"])])
    wscale = jnp.full((IDX_HEADS,), (IDX_HEADS ** -0.5) * (IDX_DIM ** -0.5), F32)
    mul_vec = _lane_vec([(SM_W, wscale), (SM_A, -jnp.exp(p["ssd_a_log"].astype(F32)))])
    col, rowt, c, ct, fkp = _prep(ua, bias_vec, mul_vec, row(p["dsa_kv_norm"]))

    dskip_full = jnp.repeat(p["ssd_d"].astype(F32), HEAD_DIM)[None, :]
    y_a = _ssd(ua, col, rowt, p["ssd_conv_w"].astype(F32), row(p["ssd_conv_b"]), dskip_full, row(p["ssd_norm"]))
    y_b = _pool(ua, p["pool_w"].astype(BF16), row(p["pool_scale"]), sub=att_sub)
    y_c = _fox(ub, fkp, sub=att_sub)
    wuk_pad, wuv_t = _pad_head_weights(p["dsa_w_uk"], p["dsa_w_uv"])
    y_d = _dsa(ub, c, ct, rowt, wuk_pad, wuv_t, topk=topk, sub=att_sub)

    ys = [y.reshape(m, GROUP_WIDTH) for y in (y_a, y_b, y_c, y_d)]
    x1 = _out_proj(ys, h.reshape(m, d), big["w_out"], row(p["norm_mix_post"]), layer=layer, tm=tm_out)
    x2 = _ffn(x1, row(p["norm_ffn_pre"]), big["ffn_w_gate"], big["ffn_w_up"],
              p["ffn_conv_w"].astype(F32), row(p["ffn_conv_b"]), big["ffn_w_down"],
              row(p["norm_ffn_post"]), layer=layer, tm=tm_ffn, tf=512, tiles_per_seq=lp // tm_ffn)
    return x2.reshape(b, lp, d)


def _stack_big_weights(w_in, w_out, ffn_w_gate, ffn_w_up, ffn_w_down):
    depth, d, _ = w_out.shape
    return dict(w_in=_to_bf16(_permute_w_in(w_in)),
                w_out=_to_bf16(w_out).reshape(depth, 4, GROUP_WIDTH, d),
                ffn_w_gate=_to_bf16(ffn_w_gate), ffn_w_up=_to_bf16(ffn_w_up), ffn_w_down=_to_bf16(ffn_w_down))


def kernel(x, meta_tokens, norm_mix_pre, norm_mix_post, norm_ffn_pre, norm_ffn_post, w_in, ssd_conv_w, ssd_conv_b, ssd_dt_bias, ssd_a_log, ssd_d, ssd_norm, pool_w, pool_scale, fox_f_bias, dsa_kv_norm, dsa_w_uk, dsa_w_uv, w_out, ffn_w_gate, ffn_w_up, ffn_conv_w, ffn_conv_b, ffn_w_down):
    bsz, seq, d = x.shape
    n = N_META + seq
    lp = -(-n // SEQ_TILE) * SEQ_TILE
    topk = min(DSA_TOPK_MAX, seq // 4)
    meta = jnp.broadcast_to(meta_tokens.astype(x.dtype)[None], (bsz, N_META, d))
    h = jnp.concatenate([meta, x, jnp.zeros((bsz, lp - n, d), x.dtype)], axis=1)
    small = dict(norm_mix_pre=norm_mix_pre, norm_mix_post=norm_mix_post, norm_ffn_pre=norm_ffn_pre,
                 norm_ffn_post=norm_ffn_post, ssd_conv_w=ssd_conv_w, ssd_conv_b=ssd_conv_b,
                 ssd_dt_bias=ssd_dt_bias, ssd_a_log=ssd_a_log, ssd_d=ssd_d, ssd_norm=ssd_norm,
                 pool_w=pool_w, pool_scale=pool_scale, fox_f_bias=fox_f_bias, dsa_kv_norm=dsa_kv_norm,
                 dsa_w_uk=dsa_w_uk, dsa_w_uv=dsa_w_uv, ffn_conv_w=ffn_conv_w, ffn_conv_b=ffn_conv_b)
    big = _stack_big_weights(w_in, w_out, ffn_w_gate, ffn_w_up, ffn_w_down)
    for i in range(norm_mix_pre.shape[0]):
        h = _layer(h, {k: v[i] for k, v in small.items()}, big, i, topk=topk)
    return h[:, N_META:n]
```

```python
import functools

import jax
import jax.numpy as jnp
from jax import lax
from jax.experimental import pallas as pl
from jax.experimental.pallas import tpu as pltpu

F32 = jnp.float32
BF16 = jnp.bfloat16
I32 = jnp.int32
I16 = jnp.int16

EPS = 1e-6
N_META = 16
CHUNK = 64
HEAD_DIM = 64
GROUP_WIDTH = 512
N_HEADS = 8
SSD_GROUPS = 2
SSD_STATE = 128
SSD_CONV = 4
SSD_XBC = GROUP_WIDTH + 2 * SSD_GROUPS * SSD_STATE
POOL_WINDOWS = (2, 4, 8, 16)
DSA_LATENT = 128
IDX_HEADS = 4
IDX_DIM = 64
DSA_TOPK_MAX = 256
FFN_CONV = 3

LANES = 128
SEQ_TILE = 128
INT_MIN = -(2 ** 31)
NEG_BIG = -1e30

A_COLS = 2560
B_COLS = 2560
SM_DT = 0
SM_F = 8
SM_W = 16
SM_A = 24
CT_ROWS = DSA_LATENT + 16
ACC_ROWS = 2 * HEAD_DIM + 16
TAIL = N_META


def _vmem(mb):
    return int(mb * 1024 * 1024)


def _softplus_parts(x):
    t = jnp.log1p(jnp.exp(-jnp.abs(x)))
    return jnp.maximum(x, 0.0) + t, jnp.minimum(x, 0.0) - t


def _silu(x):
    return x / (1.0 + jnp.exp(-x))


def _cast_body(x_ref, o_ref):
    o_ref[...] = x_ref[...].astype(BF16)


def _to_bf16(w, *, tr=256):
    shape = w.shape
    w2 = w.reshape(-1, shape[-1])
    r, c = w2.shape
    out = pl.pallas_call(
        _cast_body,
        grid=(r // tr,),
        in_specs=[pl.BlockSpec((tr, c), lambda i: (i, 0))],
        out_specs=pl.BlockSpec((tr, c), lambda i: (i, 0)),
        out_shape=jax.ShapeDtypeStruct((r, c), BF16),
        compiler_params=pltpu.CompilerParams(dimension_semantics=("parallel",), vmem_limit_bytes=_vmem(40)),
        name="to_bf16",
    )(w2)
    return out.reshape(shape)


def _inproj_body(x_ref, g_ref, w_ref, oa_ref, ob_ref, xn_ref, *, n_a):
    j = pl.program_id(1)

    @pl.when(j == 0)
    def _():
        x = x_ref[...]
        ms = jnp.mean(x * x, axis=-1, keepdims=True)
        xn_ref[...] = (x * lax.rsqrt(ms + EPS) * g_ref[...]).astype(BF16)

    acc = jnp.dot(xn_ref[...], w_ref[...], preferred_element_type=F32)

    @pl.when(j < n_a)
    def _():
        oa_ref[...] = acc

    @pl.when(j >= n_a)
    def _():
        ob_ref[...] = acc.astype(BF16)


def _in_proj(x2d, g, w_perm, *, layer, tm, tn):
    m, d = x2d.shape
    n_a, n_b = A_COLS // tn, B_COLS // tn
    return pl.pallas_call(
        functools.partial(_inproj_body, n_a=n_a),
        grid=(m // tm, n_a + n_b),
        in_specs=[
            pl.BlockSpec((tm, d), lambda i, j: (i, 0)),
            pl.BlockSpec((1, d), lambda i, j: (0, 0)),
            pl.BlockSpec((None, d, tn), lambda i, j: (layer, 0, j)),
        ],
        out_specs=[
            pl.BlockSpec((tm, tn), lambda i, j: (i, jnp.minimum(j, n_a - 1))),
            pl.BlockSpec((tm, tn), lambda i, j: (i, jnp.maximum(j - n_a, 0))),
        ],
        out_shape=[
            jax.ShapeDtypeStruct((m, A_COLS), F32),
            jax.ShapeDtypeStruct((m, B_COLS), BF16),
        ],
        scratch_shapes=[pltpu.VMEM((tm, d), BF16)],
        compiler_params=pltpu.CompilerParams(
            dimension_semantics=("parallel", "arbitrary"), vmem_limit_bytes=_vmem(58)),
        name="in_proj",
    )(x2d, g, w_perm)


def _prep_body(sm_ref, dc_ref, bias_ref, mul_ref, kvg_ref, col_ref, row_ref, c_ref, ct_ref, fkp_ref,
               carry_ref):
    t = pl.program_id(1)

    @pl.when(t == 0)
    def _():
        carry_ref[...] = jnp.zeros_like(carry_ref)

    tt = sm_ref.shape[1]
    s = sm_ref[0]
    lane = lax.broadcasted_iota(I32, (tt, LANES), 1)
    is_dt = lane < SM_F
    is_f = (lane >= SM_F) & (lane < SM_W)
    is_a = (lane >= SM_A) & (lane < SM_A + N_HEADS)
    sp, ls = _softplus_parts(s + bias_ref[...])
    v = jnp.where(is_dt, sp, jnp.where(is_f, ls, jnp.where(is_a, sp, s) * mul_ref[...]))
    ri = lax.broadcasted_iota(I32, (tt, tt), 0)
    ci = lax.broadcasted_iota(I32, (tt, tt), 1)
    tril = jnp.where(ci <= ri, 1.0, 0.0).astype(F32)
    local = jnp.dot(tril, v, precision=lax.Precision.HIGHEST, preferred_element_type=F32)
    out = jnp.where(is_f, local + carry_ref[...], jnp.where(is_a, local, v))
    col_ref[0] = out
    row_ref[0, 0] = out.T
    carry_ref[...] = jnp.where(is_f[0:1], out[tt - 1:tt, :], 0.0)

    f0 = jnp.where(is_f, out, 0.0)
    hi = f0.astype(BF16).astype(F32)
    r1 = f0 - hi
    mid = r1.astype(BF16).astype(F32)
    lo = (r1 - mid).astype(BF16).astype(F32)
    fkp_ref[0] = (hi + pltpu.roll(mid, N_HEADS, axis=1) + pltpu.roll(lo, 2 * N_HEADS, axis=1)).astype(BF16)

    dc = dc_ref[0]
    ms = jnp.mean(dc * dc, axis=-1, keepdims=True)
    cn = dc * lax.rsqrt(ms + EPS) * kvg_ref[...]
    c_ref[0] = cn.astype(BF16)
    ct_ref[0, 0] = jnp.concatenate([cn.T, jnp.ones((CT_ROWS - DSA_LATENT, tt), F32)], axis=0).astype(BF16)


def _prep(ua, bias_vec, mul_vec, kv_g):
    b, lp, _ = ua.shape
    tt = SEQ_TILE
    nblk = lp // tt
    return pl.pallas_call(
        _prep_body,
        grid=(b, nblk),
        in_specs=[
            pl.BlockSpec((1, tt, LANES), lambda i, t: (i, t, 17)),
            pl.BlockSpec((1, tt, LANES), lambda i, t: (i, t, 16)),
            pl.BlockSpec((1, LANES), lambda i, t: (0, 0)),
            pl.BlockSpec((1, LANES), lambda i, t: (0, 0)),
            pl.BlockSpec((1, LANES), lambda i, t: (0, 0)),
        ],
        out_specs=[
            pl.BlockSpec((1, tt, LANES), lambda i, t: (i, t, 0)),
            pl.BlockSpec((1, 1, LANES, tt), lambda i, t: (i, t, 0, 0)),
            pl.BlockSpec((1, tt, LANES), lambda i, t: (i, t, 0)),
            pl.BlockSpec((1, 1, CT_ROWS, tt), lambda i, t: (i, t, 0, 0)),
            pl.BlockSpec((1, tt, LANES), lambda i, t: (i, t, 0)),
        ],
        out_shape=[
            jax.ShapeDtypeStruct((b, lp, LANES), F32),
            jax.ShapeDtypeStruct((b, nblk, LANES, tt), F32),
            jax.ShapeDtypeStruct((b, lp, DSA_LATENT), BF16),
            jax.ShapeDtypeStruct((b, nblk, CT_ROWS, tt), BF16),
            jax.ShapeDtypeStruct((b, lp, LANES), BF16),
        ],
        scratch_shapes=[pltpu.VMEM((1, LANES), F32)],
        compiler_params=pltpu.CompilerParams(dimension_semantics=("parallel", "arbitrary")),
        name="prep",
    )(ua, ua, bias_vec, mul_vec, kv_g)


def _expand_heads(colv, base, lo_half):
    parts = []
    for p in range(N_HEADS // 2):
        a = colv[:, base + 2 * p:base + 2 * p + 1]
        b = colv[:, base + 2 * p + 1:base + 2 * p + 2]
        parts.append(jnp.where(lo_half, a, b))
    return jnp.concatenate(parts, axis=1)


def _ssd_body(xbc_ref, z_ref, col_ref, row_ref, cw_ref, cb_ref, dsk_ref, ng_ref, y_ref,
              xpad_ref, st_ref):
    c = pl.program_id(1)
    ll = xbc_ref.shape[1]
    gw = GROUP_WIDTH
    ns = SSD_STATE
    hpg = N_HEADS // SSD_GROUPS
    gcols = hpg * HEAD_DIM

    @pl.when(c == 0)
    def _():
        xpad_ref[0:8, :] = jnp.zeros((8, SSD_XBC), F32)
        st_ref[...] = jnp.zeros_like(st_ref)

    x = xbc_ref[0]
    xpad_ref[8:8 + ll, :] = x
    conv = (cb_ref[...]
            + cw_ref[0:1, :] * xpad_ref[5:5 + ll, :]
            + cw_ref[1:2, :] * xpad_ref[6:6 + ll, :]
            + cw_ref[2:3, :] * xpad_ref[7:7 + ll, :]
            + cw_ref[3:4, :] * x)
    xpad_ref[0:8, :] = x[ll - 8:ll, :]
    act = _silu(conv)
    xs = act[:, 0:gw]
    bm = act[:, gw:gw + SSD_GROUPS * ns]
    cm = act[:, gw + SSD_GROUPS * ns:]

    colv = col_ref[0]
    rowv = row_ref[0, 0]
    lo_half = lax.broadcasted_iota(I32, (ll, LANES), 1) < HEAD_DIM
    dt_full = _expand_heads(colv, SM_DT, lo_half)
    acs_full = _expand_heads(colv, SM_A, lo_half)
    acs_last = acs_full[ll - 1:ll, :]
    dte_full = jnp.exp(acs_last - acs_full)
    dfs_full = jnp.exp(acs_full)
    xdt = xs * dt_full
    xdt_b = xdt.astype(BF16)
    xdte_b = (xdt * dte_full).astype(BF16)
    cm_b = cm.astype(BF16)
    ri = lax.broadcasted_iota(I32, (ll, ll), 0)
    ci = lax.broadcasted_iota(I32, (ll, ll), 1)
    tril = ci <= ri

    ys = []
    for g in range(SSD_GROUPS):
        bg = bm[:, ns * g:ns * (g + 1)]
        bg_b = bg.astype(BF16)
        bgt_b = bg.T.astype(BF16)
        cg_b = cm_b[:, ns * g:ns * (g + 1)]
        cb = lax.dot_general(cg_b, bg_b, (((1,), (1,)), ((), ())), preferred_element_type=F32)
        sg = st_ref[g]
        yoff = (jnp.dot(cg_b, sg.astype(BF16), preferred_element_type=F32)
                * dfs_full[:, gcols * g:gcols * (g + 1)])
        parts = []
        for pr in range(hpg // 2):
            xpair = xdt_b[:, gcols * g + LANES * pr:gcols * g + LANES * (pr + 1)]
            res = []
            for hh in range(2):
                h = hpg * g + 2 * pr + hh
                seg = colv[:, SM_A + h:SM_A + h + 1] - rowv[SM_A + h:SM_A + h + 1, :]
                lm = jnp.exp(jnp.where(tril, seg, -jnp.inf))
                res.append(jnp.dot((cb * lm).astype(BF16), xpair, preferred_element_type=F32))
            parts.append(jnp.where(lo_half, res[0], res[1]))
        ydiag = jnp.concatenate(parts, axis=1)
        decay = jnp.exp(acs_last[:, gcols * g:gcols * (g + 1)])
        st_ref[g] = decay * sg + jnp.dot(bgt_b, xdte_b[:, gcols * g:gcols * (g + 1)],
                                         preferred_element_type=F32)
        ys.append(ydiag + yoff)

    y = jnp.concatenate(ys, axis=1) + dsk_ref[...] * xs
    gz = y * _silu(z_ref[0])
    outs = []
    for g in range(SSD_GROUPS):
        gg = gz[:, gcols * g:gcols * (g + 1)]
        outs.append(gg * lax.rsqrt(jnp.mean(gg * gg, axis=-1, keepdims=True) + EPS))
    y_ref[0] = (jnp.concatenate(outs, axis=1) * ng_ref[...]).astype(BF16)


def _ssd(ua, col, row, conv_w, conv_b, dskip_full, norm_g):
    b, lp, _ = ua.shape
    ll = SEQ_TILE
    nblk = lp // ll
    hpg = N_HEADS // SSD_GROUPS
    return pl.pallas_call(
        _ssd_body,
        grid=(b, nblk),
        in_specs=[
            pl.BlockSpec((1, ll, SSD_XBC), lambda i, c: (i, c, 0)),
            pl.BlockSpec((1, ll, GROUP_WIDTH), lambda i, c: (i, c, 2)),
            pl.BlockSpec((1, ll, LANES), lambda i, c: (i, c, 0)),
            pl.BlockSpec((1, 1, LANES, ll), lambda i, c: (i, c, 0, 0)),
            pl.BlockSpec((SSD_CONV, SSD_XBC), lambda i, c: (0, 0)),
            pl.BlockSpec((1, SSD_XBC), lambda i, c: (0, 0)),
            pl.BlockSpec((1, GROUP_WIDTH), lambda i, c: (0, 0)),
            pl.BlockSpec((1, GROUP_WIDTH), lambda i, c: (0, 0)),
        ],
        out_specs=pl.BlockSpec((1, ll, GROUP_WIDTH), lambda i, c: (i, c, 0)),
        out_shape=jax.ShapeDtypeStruct((b, lp, GROUP_WIDTH), BF16),
        scratch_shapes=[
            pltpu.VMEM((8 + ll, SSD_XBC), F32),
            pltpu.VMEM((SSD_GROUPS, SSD_STATE, hpg * HEAD_DIM), F32),
        ],
        compiler_params=pltpu.CompilerParams(dimension_semantics=("parallel", "arbitrary")),
        name="ssd",
    )(ua, ua, col, row, conv_w, conv_b, dskip_full, norm_g)


def _pool_body(u_ref, w_ref, sc_ref, y_ref, buf_ref):
    t = pl.program_id(1)
    tt = u_ref.shape[1]
    hist = max(POOL_WINDOWS)
    gd = GROUP_WIDTH // len(POOL_WINDOWS)

    @pl.when(t == 0)
    def _():
        buf_ref[0:hist, :] = jnp.zeros((hist, GROUP_WIDTH), F32)

    u = u_ref[0]
    buf_ref[hist:hist + tt, :] = u
    count = (t * tt + 1 + lax.broadcasted_iota(I32, (tt, 1), 0)).astype(F32)
    outs = []
    for gi, win in enumerate(POOL_WINDOWS):
        ug = u[:, gd * gi:gd * (gi + 1)]
        acc = ug
        for k in range(1, win):
            acc = acc + buf_ref[hist - k:hist - k + tt, gd * gi:gd * (gi + 1)]
        pooled = acc / jnp.minimum(count, float(win)) - ug
        outs.append(jnp.dot(pooled.astype(BF16), w_ref[gi], preferred_element_type=F32))
    y_ref[0] = (jnp.concatenate(outs, axis=1) * sc_ref[...]).astype(BF16)
    buf_ref[0:hist, :] = u[tt - hist:tt, :]


def _pool(ua, w, scale, *, sub):
    b, lp, _ = ua.shape
    tt = sub * SEQ_TILE
    ng = len(POOL_WINDOWS)
    gd = GROUP_WIDTH // ng
    return pl.pallas_call(
        _pool_body,
        grid=(b, lp // tt),
        in_specs=[
            pl.BlockSpec((1, tt, GROUP_WIDTH), lambda i, t: (i, t, 3)),
            pl.BlockSpec((ng, gd, gd), lambda i, t: (0, 0, 0)),
            pl.BlockSpec((1, GROUP_WIDTH), lambda i, t: (0, 0)),
        ],
        out_specs=pl.BlockSpec((1, tt, GROUP_WIDTH), lambda i, t: (i, t, 0)),
        out_shape=jax.ShapeDtypeStruct((b, lp, GROUP_WIDTH), BF16),
        scratch_shapes=[pltpu.VMEM((max(POOL_WINDOWS) + tt, GROUP_WIDTH), F32)],
        compiler_params=pltpu.CompilerParams(dimension_semantics=("parallel", "arbitrary")),
        name="pool",
    )(ua, w, scale)


def _fox_body(q_ref, k_ref, v_ref, fkp_ref, o_ref, qa_ref, vl_ref, acc_ref, m_ref):
    qi = pl.program_id(1)
    tq = q_ref.shape[1]
    tk = tq
    nt = (((1,), (1,)), ((), ()))
    npair = N_HEADS // 2
    lane = lax.broadcasted_iota(I32, (tq, LANES), 1)
    lo_half = lane < HEAD_DIM
    zero_b = jnp.zeros((tq, LANES), BF16)

    @pl.when(qi == 0)
    def _():
        nkt = vl_ref.shape[0]
        rsel = lax.broadcasted_iota(I32, (ACC_ROWS - 2 * HEAD_DIM, 2 * tk), 0)
        csel = lax.broadcasted_iota(I32, (ACC_ROWS - 2 * HEAD_DIM, 2 * tk), 1)
        ones_rows = jnp.where((rsel == 0) & (csel < tk) | (rsel == 1) & (csel >= tk), 1.0, 0.0).astype(BF16)
        zpad = jnp.zeros((HEAD_DIM, tk), BF16)

        def fill(kt, carry):
            ks = pl.multiple_of(kt * tk, tk)
            for pr in range(npair):
                vt = v_ref[0, pl.ds(ks, tk), LANES * pr:LANES * (pr + 1)].astype(F32).T.astype(BF16)
                top = jnp.concatenate([vt[:HEAD_DIM], zpad], axis=1)
                bot = jnp.concatenate([zpad, vt[HEAD_DIM:]], axis=1)
                vl_ref[kt, pr] = jnp.concatenate([top, bot, ones_rows], axis=0)
            return carry

        lax.fori_loop(0, nkt, fill, 0)

    for pr in range(npair):
        q2 = q_ref[0, :, LANES * pr:LANES * (pr + 1)] * jnp.asarray(HEAD_DIM ** -0.5, BF16)
        for hh in range(2):
            h = 2 * pr + hh
            qm = jnp.where(lo_half, q2, zero_b) if hh == 0 else jnp.where(lo_half, zero_b, q2)
            pick = (lane == SM_F + h) | (lane == SM_F + N_HEADS + h) | (lane == SM_F + 2 * N_HEADS + h)
            qa_ref[h] = jnp.concatenate([qm, jnp.where(pick, -1.0, 0.0).astype(BF16)], axis=1)
    acc_ref[...] = jnp.zeros_like(acc_ref)
    m_ref[...] = jnp.full(m_ref.shape, NEG_BIG, F32)
    causal_t = (lax.broadcasted_iota(I32, (tk, tq), 0) <= lax.broadcasted_iota(I32, (tk, tq), 1))
    tail_row = lax.broadcasted_iota(I32, (ACC_ROWS - 2 * HEAD_DIM, tq), 0)

    def step(kt, masked):
        ks = pl.multiple_of(kt * tk, tk)
        fkp = fkp_ref[0, pl.ds(ks, tk), :]
        for pr in range(npair):
            ka = jnp.concatenate([k_ref[0, pl.ds(ks, tk), LANES * pr:LANES * (pr + 1)], fkp], axis=1)
            ps, als = [], []
            for hh in range(2):
                h = 2 * pr + hh
                st = lax.dot_general(ka, qa_ref[h], nt, preferred_element_type=F32)
                if masked:
                    st = jnp.where(causal_t, st, -jnp.inf)
                m_old = m_ref[h]
                mn = jnp.maximum(m_old, jnp.max(st, axis=0, keepdims=True))
                ps.append(jnp.exp(st - mn).astype(BF16))
                als.append(jnp.exp(m_old - mn))
                m_ref[h] = mn
            upd = jnp.dot(vl_ref[kt, pr], jnp.concatenate(ps, axis=0), preferred_element_type=F32)
            scale = jnp.concatenate([jnp.broadcast_to(als[0], (HEAD_DIM, tq)),
                                     jnp.broadcast_to(als[1], (HEAD_DIM, tq)),
                                     jnp.where(tail_row == 0, als[0], als[1])], axis=0)
            acc_ref[pr] = scale * acc_ref[pr] + upd

    def body(kt, carry):
        step(kt, False)
        return carry

    lax.fori_loop(0, qi, body, 0)
    step(qi, True)
    for pr in range(npair):
        a = acc_ref[pr]
        o = jnp.concatenate([a[:HEAD_DIM] / a[2 * HEAD_DIM:2 * HEAD_DIM + 1],
                             a[HEAD_DIM:2 * HEAD_DIM] / a[2 * HEAD_DIM + 1:2 * HEAD_DIM + 2]], axis=0)
        o_ref[0, :, LANES * pr:LANES * (pr + 1)] = o.T.astype(BF16)


def _fox(ub, fkp, *, sub):
    b, lp, _ = ub.shape
    tq = sub * SEQ_TILE
    return pl.pallas_call(
        _fox_body,
        grid=(b, lp // tq),
        in_specs=[
            pl.BlockSpec((1, tq, GROUP_WIDTH), lambda i, q: (i, q, 0)),
            pl.BlockSpec((1, lp, GROUP_WIDTH), lambda i, q: (i, 0, 1)),
            pl.BlockSpec((1, lp, GROUP_WIDTH), lambda i, q: (i, 0, 2)),
            pl.BlockSpec((1, lp, LANES), lambda i, q: (i, 0, 0)),
        ],
        out_specs=pl.BlockSpec((1, tq, GROUP_WIDTH), lambda i, q: (i, q, 0)),
        out_shape=jax.ShapeDtypeStruct((b, lp, GROUP_WIDTH), BF16),
        scratch_shapes=[
            pltpu.VMEM((N_HEADS, tq, 2 * LANES), BF16),
            pltpu.VMEM((lp // tq, N_HEADS // 2, ACC_ROWS, 2 * tq), BF16),
            pltpu.VMEM((N_HEADS // 2, ACC_ROWS, tq), F32),
            pltpu.VMEM((N_HEADS, 1, tq), F32),
        ],
        compiler_params=pltpu.CompilerParams(
            dimension_semantics=("parallel", "arbitrary"), vmem_limit_bytes=_vmem(48)),
        name="fox",
    )(ub, ub, ub, fkp)


def _dsa_body(dq_ref, dqi_ref, ki_ref, c_ref, ct_ref, wr_ref, wuk_ref, wuvt_ref, o_ref,
              key_ref, bias_ref, hi_ref, lo_ref, qm_ref, qlat_ref, acc_ref, m_ref, *, topk, sub):
    qi = pl.program_id(1)
    tq = dq_ref.shape[1]
    tk = tq
    ngrp = qi + 1
    has_tail = qi + 1 < key_ref.shape[0]
    tblk = jnp.minimum(sub * (qi + 1), ct_ref.shape[1] - 1)
    t0 = pl.multiple_of(tblk * SEQ_TILE, SEQ_TILE)
    nt = (((1,), (1,)), ((), ()))
    lo_half = lax.broadcasted_iota(I32, (tq, LANES), 1) < IDX_DIM
    zero_b = jnp.zeros((tq, LANES), BF16)
    shift = CHUNK - N_META
    lg2 = CHUNK.bit_length() - 1
    kf = float(topk)

    def fold8(w):
        return jnp.sum(w.reshape(w.shape[0] // 8, 8, tq), axis=0)

    for h in range(IDX_HEADS):
        q2 = dqi_ref[0, :, LANES * (h // 2):LANES * (h // 2 + 1)]
        qm_ref[tq * h:tq * (h + 1), :] = (jnp.where(lo_half, q2, zero_b) if h % 2 == 0
                                           else jnp.where(lo_half, zero_b, q2))
    wrows = jnp.concatenate([wr_ref[0, j] for j in range(sub)], axis=1)

    def keys_of(kt, ks, rows):
        lg_all = lax.dot_general(kt, qm_ref[...], nt, preferred_element_type=F32)
        sc = None
        for h in range(IDX_HEADS):
            term = wrows[h:h + 1, :] * jnp.maximum(lg_all[:, tq * h:tq * (h + 1)], 0.0)
            sc = term if sc is None else sc + term
        bits = pltpu.bitcast(sc, I32)
        bits = jnp.where(bits == INT_MIN, 0, bits)
        skey = bits ^ ((bits >> 31) & 0x7FFFFFFF)
        kcid = (ks + lax.broadcasted_iota(I32, (rows, tq), 0) + shift) >> lg2
        qcid = (qi * tq + lax.broadcasted_iota(I32, (rows, tq), 1) + shift) >> lg2
        return skey, kcid <= qcid

    def score_step(g, carry):
        ks = pl.multiple_of(g * tk, tk)
        skey, adm = keys_of(ki_ref[0, pl.ds(ks, tk), :], ks, tk)
        key = jnp.where(adm, skey, INT_MIN)
        key_ref[g] = key
        hi_ref[g] = (key >> 16).astype(I16)
        lo_ref[g] = ((key & 0xFFFF) - 32768).astype(I16)
        return carry

    lax.fori_loop(0, ngrp, score_step, 0)
    skey, adm = keys_of(ki_ref[0, pl.ds(t0, TAIL), :], t0, TAIL)
    tkey = jnp.where(adm & has_tail, skey, INT_MIN)

    one_b = jnp.ones((), BF16)
    zero_s = jnp.zeros((), BF16)

    def count_ge(x16, s16):
        w = jnp.where(x16 >= s16, one_b, zero_s)
        w3 = w.reshape(x16.shape[0] // 16, 16, tq)
        return functools.reduce(lambda a, b: a + b, [w3[r] for r in range(w3.shape[0])]).astype(F32)

    def bisect16(ref, tail16, need):
        def bit_step(i, u):
            uc = u | lax.shift_left(jnp.int32(1), 15 - i)
            s16 = (uc - 32768).astype(I16)

            def cnt_step(g, acc):
                return acc + count_ge(ref[g], s16)

            acc = lax.fori_loop(0, ngrp, cnt_step, count_ge(tail16, s16))
            cnt = jnp.sum(acc, axis=0, keepdims=True)
            return jnp.where(cnt >= need, uc, u)

        return lax.fori_loop(0, 16, bit_step, jnp.zeros((1, tq), I32))

    thi = tkey >> 16
    tlo = (tkey & 0xFFFF) - 32768
    u_hi = bisect16(hi_ref, thi.astype(I16), kf)
    t_hi = u_hi - 32768
    above16 = (t_hi + 1).astype(I16)

    def above_step(g, acc):
        return acc + count_ge(hi_ref[g], above16)

    n_above = jnp.sum(lax.fori_loop(0, ngrp, above_step, count_ge(thi.astype(I16), above16)), axis=0, keepdims=True)
    n_above = jnp.where(t_hi == 32767, 0.0, n_above)
    t_hi16 = t_hi.astype(I16)

    def narrow_step(g, carry):
        lo_ref[g] = jnp.where(hi_ref[g] == t_hi16, lo_ref[g], jnp.int16(-32768))
        return carry

    lax.fori_loop(0, ngrp, narrow_step, 0)
    tlo16 = jnp.where(thi == t_hi, tlo, -32768).astype(I16)
    u_lo = bisect16(lo_ref, tlo16, kf - n_above)
    thr = t_hi * 65536 + u_lo

    def gt_step(g, acc):
        return acc + fold8(jnp.where(key_ref[g] > thr, 1.0, 0.0))

    ngt = jnp.sum(lax.fori_loop(0, ngrp, gt_step, fold8(jnp.where(tkey > thr, 1.0, 0.0))),
                  axis=0, keepdims=True)
    room = kf - ngt
    incl = jnp.where(lax.broadcasted_iota(I32, (tk, tk), 1) <= lax.broadcasted_iota(I32, (tk, tk), 0),
                     1.0, 0.0).astype(BF16)
    incl_tail = jnp.where(lax.broadcasted_iota(I32, (TAIL, TAIL), 1) <= lax.broadcasted_iota(I32, (TAIL, TAIL), 0),
                          1.0, 0.0).astype(BF16)

    def bias_of(key, seen):
        rows = key.shape[0]
        eqf = jnp.where(key == thr, 1.0, 0.0)
        rank = jnp.dot(incl if rows == tk else incl_tail, eqf.astype(BF16), preferred_element_type=F32) + seen
        tie = jnp.where(rank <= room, eqf, 0.0)
        sel = jnp.where(key > thr, 1.0, tie)
        sel = jnp.where(key == INT_MIN, 0.0, sel)
        return jnp.where(sel > 0.5, 0.0, -jnp.inf), seen + jnp.sum(fold8(eqf), axis=0, keepdims=True)

    def mask_step(g, seen):
        bias_ref[g], seen = bias_of(key_ref[g], seen)
        return seen

    seen = lax.fori_loop(0, ngrp, mask_step, jnp.zeros((1, tq), F32))
    tbias, _ = bias_of(tkey, seen)
    tbias_blk = jnp.concatenate([tbias, jnp.full((SEQ_TILE - TAIL, tq), -jnp.inf, F32)], axis=0)

    for h in range(N_HEADS):
        dq2 = dq_ref[0, :, LANES * (h // 2):LANES * (h // 2 + 1)]
        ql = jnp.dot(dq2, wuk_ref[h], preferred_element_type=F32) * (HEAD_DIM ** -0.5)
        qlat_ref[tq * h:tq * (h + 1), :] = ql.astype(BF16)
    acc_ref[...] = jnp.zeros_like(acc_ref)
    m_ref[...] = jnp.full(m_ref.shape, NEG_BIG, F32)

    def attend(ck, cx, bias):
        st_all = lax.dot_general(ck, qlat_ref[...], nt, preferred_element_type=F32)
        ps, als = [], []
        for h in range(N_HEADS):
            st = st_all[:, tq * h:tq * (h + 1)] + bias
            m_old = m_ref[h]
            mn = jnp.maximum(m_old, jnp.max(st, axis=0, keepdims=True))
            ps.append(jnp.exp(st - mn).astype(BF16))
            als.append(jnp.exp(m_old - mn))
            m_ref[h] = mn
        upd = jnp.dot(cx, jnp.concatenate(ps, axis=1), preferred_element_type=F32)
        acc_ref[...] = jnp.concatenate(als, axis=1) * acc_ref[...] + upd

    def att_step(g, carry):
        ks = pl.multiple_of(g * tk, tk)
        cx = jnp.concatenate([ct_ref[0, g * sub + j] for j in range(sub)], axis=1)
        attend(c_ref[0, pl.ds(ks, tk), :], cx, bias_ref[g])
        return carry

    lax.fori_loop(0, ngrp, att_step, 0)
    attend(c_ref[0, pl.ds(t0, SEQ_TILE), :], ct_ref[0, tblk], tbias_blk)

    for pr in range(N_HEADS // 2):
        outs = []
        for hh in range(2):
            a = acc_ref[:, tq * (2 * pr + hh):tq * (2 * pr + hh + 1)]
            olat = (a[:DSA_LATENT] / a[DSA_LATENT:DSA_LATENT + 1]).astype(BF16)
            outs.append(jnp.dot(wuvt_ref[2 * pr + hh], olat, preferred_element_type=F32))
        o_ref[0, :, LANES * pr:LANES * (pr + 1)] = jnp.concatenate(outs, axis=0).T.astype(BF16)


def _dsa(ub, c, ct, rowt, wuk_pad, wuv_t, *, topk, sub):
    b, lp, _ = ub.shape
    tq = sub * SEQ_TILE
    ngrp = lp // tq
    nblk = lp // SEQ_TILE
    return pl.pallas_call(
        functools.partial(_dsa_body, topk=topk, sub=sub),
        grid=(b, ngrp),
        in_specs=[
            pl.BlockSpec((1, tq, GROUP_WIDTH), lambda i, q: (i, q, 3)),
            pl.BlockSpec((1, tq, IDX_HEADS * IDX_DIM), lambda i, q: (i, q, 8)),
            pl.BlockSpec((1, lp, LANES), lambda i, q: (i, 0, 18)),
            pl.BlockSpec((1, lp, DSA_LATENT), lambda i, q: (i, 0, 0)),
            pl.BlockSpec((1, nblk, CT_ROWS, SEQ_TILE), lambda i, q: (i, 0, 0, 0)),
            pl.BlockSpec((1, sub, N_HEADS, SEQ_TILE), lambda i, q: (i, q, SM_W // N_HEADS, 0)),
            pl.BlockSpec((N_HEADS, LANES, DSA_LATENT), lambda i, q: (0, 0, 0)),
            pl.BlockSpec((N_HEADS, HEAD_DIM, DSA_LATENT), lambda i, q: (0, 0, 0)),
        ],
        out_specs=pl.BlockSpec((1, tq, GROUP_WIDTH), lambda i, q: (i, q, 0)),
        out_shape=jax.ShapeDtypeStruct((b, lp, GROUP_WIDTH), BF16),
        scratch_shapes=[
            pltpu.VMEM((ngrp, tq, tq), I32),
            pltpu.VMEM((ngrp, tq, tq), F32),
            pltpu.VMEM((ngrp, tq, tq), I16),
            pltpu.VMEM((ngrp, tq, tq), I16),
            pltpu.VMEM((IDX_HEADS * tq, LANES), BF16),
            pltpu.VMEM((N_HEADS * tq, DSA_LATENT), BF16),
            pltpu.VMEM((CT_ROWS, N_HEADS * tq), F32),
            pltpu.VMEM((N_HEADS, 1, tq), F32),
        ],
        compiler_params=pltpu.CompilerParams(
            dimension_semantics=("parallel", "arbitrary"), vmem_limit_bytes=_vmem(48)),
        name="dsa",
    )(ub, ub, ub, c, ct, rowt, wuk_pad, wuv_t)


def _outproj_body(ya_ref, yb_ref, yc_ref, yd_ref, x_ref, w_ref, g_ref, o_ref):
    acc = jnp.dot(ya_ref[...], w_ref[0], preferred_element_type=F32)
    acc = acc + jnp.dot(yb_ref[...], w_ref[1], preferred_element_type=F32)
    acc = acc + jnp.dot(yc_ref[...], w_ref[2], preferred_element_type=F32)
    acc = acc + jnp.dot(yd_ref[...], w_ref[3], preferred_element_type=F32)
    ms = jnp.mean(acc * acc, axis=-1, keepdims=True)
    o_ref[...] = x_ref[...] + acc * lax.rsqrt(ms + EPS) * g_ref[...]


def _out_proj(ys, x2d, w4, g, *, layer, tm):
    m, d = x2d.shape
    gw = GROUP_WIDTH
    yspec = pl.BlockSpec((tm, gw), lambda i: (i, 0))
    return pl.pallas_call(
        _outproj_body,
        grid=(m // tm,),
        in_specs=[yspec, yspec, yspec, yspec,
                  pl.BlockSpec((tm, d), lambda i: (i, 0)),
                  pl.BlockSpec((None, 4, gw, d), lambda i: (layer, 0, 0, 0)),
                  pl.BlockSpec((1, d), lambda i: (0, 0))],
        out_specs=pl.BlockSpec((tm, d), lambda i: (i, 0)),
        out_shape=jax.ShapeDtypeStruct((m, d), F32),
        compiler_params=pltpu.CompilerParams(
            dimension_semantics=("parallel",), vmem_limit_bytes=_vmem(48)),
        name="out_proj",
    )(*ys, x2d, w4, g)


def _ffn_body(x_ref, gpre_ref, wg_ref, wu_ref, cw_ref, cb_ref, wd_ref, gpost_ref, o_ref,
              xn_ref, gbuf_ref, carry_ref, *, tiles_per_seq):
    i = pl.program_id(0)
    f = pl.program_id(1)
    nf = pl.num_programs(1)
    tm = x_ref.shape[0]

    @pl.when(f == 0)
    def _():
        x = x_ref[...]
        ms = jnp.mean(x * x, axis=-1, keepdims=True)
        xn_ref[...] = (x * lax.rsqrt(ms + EPS) * gpre_ref[...]).astype(BF16)
        o_ref[...] = jnp.zeros_like(o_ref)

    @pl.when(i % tiles_per_seq == 0)
    def _():
        carry_ref[f] = jnp.zeros(carry_ref.shape[1:], F32)

    xn = xn_ref[...]
    g = jnp.dot(xn, wg_ref[...], preferred_element_type=F32)
    u = jnp.dot(xn, wu_ref[...], preferred_element_type=F32)
    gbuf_ref[0:8, :] = carry_ref[f]
    gbuf_ref[8:8 + tm, :] = g
    conv = (cb_ref[...]
            + cw_ref[0:1, :] * gbuf_ref[6:6 + tm, :]
            + cw_ref[1:2, :] * gbuf_ref[7:7 + tm, :]
            + cw_ref[2:3, :] * g)
    carry_ref[f] = g[tm - 8:tm, :]
    a = (_silu(conv) * u).astype(BF16)
    o_ref[...] += jnp.dot(a, wd_ref[...], preferred_element_type=F32)

    @pl.when(f == nf - 1)
    def _():
        y = o_ref[...]
        ms = jnp.mean(y * y, axis=-1, keepdims=True)
        o_ref[...] = x_ref[...] + y * lax.rsqrt(ms + EPS) * gpost_ref[...]


def _ffn(x2d, g_pre, w_gate, w_up, conv_w, conv_b, w_down, g_post, *, layer, tm, tf, tiles_per_seq):
    m, d = x2d.shape
    fdim = w_gate.shape[-1]
    nf = fdim // tf
    return pl.pallas_call(
        functools.partial(_ffn_body, tiles_per_seq=tiles_per_seq),
        grid=(m // tm, nf),
        in_specs=[
            pl.BlockSpec((tm, d), lambda i, f: (i, 0)),
            pl.BlockSpec((1, d), lambda i, f: (0, 0)),
            pl.BlockSpec((None, d, tf), lambda i, f: (layer, 0, f)),
            pl.BlockSpec((None, d, tf), lambda i, f: (layer, 0, f)),
            pl.BlockSpec((FFN_CONV, tf), lambda i, f: (0, f)),
            pl.BlockSpec((1, tf), lambda i, f: (0, f)),
            pl.BlockSpec((None, tf, d), lambda i, f: (layer, f, 0)),
            pl.BlockSpec((1, d), lambda i, f: (0, 0)),
        ],
        out_specs=pl.BlockSpec((tm, d), lambda i, f: (i, 0), pipeline_mode=pl.Buffered(1)),
        out_shape=jax.ShapeDtypeStruct((m, d), F32),
        scratch_shapes=[
            pltpu.VMEM((tm, d), BF16),
            pltpu.VMEM((8 + tm, tf), F32),
            pltpu.VMEM((nf, 8, tf), F32),
        ],
        compiler_params=pltpu.CompilerParams(
            dimension_semantics=("arbitrary", "arbitrary"), vmem_limit_bytes=_vmem(60)),
        name="ffn",
    )(x2d, g_pre, w_gate, w_up, conv_w, conv_b, w_down, g_post)


def _permute_w_in(w_in):
    gw = GROUP_WIDTH
    sizes = (gw, SSD_XBC, N_HEADS, gw, 3 * gw, N_HEADS, gw, DSA_LATENT, IDX_HEADS * IDX_DIM, IDX_DIM, IDX_HEADS)
    offs = [0]
    for s in sizes:
        offs.append(offs[-1] + s)
    z, xbc, dt, pool, qkv, fl, dq, dc, dqi, dki, dwi = (w_in[..., offs[k]:offs[k + 1]] for k in range(len(sizes)))
    zeros = lambda n: jnp.zeros(w_in.shape[:-1] + (n,), w_in.dtype)
    small = jnp.concatenate([dt, fl, dwi, zeros(SM_A - SM_W - IDX_HEADS), dt, zeros(LANES - SM_A - N_HEADS)], axis=-1)
    a = jnp.concatenate([xbc, z, pool, dc, small, zeros(A_COLS - 2 * gw - SSD_XBC - DSA_LATENT - LANES)], axis=-1)
    bcols = jnp.concatenate([qkv, dq, dqi, dki, dki], axis=-1)
    bcols = jnp.concatenate([bcols, zeros(B_COLS - bcols.shape[-1])], axis=-1)
    return jnp.concatenate([a, bcols], axis=-1)


def _lane_vec(pieces):
    v = jnp.zeros((LANES,), F32)
    for off, val in pieces:
        v = v.at[off:off + val.shape[0]].set(val.astype(F32))
    return v[None, :]


def _pad_head_weights(w_uk, w_uv):
    h, r, d = w_uk.shape
    uk = jnp.zeros((h, 2 * d, r), F32)
    for i in range(h):
        o = d * (i % 2)
        uk = uk.at[i, o:o + d, :].set(w_uk[i].T)
    return uk.astype(BF16), jnp.swapaxes(w_uv, 1, 2).astype(BF16)


def _tile_sizes(b, lp):
    m = b * lp
    tm_proj = next(t for t in (1024, 512, 256, 128) if m % t == 0)
    tm_out = next(t for t in (512, 256, 128) if m % t == 0)
    tm_ffn = next(t for t in (1056, 528, 384, 320, 256, 128) if lp % t == 0)
    att_sub = 3 if lp % (3 * SEQ_TILE) == 0 else 1
    return tm_proj, tm_out, tm_ffn, att_sub


def _layer(h, p, big, layer, *, topk):
    b, lp, d = h.shape
    m = b * lp
    tm_proj, tm_out, tm_ffn, att_sub = _tile_sizes(b, lp)
    row = lambda v: v.astype(F32)[None, :]

    ua, ub = _in_proj(h.reshape(m, d), row(p["norm_mix_pre"]), big["w_in"], layer=layer, tm=tm_proj, tn=1280)
    ua = ua.reshape(b, lp, A_COLS)
    ub = ub.reshape(b, lp, B_COLS)

    bias_vec = _lane_vec([(SM_DT, p["ssd_dt_bias"]), (SM_F, p["fox_f_bias"]), (SM_A, p["ssd_dt_bias"])])
    wscale = jnp.full((IDX_HEADS,), (IDX_HEADS ** -0.5) * (IDX_DIM ** -0.5), F32)
    mul_vec = _lane_vec([(SM_W, wscale), (SM_A, -jnp.exp(p["ssd_a_log"].astype(F32)))])
    col, rowt, c, ct, fkp = _prep(ua, bias_vec, mul_vec, row(p["dsa_kv_norm"]))

    dskip_full = jnp.repeat(p["ssd_d"].astype(F32), HEAD_DIM)[None, :]
    y_a = _ssd(ua, col, rowt, p["ssd_conv_w"].astype(F32), row(p["ssd_conv_b"]), dskip_full, row(p["ssd_norm"]))
    y_b = _pool(ua, p["pool_w"].astype(BF16), row(p["pool_scale"]), sub=att_sub)
    y_c = _fox(ub, fkp, sub=att_sub)
    wuk_pad, wuv_t = _pad_head_weights(p["dsa_w_uk"], p["dsa_w_uv"])
    y_d = _dsa(ub, c, ct, rowt, wuk_pad, wuv_t, topk=topk, sub=att_sub)

    ys = [y.reshape(m, GROUP_WIDTH) for y in (y_a, y_b, y_c, y_d)]
    x1 = _out_proj(ys, h.reshape(m, d), big["w_out"], row(p["norm_mix_post"]), layer=layer, tm=tm_out)
    x2 = _ffn(x1, row(p["norm_ffn_pre"]), big["ffn_w_gate"], big["ffn_w_up"],
              p["ffn_conv_w"].astype(F32), row(p["ffn_conv_b"]), big["ffn_w_down"],
              row(p["norm_ffn_post"]), layer=layer, tm=tm_ffn, tf=512, tiles_per_seq=lp // tm_ffn)
    return x2.reshape(b, lp, d)


def _stack_big_weights(w_in, w_out, ffn_w_gate, ffn_w_up, ffn_w_down):
    depth, d, _ = w_out.shape
    return dict(w_in=_to_bf16(_permute_w_in(w_in)),
                w_out=_to_bf16(w_out).reshape(depth, 4, GROUP_WIDTH, d),
                ffn_w_gate=_to_bf16(ffn_w_gate), ffn_w_up=_to_bf16(ffn_w_up), ffn_w_down=_to_bf16(ffn_w_down))


def kernel(x, meta_tokens, norm_mix_pre, norm_mix_post, norm_ffn_pre, norm_ffn_post, w_in, ssd_conv_w, ssd_conv_b, ssd_dt_bias, ssd_a_log, ssd_d, ssd_norm, pool_w, pool_scale, fox_f_bias, dsa_kv_norm, dsa_w_uk, dsa_w_uv, w_out, ffn_w_gate, ffn_w_up, ffn_conv_w, ffn_conv_b, ffn_w_down):
    bsz, seq, d = x.shape
    n = N_META + seq
    lp = -(-n // SEQ_TILE) * SEQ_TILE
    topk = min(DSA_TOPK_MAX, seq // 4)
    meta = jnp.broadcast_to(meta_tokens.astype(x.dtype)[None], (bsz, N_META, d))
    h = jnp.concatenate([meta, x, jnp.zeros((bsz, lp - n, d), x.dtype)], axis=1)
    small = dict(norm_mix_pre=norm_mix_pre, norm_mix_post=norm_mix_post, norm_ffn_pre=norm_ffn_pre,
                 norm_ffn_post=norm_ffn_post, ssd_conv_w=ssd_conv_w, ssd_conv_b=ssd_conv_b,
                 ssd_dt_bias=ssd_dt_bias, ssd_a_log=ssd_a_log, ssd_d=ssd_d, ssd_norm=ssd_norm,
                 pool_w=pool_w, pool_scale=pool_scale, fox_f_bias=fox_f_bias, dsa_kv_norm=dsa_kv_norm,
                 dsa_w_uk=dsa_w_uk, dsa_w_uv=dsa_w_uv, ffn_conv_w=ffn_conv_w, ffn_conv_b=ffn_conv_b)
    big = _stack_big_weights(w_in, w_out, ffn_w_gate, ffn_w_up, ffn_w_down)
    for i in range(norm_mix_pre.shape[0]):
        h = _layer(h, {k: v[i] for k, v in small.items()}, big, i, topk=topk)
    return h[:, N_META:n]
```

```python
import functools

import jax
import jax.numpy as jnp
from jax import lax
from jax.experimental import pallas as pl
from jax.experimental.pallas import tpu as pltpu

F32 = jnp.float32
BF16 = jnp.bfloat16
I32 = jnp.int32
I16 = jnp.int16

EPS = 1e-6
N_META = 16
CHUNK = 64
HEAD_DIM = 64
GROUP_WIDTH = 512
N_HEADS = 8
SSD_GROUPS = 2
SSD_STATE = 128
SSD_CONV = 4
SSD_XBC = GROUP_WIDTH + 2 * SSD_GROUPS * SSD_STATE
POOL_WINDOWS = (2, 4, 8, 16)
DSA_LATENT = 128
IDX_HEADS = 4
IDX_DIM = 64
DSA_TOPK_MAX = 256
FFN_CONV = 3

LANES = 128
SEQ_TILE = 128
INT_MIN = -(2 ** 31)
NEG_BIG = -1e30
LOG2E = 1.4426950408889634

A_COLS = 2560
B_COLS = 2560
SM_DT = 0
SM_F = 8
SM_W = 16
SM_A = 24
CT_ROWS = DSA_LATENT + 16
ACC_ROWS = 2 * HEAD_DIM + 16
TAIL = N_META


def _vmem(mb):
    return int(mb * 1024 * 1024)


def _softplus_parts(x):
    t = jnp.log1p(jnp.exp(-jnp.abs(x)))
    return jnp.maximum(x, 0.0) + t, jnp.minimum(x, 0.0) - t


def _silu(x):
    return x / (1.0 + jnp.exp(-x))


def _cast_body(x_ref, o_ref):
    o_ref[...] = x_ref[...].astype(BF16)


def _to_bf16(w, *, tr=256):
    shape = w.shape
    w2 = w.reshape(-1, shape[-1])
    r, c = w2.shape
    out = pl.pallas_call(
        _cast_body,
        grid=(r // tr,),
        in_specs=[pl.BlockSpec((tr, c), lambda i: (i, 0))],
        out_specs=pl.BlockSpec((tr, c), lambda i: (i, 0)),
        out_shape=jax.ShapeDtypeStruct((r, c), BF16),
        compiler_params=pltpu.CompilerParams(dimension_semantics=("parallel",), vmem_limit_bytes=_vmem(40)),
        name="to_bf16",
    )(w2)
    return out.reshape(shape)


def _inproj_body(x_ref, g_ref, w_ref, oa_ref, ob_ref, xn_ref, *, n_a):
    j = pl.program_id(1)

    @pl.when(j == 0)
    def _():
        x = x_ref[...]
        ms = jnp.mean(x * x, axis=-1, keepdims=True)
        xn_ref[...] = (x * lax.rsqrt(ms + EPS) * g_ref[...]).astype(BF16)

    acc = jnp.dot(xn_ref[...], w_ref[...], preferred_element_type=F32)

    @pl.when(j < n_a)
    def _():
        oa_ref[...] = acc

    @pl.when(j >= n_a)
    def _():
        ob_ref[...] = acc.astype(BF16)


def _in_proj(x2d, g, w_perm, *, layer, tm, tn):
    m, d = x2d.shape
    n_a, n_b = A_COLS // tn, B_COLS // tn
    return pl.pallas_call(
        functools.partial(_inproj_body, n_a=n_a),
        grid=(m // tm, n_a + n_b),
        in_specs=[
            pl.BlockSpec((tm, d), lambda i, j: (i, 0)),
            pl.BlockSpec((1, d), lambda i, j: (0, 0)),
            pl.BlockSpec((None, d, tn), lambda i, j: (layer, 0, j)),
        ],
        out_specs=[
            pl.BlockSpec((tm, tn), lambda i, j: (i, jnp.minimum(j, n_a - 1))),
            pl.BlockSpec((tm, tn), lambda i, j: (i, jnp.maximum(j - n_a, 0))),
        ],
        out_shape=[
            jax.ShapeDtypeStruct((m, A_COLS), F32),
            jax.ShapeDtypeStruct((m, B_COLS), BF16),
        ],
        scratch_shapes=[pltpu.VMEM((tm, d), BF16)],
        compiler_params=pltpu.CompilerParams(
            dimension_semantics=("parallel", "arbitrary"), vmem_limit_bytes=_vmem(58)),
        name="in_proj",
    )(x2d, g, w_perm)


def _prep_body(sm_ref, dc_ref, bias_ref, mul_ref, kvg_ref, col_ref, row_ref, c_ref, ct_ref, fkp_ref,
               carry_ref):
    t = pl.program_id(1)

    @pl.when(t == 0)
    def _():
        carry_ref[...] = jnp.zeros_like(carry_ref)

    tt = sm_ref.shape[1]
    s = sm_ref[0]
    lane = lax.broadcasted_iota(I32, (tt, LANES), 1)
    is_dt = lane < SM_F
    is_f = (lane >= SM_F) & (lane < SM_W)
    is_a = (lane >= SM_A) & (lane < SM_A + N_HEADS)
    sp, ls = _softplus_parts(s + bias_ref[...])
    v = jnp.where(is_dt, sp, jnp.where(is_f, ls, jnp.where(is_a, sp, s) * mul_ref[...]))
    ri = lax.broadcasted_iota(I32, (tt, tt), 0)
    ci = lax.broadcasted_iota(I32, (tt, tt), 1)
    tril = jnp.where(ci <= ri, 1.0, 0.0).astype(F32)
    local = jnp.dot(tril, v, precision=lax.Precision.HIGHEST, preferred_element_type=F32)
    out = jnp.where(is_f, local + carry_ref[...], jnp.where(is_a, local, v))
    col_ref[0] = out
    row_ref[0, 0] = out.T
    carry_ref[...] = jnp.where(is_f[0:1], out[tt - 1:tt, :], 0.0)

    f0 = jnp.where(is_f, out, 0.0) * LOG2E
    hi = f0.astype(BF16).astype(F32)
    r1 = f0 - hi
    mid = r1.astype(BF16).astype(F32)
    lo = (r1 - mid).astype(BF16).astype(F32)
    fkp_ref[0] = (hi + pltpu.roll(mid, N_HEADS, axis=1) + pltpu.roll(lo, 2 * N_HEADS, axis=1)).astype(BF16)

    dc = dc_ref[0]
    ms = jnp.mean(dc * dc, axis=-1, keepdims=True)
    cn = dc * lax.rsqrt(ms + EPS) * kvg_ref[...]
    c_ref[0] = cn.astype(BF16)
    ct_ref[0, 0] = jnp.concatenate([cn.T, jnp.ones((CT_ROWS - DSA_LATENT, tt), F32)], axis=0).astype(BF16)


def _prep(ua, bias_vec, mul_vec, kv_g):
    b, lp, _ = ua.shape
    tt = SEQ_TILE
    nblk = lp // tt
    return pl.pallas_call(
        _prep_body,
        grid=(b, nblk),
        in_specs=[
            pl.BlockSpec((1, tt, LANES), lambda i, t: (i, t, 17)),
            pl.BlockSpec((1, tt, LANES), lambda i, t: (i, t, 16)),
            pl.BlockSpec((1, LANES), lambda i, t: (0, 0)),
            pl.BlockSpec((1, LANES), lambda i, t: (0, 0)),
            pl.BlockSpec((1, LANES), lambda i, t: (0, 0)),
        ],
        out_specs=[
            pl.BlockSpec((1, tt, LANES), lambda i, t: (i, t, 0)),
            pl.BlockSpec((1, 1, LANES, tt), lambda i, t: (i, t, 0, 0)),
            pl.BlockSpec((1, tt, LANES), lambda i, t: (i, t, 0)),
            pl.BlockSpec((1, 1, CT_ROWS, tt), lambda i, t: (i, t, 0, 0)),
            pl.BlockSpec((1, tt, LANES), lambda i, t: (i, t, 0)),
        ],
        out_shape=[
            jax.ShapeDtypeStruct((b, lp, LANES), F32),
            jax.ShapeDtypeStruct((b, nblk, LANES, tt), F32),
            jax.ShapeDtypeStruct((b, lp, DSA_LATENT), BF16),
            jax.ShapeDtypeStruct((b, nblk, CT_ROWS, tt), BF16),
            jax.ShapeDtypeStruct((b, lp, LANES), BF16),
        ],
        scratch_shapes=[pltpu.VMEM((1, LANES), F32)],
        compiler_params=pltpu.CompilerParams(dimension_semantics=("parallel", "arbitrary")),
        name="prep",
    )(ua, ua, bias_vec, mul_vec, kv_g)


def _expand_heads(colv, base, lo_half):
    parts = []
    for p in range(N_HEADS // 2):
        a = colv[:, base + 2 * p:base + 2 * p + 1]
        b = colv[:, base + 2 * p + 1:base + 2 * p + 2]
        parts.append(jnp.where(lo_half, a, b))
    return jnp.concatenate(parts, axis=1)


def _ssd_body(xbc_ref, z_ref, col_ref, row_ref, cw_ref, cb_ref, dsk_ref, ng_ref, y_ref,
              xpad_ref, st_ref):
    c = pl.program_id(1)
    ll = xbc_ref.shape[1]
    gw = GROUP_WIDTH
    ns = SSD_STATE
    hpg = N_HEADS // SSD_GROUPS
    gcols = hpg * HEAD_DIM

    @pl.when(c == 0)
    def _():
        xpad_ref[0:8, :] = jnp.zeros((8, SSD_XBC), F32)
        st_ref[...] = jnp.zeros_like(st_ref)

    x = xbc_ref[0]
    xpad_ref[8:8 + ll, :] = x
    conv = (cb_ref[...]
            + cw_ref[0:1, :] * xpad_ref[5:5 + ll, :]
            + cw_ref[1:2, :] * xpad_ref[6:6 + ll, :]
            + cw_ref[2:3, :] * xpad_ref[7:7 + ll, :]
            + cw_ref[3:4, :] * x)
    xpad_ref[0:8, :] = x[ll - 8:ll, :]
    act = _silu(conv)
    xs = act[:, 0:gw]
    bm = act[:, gw:gw + SSD_GROUPS * ns]
    cm = act[:, gw + SSD_GROUPS * ns:]

    colv = col_ref[0]
    rowv = row_ref[0, 0]
    lo_half = lax.broadcasted_iota(I32, (ll, LANES), 1) < HEAD_DIM
    dt_full = _expand_heads(colv, SM_DT, lo_half)
    acs_full = _expand_heads(colv, SM_A, lo_half)
    acs_last = acs_full[ll - 1:ll, :]
    dte_full = jnp.exp(acs_last - acs_full)
    dfs_full = jnp.exp(acs_full)
    xdt = xs * dt_full
    xdt_b = xdt.astype(BF16)
    xdte_b = (xdt * dte_full).astype(BF16)
    cm_b = cm.astype(BF16)
    ri = lax.broadcasted_iota(I32, (ll, ll), 0)
    ci = lax.broadcasted_iota(I32, (ll, ll), 1)
    tril = ci <= ri

    ys = []
    for g in range(SSD_GROUPS):
        bg = bm[:, ns * g:ns * (g + 1)]
        bg_b = bg.astype(BF16)
        bgt_b = bg.T.astype(BF16)
        cg_b = cm_b[:, ns * g:ns * (g + 1)]
        cb = lax.dot_general(cg_b, bg_b, (((1,), (1,)), ((), ())), preferred_element_type=F32)
        sg = st_ref[g]
        yoff = (jnp.dot(cg_b, sg.astype(BF16), preferred_element_type=F32)
                * dfs_full[:, gcols * g:gcols * (g + 1)])
        parts = []
        for pr in range(hpg // 2):
            xpair = xdt_b[:, gcols * g + LANES * pr:gcols * g + LANES * (pr + 1)]
            res = []
            for hh in range(2):
                h = hpg * g + 2 * pr + hh
                seg = colv[:, SM_A + h:SM_A + h + 1] - rowv[SM_A + h:SM_A + h + 1, :]
                lm = jnp.exp(jnp.where(tril, seg, -jnp.inf))
                res.append(jnp.dot((cb * lm).astype(BF16), xpair, preferred_element_type=F32))
            parts.append(jnp.where(lo_half, res[0], res[1]))
        ydiag = jnp.concatenate(parts, axis=1)
        decay = jnp.exp(acs_last[:, gcols * g:gcols * (g + 1)])
        st_ref[g] = decay * sg + jnp.dot(bgt_b, xdte_b[:, gcols * g:gcols * (g + 1)],
                                         preferred_element_type=F32)
        ys.append(ydiag + yoff)

    y = jnp.concatenate(ys, axis=1) + dsk_ref[...] * xs
    gz = y * _silu(z_ref[0])
    outs = []
    for g in range(SSD_GROUPS):
        gg = gz[:, gcols * g:gcols * (g + 1)]
        outs.append(gg * lax.rsqrt(jnp.mean(gg * gg, axis=-1, keepdims=True) + EPS))
    y_ref[0] = (jnp.concatenate(outs, axis=1) * ng_ref[...]).astype(BF16)


def _ssd(ua, col, row, conv_w, conv_b, dskip_full, norm_g):
    b, lp, _ = ua.shape
    ll = SEQ_TILE
    nblk = lp // ll
    hpg = N_HEADS // SSD_GROUPS
    return pl.pallas_call(
        _ssd_body,
        grid=(b, nblk),
        in_specs=[
            pl.BlockSpec((1, ll, SSD_XBC), lambda i, c: (i, c, 0)),
            pl.BlockSpec((1, ll, GROUP_WIDTH), lambda i, c: (i, c, 2)),
            pl.BlockSpec((1, ll, LANES), lambda i, c: (i, c, 0)),
            pl.BlockSpec((1, 1, LANES, ll), lambda i, c: (i, c, 0, 0)),
            pl.BlockSpec((SSD_CONV, SSD_XBC), lambda i, c: (0, 0)),
            pl.BlockSpec((1, SSD_XBC), lambda i, c: (0, 0)),
            pl.BlockSpec((1, GROUP_WIDTH), lambda i, c: (0, 0)),
            pl.BlockSpec((1, GROUP_WIDTH), lambda i, c: (0, 0)),
        ],
        out_specs=pl.BlockSpec((1, ll, GROUP_WIDTH), lambda i, c: (i, c, 0)),
        out_shape=jax.ShapeDtypeStruct((b, lp, GROUP_WIDTH), BF16),
        scratch_shapes=[
            pltpu.VMEM((8 + ll, SSD_XBC), F32),
            pltpu.VMEM((SSD_GROUPS, SSD_STATE, hpg * HEAD_DIM), F32),
        ],
        compiler_params=pltpu.CompilerParams(dimension_semantics=("parallel", "arbitrary")),
        name="ssd",
    )(ua, ua, col, row, conv_w, conv_b, dskip_full, norm_g)


def _pool_body(u_ref, w_ref, sc_ref, y_ref, buf_ref):
    t = pl.program_id(1)
    tt = u_ref.shape[1]
    hist = max(POOL_WINDOWS)
    gd = GROUP_WIDTH // len(POOL_WINDOWS)

    @pl.when(t == 0)
    def _():
        buf_ref[0:hist, :] = jnp.zeros((hist, GROUP_WIDTH), F32)

    u = u_ref[0]
    buf_ref[hist:hist + tt, :] = u
    count = (t * tt + 1 + lax.broadcasted_iota(I32, (tt, 1), 0)).astype(F32)
    outs = []
    for gi, win in enumerate(POOL_WINDOWS):
        ug = u[:, gd * gi:gd * (gi + 1)]
        acc = ug
        for k in range(1, win):
            acc = acc + buf_ref[hist - k:hist - k + tt, gd * gi:gd * (gi + 1)]
        pooled = acc / jnp.minimum(count, float(win)) - ug
        outs.append(jnp.dot(pooled.astype(BF16), w_ref[gi], preferred_element_type=F32))
    y_ref[0] = (jnp.concatenate(outs, axis=1) * sc_ref[...]).astype(BF16)
    buf_ref[0:hist, :] = u[tt - hist:tt, :]


def _pool(ua, w, scale, *, sub):
    b, lp, _ = ua.shape
    tt = sub * SEQ_TILE
    ng = len(POOL_WINDOWS)
    gd = GROUP_WIDTH // ng
    return pl.pallas_call(
        _pool_body,
        grid=(b, lp // tt),
        in_specs=[
            pl.BlockSpec((1, tt, GROUP_WIDTH), lambda i, t: (i, t, 3)),
            pl.BlockSpec((ng, gd, gd), lambda i, t: (0, 0, 0)),
            pl.BlockSpec((1, GROUP_WIDTH), lambda i, t: (0, 0)),
        ],
        out_specs=pl.BlockSpec((1, tt, GROUP_WIDTH), lambda i, t: (i, t, 0)),
        out_shape=jax.ShapeDtypeStruct((b, lp, GROUP_WIDTH), BF16),
        scratch_shapes=[pltpu.VMEM((max(POOL_WINDOWS) + tt, GROUP_WIDTH), F32)],
        compiler_params=pltpu.CompilerParams(dimension_semantics=("parallel", "arbitrary")),
        name="pool",
    )(ua, w, scale)


def _fox_body(q_ref, k_ref, v_ref, fkp_ref, o_ref, qa_ref, vl_ref, acc_ref, m_ref):
    qi = pl.program_id(1)
    tq = q_ref.shape[1]
    tk = tq
    nt = (((1,), (1,)), ((), ()))
    npair = N_HEADS // 2
    lane = lax.broadcasted_iota(I32, (tq, LANES), 1)
    lo_half = lane < HEAD_DIM
    zero_b = jnp.zeros((tq, LANES), BF16)

    @pl.when(qi == 0)
    def _():
        nkt = vl_ref.shape[0]
        rsel = lax.broadcasted_iota(I32, (ACC_ROWS - 2 * HEAD_DIM, 2 * tk), 0)
        csel = lax.broadcasted_iota(I32, (ACC_ROWS - 2 * HEAD_DIM, 2 * tk), 1)
        ones_rows = jnp.where((rsel == 0) & (csel < tk) | (rsel == 1) & (csel >= tk), 1.0, 0.0).astype(BF16)
        zpad = jnp.zeros((HEAD_DIM, tk), BF16)

        def fill(kt, carry):
            ks = pl.multiple_of(kt * tk, tk)
            for pr in range(npair):
                vt = v_ref[0, pl.ds(ks, tk), LANES * pr:LANES * (pr + 1)].astype(F32).T.astype(BF16)
                top = jnp.concatenate([vt[:HEAD_DIM], zpad], axis=1)
                bot = jnp.concatenate([zpad, vt[HEAD_DIM:]], axis=1)
                vl_ref[kt, pr] = jnp.concatenate([top, bot, ones_rows], axis=0)
            return carry

        lax.fori_loop(0, nkt, fill, 0)

    for pr in range(npair):
        q2 = q_ref[0, :, LANES * pr:LANES * (pr + 1)]
        for hh in range(2):
            h = 2 * pr + hh
            qm = jnp.where(lo_half, q2, zero_b) if hh == 0 else jnp.where(lo_half, zero_b, q2)
            pick = (lane == SM_F + h) | (lane == SM_F + N_HEADS + h) | (lane == SM_F + 2 * N_HEADS + h)
            qa_ref[h] = jnp.concatenate([qm, jnp.where(pick, -1.0, 0.0).astype(BF16)], axis=1)
    acc_ref[...] = jnp.zeros_like(acc_ref)
    m_ref[...] = jnp.full(m_ref.shape, NEG_BIG, F32)
    causal_t = (lax.broadcasted_iota(I32, (tk, tq), 0) <= lax.broadcasted_iota(I32, (tk, tq), 1))
    tail_row = lax.broadcasted_iota(I32, (ACC_ROWS - 2 * HEAD_DIM, tq), 0)

    def step(kt, masked):
        ks = pl.multiple_of(kt * tk, tk)
        fkp = fkp_ref[0, pl.ds(ks, tk), :]
        for pr in range(npair):
            ka = jnp.concatenate([k_ref[0, pl.ds(ks, tk), LANES * pr:LANES * (pr + 1)], fkp], axis=1)
            ps, als = [], []
            for hh in range(2):
                h = 2 * pr + hh
                st = lax.dot_general(ka, qa_ref[h], nt, preferred_element_type=F32)
                if masked:
                    st = jnp.where(causal_t, st, -jnp.inf)
                m_old = m_ref[h]
                mn = jnp.maximum(m_old, jnp.max(st, axis=0, keepdims=True))
                ps.append(jnp.exp2(st - mn).astype(BF16))
                als.append(jnp.exp2(m_old - mn))
                m_ref[h] = mn
            upd = jnp.dot(vl_ref[kt, pr], jnp.concatenate(ps, axis=0), preferred_element_type=F32)
            scale = jnp.concatenate([jnp.broadcast_to(als[0], (HEAD_DIM, tq)),
                                     jnp.broadcast_to(als[1], (HEAD_DIM, tq)),
                                     jnp.where(tail_row == 0, als[0], als[1])], axis=0)
            acc_ref[pr] = scale * acc_ref[pr] + upd

    def body(kt, carry):
        step(kt, False)
        return carry

    lax.fori_loop(0, qi, body, 0)
    step(qi, True)
    for pr in range(npair):
        a = acc_ref[pr]
        o = jnp.concatenate([a[:HEAD_DIM] / a[2 * HEAD_DIM:2 * HEAD_DIM + 1],
                             a[HEAD_DIM:2 * HEAD_DIM] / a[2 * HEAD_DIM + 1:2 * HEAD_DIM + 2]], axis=0)
        o_ref[0, :, LANES * pr:LANES * (pr + 1)] = o.T.astype(BF16)


def _fox(ub, fkp, *, sub):
    b, lp, _ = ub.shape
    tq = sub * SEQ_TILE
    return pl.pallas_call(
        _fox_body,
        grid=(b, lp // tq),
        in_specs=[
            pl.BlockSpec((1, tq, GROUP_WIDTH), lambda i, q: (i, q, 0)),
            pl.BlockSpec((1, lp, GROUP_WIDTH), lambda i, q: (i, 0, 1)),
            pl.BlockSpec((1, lp, GROUP_WIDTH), lambda i, q: (i, 0, 2)),
            pl.BlockSpec((1, lp, LANES), lambda i, q: (i, 0, 0)),
        ],
        out_specs=pl.BlockSpec((1, tq, GROUP_WIDTH), lambda i, q: (i, q, 0)),
        out_shape=jax.ShapeDtypeStruct((b, lp, GROUP_WIDTH), BF16),
        scratch_shapes=[
            pltpu.VMEM((N_HEADS, tq, 2 * LANES), BF16),
            pltpu.VMEM((lp // tq, N_HEADS // 2, ACC_ROWS, 2 * tq), BF16),
            pltpu.VMEM((N_HEADS // 2, ACC_ROWS, tq), F32),
            pltpu.VMEM((N_HEADS, 1, tq), F32),
        ],
        compiler_params=pltpu.CompilerParams(
            dimension_semantics=("parallel", "arbitrary"), vmem_limit_bytes=_vmem(48)),
        name="fox",
    )(ub, ub, ub, fkp)


def _dsa_body(dq_ref, dqi_ref, ki_ref, c_ref, ct_ref, wr_ref, wuk_ref, wuvt_ref, o_ref,
              key_ref, bias_ref, hi_ref, lo_ref, qm_ref, qlat_ref, acc_ref, m_ref, *, topk, sub):
    qi = pl.program_id(1)
    tq = dq_ref.shape[1]
    tk = tq
    ngrp = qi + 1
    has_tail = qi + 1 < key_ref.shape[0]
    tblk = jnp.minimum(sub * (qi + 1), ct_ref.shape[1] - 1)
    t0 = pl.multiple_of(tblk * SEQ_TILE, SEQ_TILE)
    nt = (((1,), (1,)), ((), ()))
    lo_half = lax.broadcasted_iota(I32, (tq, LANES), 1) < IDX_DIM
    zero_b = jnp.zeros((tq, LANES), BF16)
    shift = CHUNK - N_META
    lg2 = CHUNK.bit_length() - 1
    kf = float(topk)

    def fold8(w):
        return jnp.sum(w.reshape(w.shape[0] // 8, 8, tq), axis=0)

    for h in range(IDX_HEADS):
        q2 = dqi_ref[0, :, LANES * (h // 2):LANES * (h // 2 + 1)]
        qm_ref[tq * h:tq * (h + 1), :] = (jnp.where(lo_half, q2, zero_b) if h % 2 == 0
                                           else jnp.where(lo_half, zero_b, q2))
    wrows = jnp.concatenate([wr_ref[0, j] for j in range(sub)], axis=1)

    def keys_of(kt):
        lg_all = lax.dot_general(kt, qm_ref[...], nt, preferred_element_type=F32)
        sc = None
        for h in range(IDX_HEADS):
            term = wrows[h:h + 1, :] * jnp.maximum(lg_all[:, tq * h:tq * (h + 1)], 0.0)
            sc = term if sc is None else sc + term
        bits = pltpu.bitcast(sc, I32)
        bits = jnp.where(bits == INT_MIN, 0, bits)
        return bits ^ ((bits >> 31) & 0x7FFFFFFF)

    def admissible(ks, rows):
        kcid = (ks + lax.broadcasted_iota(I32, (rows, 1), 0) + shift) >> lg2
        qcid = (qi * tq + lax.broadcasted_iota(I32, (1, tq), 1) + shift) >> lg2
        return kcid <= qcid

    def store_keys(g, key):
        key_ref[g] = key
        hi_ref[g] = (key >> 16).astype(I16)
        lo_ref[g] = ((key & 0xFFFF) - 32768).astype(I16)

    def score_step(g, carry):
        ks = pl.multiple_of(g * tk, tk)
        store_keys(g, keys_of(ki_ref[0, pl.ds(ks, tk), :]))
        return carry

    lax.fori_loop(0, qi, score_step, 0)
    q0 = pl.multiple_of(qi * tk, tk)
    store_keys(qi, jnp.where(admissible(q0, tk), keys_of(ki_ref[0, pl.ds(q0, tk), :]), INT_MIN))
    tkey = jnp.where(admissible(t0, TAIL) & has_tail, keys_of(ki_ref[0, pl.ds(t0, TAIL), :]), INT_MIN)

    one_b = jnp.ones((), BF16)
    zero_s = jnp.zeros((), BF16)

    def count_ge(x16, s16):
        w = jnp.where(x16 >= s16, one_b, zero_s)
        w3 = w.reshape(x16.shape[0] // 16, 16, tq)
        return functools.reduce(lambda a, b: a + b, [w3[r] for r in range(w3.shape[0])]).astype(F32)

    def bisect16(ref, tail16, need):
        def bit_step(i, u):
            uc = u | lax.shift_left(jnp.int32(1), 15 - i)
            s16 = (uc - 32768).astype(I16)

            def cnt_step(g, acc):
                return acc + count_ge(ref[g], s16)

            acc = lax.fori_loop(0, ngrp, cnt_step, count_ge(tail16, s16))
            cnt = jnp.sum(acc, axis=0, keepdims=True)
            return jnp.where(cnt >= need, uc, u)

        return lax.fori_loop(0, 16, bit_step, jnp.zeros((1, tq), I32))

    thi = tkey >> 16
    tlo = (tkey & 0xFFFF) - 32768
    u_hi = bisect16(hi_ref, thi.astype(I16), kf)
    t_hi = u_hi - 32768
    above16 = (t_hi + 1).astype(I16)

    def above_step(g, acc):
        return acc + count_ge(hi_ref[g], above16)

    n_above = jnp.sum(lax.fori_loop(0, ngrp, above_step, count_ge(thi.astype(I16), above16)), axis=0, keepdims=True)
    n_above = jnp.where(t_hi == 32767, 0.0, n_above)
    t_hi16 = t_hi.astype(I16)

    def narrow_step(g, carry):
        lo_ref[g] = jnp.where(hi_ref[g] == t_hi16, lo_ref[g], jnp.int16(-32768))
        return carry

    lax.fori_loop(0, ngrp, narrow_step, 0)
    tlo16 = jnp.where(thi == t_hi, tlo, -32768).astype(I16)
    u_lo = bisect16(lo_ref, tlo16, kf - n_above)
    thr = t_hi * 65536 + u_lo

    def gt_step(g, acc):
        return acc + fold8(jnp.where(key_ref[g] > thr, 1.0, 0.0))

    ngt = jnp.sum(lax.fori_loop(0, ngrp, gt_step, fold8(jnp.where(tkey > thr, 1.0, 0.0))),
                  axis=0, keepdims=True)
    room = kf - ngt
    incl = jnp.where(lax.broadcasted_iota(I32, (tk, tk), 1) <= lax.broadcasted_iota(I32, (tk, tk), 0),
                     1.0, 0.0).astype(BF16)
    incl_tail = jnp.where(lax.broadcasted_iota(I32, (TAIL, TAIL), 1) <= lax.broadcasted_iota(I32, (TAIL, TAIL), 0),
                          1.0, 0.0).astype(BF16)

    room = jnp.where(thr == INT_MIN, 0.0, room)

    def bias_of(key, seen):
        rows = key.shape[0]
        eq = key == thr
        eqf = jnp.where(eq, 1.0, 0.0)
        rank = jnp.dot(incl if rows == tk else incl_tail, eqf.astype(BF16), preferred_element_type=F32) + seen
        tie = jnp.where(eq, jnp.where(rank <= room, 0.0, -jnp.inf), -jnp.inf)
        return jnp.where(key > thr, 0.0, tie), seen + jnp.sum(fold8(eqf), axis=0, keepdims=True)

    def mask_step(g, seen):
        bias_ref[g], seen = bias_of(key_ref[g], seen)
        return seen

    seen = lax.fori_loop(0, ngrp, mask_step, jnp.zeros((1, tq), F32))
    tbias, _ = bias_of(tkey, seen)
    tbias_blk = jnp.concatenate([tbias, jnp.full((SEQ_TILE - TAIL, tq), -jnp.inf, F32)], axis=0)

    for h in range(N_HEADS):
        dq2 = dq_ref[0, :, LANES * (h // 2):LANES * (h // 2 + 1)]
        ql = jnp.dot(dq2, wuk_ref[h], preferred_element_type=F32) * (LOG2E * HEAD_DIM ** -0.5)
        qlat_ref[tq * h:tq * (h + 1), :] = ql.astype(BF16)
    acc_ref[...] = jnp.zeros_like(acc_ref)
    m_ref[...] = jnp.full(m_ref.shape, NEG_BIG, F32)

    def attend(ck, cx, bias):
        st_all = lax.dot_general(ck, qlat_ref[...], nt, preferred_element_type=F32)
        ps, als = [], []
        for h in range(N_HEADS):
            st = st_all[:, tq * h:tq * (h + 1)] + bias
            m_old = m_ref[h]
            mn = jnp.maximum(m_old, jnp.max(st, axis=0, keepdims=True))
            ps.append(jnp.exp2(st - mn).astype(BF16))
            als.append(jnp.exp2(m_old - mn))
            m_ref[h] = mn
        upd = jnp.dot(cx, jnp.concatenate(ps, axis=1), preferred_element_type=F32)
        acc_ref[...] = jnp.concatenate(als, axis=1) * acc_ref[...] + upd

    def att_step(g, carry):
        ks = pl.multiple_of(g * tk, tk)
        cx = jnp.concatenate([ct_ref[0, g * sub + j] for j in range(sub)], axis=1)
        attend(c_ref[0, pl.ds(ks, tk), :], cx, bias_ref[g])
        return carry

    lax.fori_loop(0, ngrp, att_step, 0)
    attend(c_ref[0, pl.ds(t0, SEQ_TILE), :], ct_ref[0, tblk], tbias_blk)

    for pr in range(N_HEADS // 2):
        outs = []
        for hh in range(2):
            a = acc_ref[:, tq * (2 * pr + hh):tq * (2 * pr + hh + 1)]
            olat = (a[:DSA_LATENT] / a[DSA_LATENT:DSA_LATENT + 1]).astype(BF16)
            outs.append(jnp.dot(wuvt_ref[2 * pr + hh], olat, preferred_element_type=F32))
        o_ref[0, :, LANES * pr:LANES * (pr + 1)] = jnp.concatenate(outs, axis=0).T.astype(BF16)


def _dsa(ub, c, ct, rowt, wuk_pad, wuv_t, *, topk, sub):
    b, lp, _ = ub.shape
    tq = sub * SEQ_TILE
    ngrp = lp // tq
    nblk = lp // SEQ_TILE
    return pl.pallas_call(
        functools.partial(_dsa_body, topk=topk, sub=sub),
        grid=(b, ngrp),
        in_specs=[
            pl.BlockSpec((1, tq, GROUP_WIDTH), lambda i, q: (i, q, 3)),
            pl.BlockSpec((1, tq, IDX_HEADS * IDX_DIM), lambda i, q: (i, q, 8)),
            pl.BlockSpec((1, lp, LANES), lambda i, q: (i, 0, 18)),
            pl.BlockSpec((1, lp, DSA_LATENT), lambda i, q: (i, 0, 0)),
            pl.BlockSpec((1, nblk, CT_ROWS, SEQ_TILE), lambda i, q: (i, 0, 0, 0)),
            pl.BlockSpec((1, sub, N_HEADS, SEQ_TILE), lambda i, q: (i, q, SM_W // N_HEADS, 0)),
            pl.BlockSpec((N_HEADS, LANES, DSA_LATENT), lambda i, q: (0, 0, 0)),
            pl.BlockSpec((N_HEADS, HEAD_DIM, DSA_LATENT), lambda i, q: (0, 0, 0)),
        ],
        out_specs=pl.BlockSpec((1, tq, GROUP_WIDTH), lambda i, q: (i, q, 0)),
        out_shape=jax.ShapeDtypeStruct((b, lp, GROUP_WIDTH), BF16),
        scratch_shapes=[
            pltpu.VMEM((ngrp, tq, tq), I32),
            pltpu.VMEM((ngrp, tq, tq), F32),
            pltpu.VMEM((ngrp, tq, tq), I16),
            pltpu.VMEM((ngrp, tq, tq), I16),
            pltpu.VMEM((IDX_HEADS * tq, LANES), BF16),
            pltpu.VMEM((N_HEADS * tq, DSA_LATENT), BF16),
            pltpu.VMEM((CT_ROWS, N_HEADS * tq), F32),
            pltpu.VMEM((N_HEADS, 1, tq), F32),
        ],
        compiler_params=pltpu.CompilerParams(
            dimension_semantics=("parallel", "arbitrary"), vmem_limit_bytes=_vmem(48)),
        name="dsa",
    )(ub, ub, ub, c, ct, rowt, wuk_pad, wuv_t)


def _outproj_body(ya_ref, yb_ref, yc_ref, yd_ref, x_ref, w_ref, g_ref, o_ref):
    acc = jnp.dot(ya_ref[...], w_ref[0], preferred_element_type=F32)
    acc = acc + jnp.dot(yb_ref[...], w_ref[1], preferred_element_type=F32)
    acc = acc + jnp.dot(yc_ref[...], w_ref[2], preferred_element_type=F32)
    acc = acc + jnp.dot(yd_ref[...], w_ref[3], preferred_element_type=F32)
    ms = jnp.mean(acc * acc, axis=-1, keepdims=True)
    o_ref[...] = x_ref[...] + acc * lax.rsqrt(ms + EPS) * g_ref[...]


def _out_proj(ys, x2d, w4, g, *, layer, tm):
    m, d = x2d.shape
    gw = GROUP_WIDTH
    yspec = pl.BlockSpec((tm, gw), lambda i: (i, 0))
    return pl.pallas_call(
        _outproj_body,
        grid=(m // tm,),
        in_specs=[yspec, yspec, yspec, yspec,
                  pl.BlockSpec((tm, d), lambda i: (i, 0)),
                  pl.BlockSpec((None, 4, gw, d), lambda i: (layer, 0, 0, 0)),
                  pl.BlockSpec((1, d), lambda i: (0, 0))],
        out_specs=pl.BlockSpec((tm, d), lambda i: (i, 0)),
        out_shape=jax.ShapeDtypeStruct((m, d), F32),
        compiler_params=pltpu.CompilerParams(
            dimension_semantics=("parallel",), vmem_limit_bytes=_vmem(48)),
        name="out_proj",
    )(*ys, x2d, w4, g)


def _ffn_body(x_ref, gpre_ref, wg_ref, wu_ref, cw_ref, cb_ref, wd_ref, gpost_ref, o_ref,
              xn_ref, gbuf_ref, carry_ref, *, tiles_per_seq):
    i = pl.program_id(0)
    f = pl.program_id(1)
    nf = pl.num_programs(1)
    tm = x_ref.shape[0]

    @pl.when(f == 0)
    def _():
        x = x_ref[...]
        ms = jnp.mean(x * x, axis=-1, keepdims=True)
        xn_ref[...] = (x * lax.rsqrt(ms + EPS) * gpre_ref[...]).astype(BF16)
        o_ref[...] = jnp.zeros_like(o_ref)

    @pl.when(i % tiles_per_seq == 0)
    def _():
        carry_ref[f] = jnp.zeros(carry_ref.shape[1:], F32)

    xn = xn_ref[...]
    g = jnp.dot(xn, wg_ref[...], preferred_element_type=F32)
    u = jnp.dot(xn, wu_ref[...], preferred_element_type=F32)
    gbuf_ref[0:8, :] = carry_ref[f]
    gbuf_ref[8:8 + tm, :] = g
    conv = (cb_ref[...]
            + cw_ref[0:1, :] * gbuf_ref[6:6 + tm, :]
            + cw_ref[1:2, :] * gbuf_ref[7:7 + tm, :]
            + cw_ref[2:3, :] * g)
    carry_ref[f] = g[tm - 8:tm, :]
    a = (_silu(conv) * u).astype(BF16)
    o_ref[...] += jnp.dot(a, wd_ref[...], preferred_element_type=F32)

    @pl.when(f == nf - 1)
    def _():
        y = o_ref[...]
        ms = jnp.mean(y * y, axis=-1, keepdims=True)
        o_ref[...] = x_ref[...] + y * lax.rsqrt(ms + EPS) * gpost_ref[...]


def _ffn(x2d, g_pre, w_gate, w_up, conv_w, conv_b, w_down, g_post, *, layer, tm, tf, tiles_per_seq):
    m, d = x2d.shape
    fdim = w_gate.shape[-1]
    nf = fdim // tf
    return pl.pallas_call(
        functools.partial(_ffn_body, tiles_per_seq=tiles_per_seq),
        grid=(m // tm, nf),
        in_specs=[
            pl.BlockSpec((tm, d), lambda i, f: (i, 0)),
            pl.BlockSpec((1, d), lambda i, f: (0, 0)),
            pl.BlockSpec((None, d, tf), lambda i, f: (layer, 0, f)),
            pl.BlockSpec((None, d, tf), lambda i, f: (layer, 0, f)),
            pl.BlockSpec((FFN_CONV, tf), lambda i, f: (0, f)),
            pl.BlockSpec((1, tf), lambda i, f: (0, f)),
            pl.BlockSpec((None, tf, d), lambda i, f: (layer, f, 0)),
            pl.BlockSpec((1, d), lambda i, f: (0, 0)),
        ],
        out_specs=pl.BlockSpec((tm, d), lambda i, f: (i, 0), pipeline_mode=pl.Buffered(1)),
        out_shape=jax.ShapeDtypeStruct((m, d), F32),
        scratch_shapes=[
            pltpu.VMEM((tm, d), BF16),
            pltpu.VMEM((8 + tm, tf), F32),
            pltpu.VMEM((nf, 8, tf), F32),
        ],
        compiler_params=pltpu.CompilerParams(
            dimension_semantics=("arbitrary", "arbitrary"), vmem_limit_bytes=_vmem(60)),
        name="ffn",
    )(x2d, g_pre, w_gate, w_up, conv_w, conv_b, w_down, g_post)


def _permute_w_in(w_in):
    gw = GROUP_WIDTH
    sizes = (gw, SSD_XBC, N_HEADS, gw, 3 * gw, N_HEADS, gw, DSA_LATENT, IDX_HEADS * IDX_DIM, IDX_DIM, IDX_HEADS)
    offs = [0]
    for s in sizes:
        offs.append(offs[-1] + s)
    z, xbc, dt, pool, qkv, fl, dq, dc, dqi, dki, dwi = (w_in[..., offs[k]:offs[k + 1]] for k in range(len(sizes)))
    qkv = jnp.concatenate([qkv[..., :gw] * (LOG2E * HEAD_DIM ** -0.5), qkv[..., gw:]], axis=-1)
    zeros = lambda n: jnp.zeros(w_in.shape[:-1] + (n,), w_in.dtype)
    small = jnp.concatenate([dt, fl, dwi, zeros(SM_A - SM_W - IDX_HEADS), dt, zeros(LANES - SM_A - N_HEADS)], axis=-1)
    a = jnp.concatenate([xbc, z, pool, dc, small, zeros(A_COLS - 2 * gw - SSD_XBC - DSA_LATENT - LANES)], axis=-1)
    bcols = jnp.concatenate([qkv, dq, dqi, dki, dki], axis=-1)
    bcols = jnp.concatenate([bcols, zeros(B_COLS - bcols.shape[-1])], axis=-1)
    return jnp.concatenate([a, bcols], axis=-1)


def _lane_vec(pieces):
    v = jnp.zeros((LANES,), F32)
    for off, val in pieces:
        v = v.at[off:off + val.shape[0]].set(val.astype(F32))
    return v[None, :]


def _pad_head_weights(w_uk, w_uv):
    h, r, d = w_uk.shape
    uk = jnp.zeros((h, 2 * d, r), F32)
    for i in range(h):
        o = d * (i % 2)
        uk = uk.at[i, o:o + d, :].set(w_uk[i].T)
    return uk.astype(BF16), jnp.swapaxes(w_uv, 1, 2).astype(BF16)


def _tile_sizes(b, lp):
    m = b * lp
    tm_proj = next(t for t in (1024, 512, 256, 128) if m % t == 0)
    tm_out = next(t for t in (512, 256, 128) if m % t == 0)
    tm_ffn = next(t for t in (1056, 528, 384, 320, 256, 128) if lp % t == 0)
    att_sub = 3 if lp % (3 * SEQ_TILE) == 0 else 1
    return tm_proj, tm_out, tm_ffn, att_sub


def _layer(h, p, big, layer, *, topk):
    b, lp, d = h.shape
    m = b * lp
    tm_proj, tm_out, tm_ffn, att_sub = _tile_sizes(b, lp)
    row = lambda v: v.astype(F32)[None, :]

    ua, ub = _in_proj(h.reshape(m, d), row(p["norm_mix_pre"]), big["w_in"], layer=layer, tm=tm_proj, tn=1280)
    ua = ua.reshape(b, lp, A_COLS)
    ub = ub.reshape(b, lp, B_COLS)

    bias_vec = _lane_vec([(SM_DT, p["ssd_dt_bias"]), (SM_F, p["fox_f_bias"]), (SM_A, p["ssd_dt_bias"])])
    wscale = jnp.full((IDX_HEADS,), (IDX_HEADS ** -0.5) * (IDX_DIM ** -0.5), F32)
    mul_vec = _lane_vec([(SM_W, wscale), (SM_A, -jnp.exp(p["ssd_a_log"].astype(F32)))])
    col, rowt, c, ct, fkp = _prep(ua, bias_vec, mul_vec, row(p["dsa_kv_norm"]))

    dskip_full = jnp.repeat(p["ssd_d"].astype(F32), HEAD_DIM)[None, :]
    y_a = _ssd(ua, col, rowt, p["ssd_conv_w"].astype(F32), row(p["ssd_conv_b"]), dskip_full, row(p["ssd_norm"]))
    y_b = _pool(ua, p["pool_w"].astype(BF16), row(p["pool_scale"]), sub=att_sub)
    y_c = _fox(ub, fkp, sub=att_sub)
    wuk_pad, wuv_t = _pad_head_weights(p["dsa_w_uk"], p["dsa_w_uv"])
    y_d = _dsa(ub, c, ct, rowt, wuk_pad, wuv_t, topk=topk, sub=att_sub)

    ys = [y.reshape(m, GROUP_WIDTH) for y in (y_a, y_b, y_c, y_d)]
    x1 = _out_proj(ys, h.reshape(m, d), big["w_out"], row(p["norm_mix_post"]), layer=layer, tm=tm_out)
    x2 = _ffn(x1, row(p["norm_ffn_pre"]), big["ffn_w_gate"], big["ffn_w_up"],
              p["ffn_conv_w"].astype(F32), row(p["ffn_conv_b"]), big["ffn_w_down"],
              row(p["norm_ffn_post"]), layer=layer, tm=tm_ffn, tf=512, tiles_per_seq=lp // tm_ffn)
    return x2.reshape(b, lp, d)


def _stack_big_weights(w_in, w_out, ffn_w_gate, ffn_w_up, ffn_w_down):
    depth, d, _ = w_out.shape
    return dict(w_in=_to_bf16(_permute_w_in(w_in)),
                w_out=_to_bf16(w_out).reshape(depth, 4, GROUP_WIDTH, d),
                ffn_w_gate=_to_bf16(ffn_w_gate), ffn_w_up=_to_bf16(ffn_w_up), ffn_w_down=_to_bf16(ffn_w_down))


def kernel(x, meta_tokens, norm_mix_pre, norm_mix_post, norm_ffn_pre, norm_ffn_post, w_in, ssd_conv_w, ssd_conv_b, ssd_dt_bias, ssd_a_log, ssd_d, ssd_norm, pool_w, pool_scale, fox_f_bias, dsa_kv_norm, dsa_w_uk, dsa_w_uv, w_out, ffn_w_gate, ffn_w_up, ffn_conv_w, ffn_conv_b, ffn_w_down):
    bsz, seq, d = x.shape
    n = N_META + seq
    lp = -(-n // SEQ_TILE) * SEQ_TILE
    topk = min(DSA_TOPK_MAX, seq // 4)
    meta = jnp.broadcast_to(meta_tokens.astype(x.dtype)[None], (bsz, N_META, d))
    h = jnp.concatenate([meta, x, jnp.zeros((bsz, lp - n, d), x.dtype)], axis=1)
    small = dict(norm_mix_pre=norm_mix_pre, norm_mix_post=norm_mix_post, norm_ffn_pre=norm_ffn_pre,
                 norm_ffn_post=norm_ffn_post, ssd_conv_w=ssd_conv_w, ssd_conv_b=ssd_conv_b,
                 ssd_dt_bias=ssd_dt_bias, ssd_a_log=ssd_a_log, ssd_d=ssd_d, ssd_norm=ssd_norm,
                 pool_w=pool_w, pool_scale=pool_scale, fox_f_bias=fox_f_bias, dsa_kv_norm=dsa_kv_norm,
                 dsa_w_uk=dsa_w_uk, dsa_w_uv=dsa_w_uv, ffn_conv_w=ffn_conv_w, ffn_conv_b=ffn_conv_b)
    big = _stack_big_weights(w_in, w_out, ffn_w_gate, ffn_w_up, ffn_w_down)
    for i in range(norm_mix_pre.shape[0]):
        h = _layer(h, {k: v[i] for k, v in small.items()}, big, i, topk=topk)
    return h[:, N_META:n]
```

```python
import functools

import jax
import jax.numpy as jnp
from jax import lax
from jax.experimental import pallas as pl
from jax.experimental.pallas import tpu as pltpu

F32 = jnp.float32
BF16 = jnp.bfloat16
I32 = jnp.int32
I16 = jnp.int16

EPS = 1e-6
N_META = 16
CHUNK = 64
HEAD_DIM = 64
GROUP_WIDTH = 512
N_HEADS = 8
SSD_GROUPS = 2
SSD_STATE = 128
SSD_CONV = 4
SSD_XBC = GROUP_WIDTH + 2 * SSD_GROUPS * SSD_STATE
POOL_WINDOWS = (2, 4, 8, 16)
DSA_LATENT = 128
IDX_HEADS = 4
IDX_DIM = 64
DSA_TOPK_MAX = 256
FFN_CONV = 3

LANES = 128
SEQ_TILE = 128
INT_MIN = -(2 ** 31)
HALF = 1 << 16
HALF_BIAS = 1 << 15
SUBLANES = 8
NEG_BIG = -1e30
LOG2E = 1.4426950408889634

A_COLS = 2560
B_COLS = 2560
A_XBC, A_Z, A_POOL, A_CKV, A_SMALL = 0, 1024, 1536, 2048, 2176
B_FQ, B_FK, B_FV, B_DQ, B_IQ, B_IK = 0, 512, 1024, 1536, 2048, 2304
SM_DT = 0
SM_F = 8
SM_W = 16
SM_A = 24
CT_ROWS = DSA_LATENT + 16
ACC_ROWS = 2 * HEAD_DIM + 16
TAIL = N_META


def _vmem(mb):
    return int(mb * 1024 * 1024)


def _softplus_parts(x):
    t = jnp.log1p(jnp.exp(-jnp.abs(x)))
    return jnp.maximum(x, 0.0) + t, jnp.minimum(x, 0.0) - t


def _silu(x):
    return x / (1.0 + jnp.exp(-x))


def _cast_body(x_ref, o_ref):
    o_ref[...] = x_ref[...].astype(BF16)


def _to_bf16(w, *, tr=256):
    shape = w.shape
    w2 = w.reshape(-1, shape[-1])
    r, c = w2.shape
    out = pl.pallas_call(
        _cast_body,
        grid=(r // tr,),
        in_specs=[pl.BlockSpec((tr, c), lambda i: (i, 0))],
        out_specs=pl.BlockSpec((tr, c), lambda i: (i, 0)),
        out_shape=jax.ShapeDtypeStruct((r, c), BF16),
        compiler_params=pltpu.CompilerParams(dimension_semantics=("parallel",), vmem_limit_bytes=_vmem(40)),
        name="to_bf16",
    )(w2)
    return out.reshape(shape)


def _inproj_body(x_ref, g_ref, w_ref, oa_ref, ob_ref, xn_ref, *, n_a):
    j = pl.program_id(1)

    @pl.when(j == 0)
    def _():
        x = x_ref[...]
        ms = jnp.mean(x * x, axis=-1, keepdims=True)
        xn_ref[...] = (x * lax.rsqrt(ms + EPS) * g_ref[...]).astype(BF16)

    acc = jnp.dot(xn_ref[...], w_ref[...], preferred_element_type=F32)

    @pl.when(j < n_a)
    def _():
        oa_ref[...] = acc

    @pl.when(j >= n_a)
    def _():
        ob_ref[...] = acc.astype(BF16)


def _in_proj(x2d, g, w_perm, *, layer, tm, tn):
    m, d = x2d.shape
    n_a, n_b = A_COLS // tn, B_COLS // tn
    return pl.pallas_call(
        functools.partial(_inproj_body, n_a=n_a),
        grid=(m // tm, n_a + n_b),
        in_specs=[
            pl.BlockSpec((tm, d), lambda i, j: (i, 0)),
            pl.BlockSpec((1, d), lambda i, j: (0, 0)),
            pl.BlockSpec((None, d, tn), lambda i, j: (layer, 0, j)),
        ],
        out_specs=[
            pl.BlockSpec((tm, tn), lambda i, j: (i, jnp.minimum(j, n_a - 1))),
            pl.BlockSpec((tm, tn), lambda i, j: (i, jnp.maximum(j - n_a, 0))),
        ],
        out_shape=[
            jax.ShapeDtypeStruct((m, A_COLS), F32),
            jax.ShapeDtypeStruct((m, B_COLS), BF16),
        ],
        scratch_shapes=[pltpu.VMEM((tm, d), BF16)],
        compiler_params=pltpu.CompilerParams(
            dimension_semantics=("parallel", "arbitrary"), vmem_limit_bytes=_vmem(58)),
        name="in_proj",
    )(x2d, g, w_perm)


def _prep_body(sm_ref, dc_ref, bias_ref, mul_ref, kvg_ref, col_ref, row_ref, c_ref, ct_ref, fkp_ref,
               carry_ref):
    t = pl.program_id(1)

    @pl.when(t == 0)
    def _():
        carry_ref[...] = jnp.zeros_like(carry_ref)

    tt = sm_ref.shape[1]
    sub = tt // SEQ_TILE
    s = sm_ref[0]
    lane = lax.broadcasted_iota(I32, (tt, LANES), 1)
    is_dt = lane < SM_F
    is_f = (lane >= SM_F) & (lane < SM_W)
    is_a = (lane >= SM_A) & (lane < SM_A + N_HEADS)
    sp, ls = _softplus_parts(s + bias_ref[...])
    v = jnp.where(is_dt, sp, jnp.where(is_f, ls, jnp.where(is_a, sp, s) * mul_ref[...]))
    ri = lax.broadcasted_iota(I32, (tt, tt), 0)
    ci = lax.broadcasted_iota(I32, (tt, tt), 1)
    tril = jnp.where(ci <= ri, 1.0, 0.0).astype(F32)
    run = jnp.dot(tril, v, precision=lax.Precision.HIGHEST, preferred_element_type=F32)
    rowi = lax.broadcasted_iota(I32, (tt, 1), 0)
    local = run
    for j in range(1, sub):
        local = jnp.where(rowi >= SEQ_TILE * j, run - run[SEQ_TILE * j - 1:SEQ_TILE * j, :], local)
    out = jnp.where(is_f, run + carry_ref[...], jnp.where(is_a, local, v))
    col_ref[0] = out
    out_t = out.T
    for j in range(sub):
        row_ref[0, j] = out_t[:, SEQ_TILE * j:SEQ_TILE * (j + 1)]
    carry_ref[...] = jnp.where(is_f[0:1], out[tt - 1:tt, :], 0.0)

    f0 = jnp.where(is_f, out, 0.0) * LOG2E
    hi = f0.astype(BF16).astype(F32)
    r1 = f0 - hi
    mid = r1.astype(BF16).astype(F32)
    lo = (r1 - mid).astype(BF16).astype(F32)
    fkp_ref[0] = (hi + pltpu.roll(mid, N_HEADS, axis=1) + pltpu.roll(lo, 2 * N_HEADS, axis=1)).astype(BF16)

    dc = dc_ref[0]
    ms = jnp.mean(dc * dc, axis=-1, keepdims=True)
    cn = dc * lax.rsqrt(ms + EPS) * kvg_ref[...]
    c_ref[0] = cn.astype(BF16)
    cn_t = cn.T
    ones = jnp.ones((CT_ROWS - DSA_LATENT, SEQ_TILE), F32)
    for j in range(sub):
        ct_ref[0, j] = jnp.concatenate([cn_t[:, SEQ_TILE * j:SEQ_TILE * (j + 1)], ones], axis=0).astype(BF16)


def _prep(ua, bias_vec, mul_vec, kv_g, *, sub):
    b, lp, _ = ua.shape
    tt = sub * SEQ_TILE
    nblk = lp // SEQ_TILE
    return pl.pallas_call(
        _prep_body,
        grid=(b, lp // tt),
        in_specs=[
            pl.BlockSpec((1, tt, LANES), lambda i, t: (i, t, A_SMALL // LANES)),
            pl.BlockSpec((1, tt, LANES), lambda i, t: (i, t, A_CKV // LANES)),
            pl.BlockSpec((1, LANES), lambda i, t: (0, 0)),
            pl.BlockSpec((1, LANES), lambda i, t: (0, 0)),
            pl.BlockSpec((1, LANES), lambda i, t: (0, 0)),
        ],
        out_specs=[
            pl.BlockSpec((1, tt, LANES), lambda i, t: (i, t, 0)),
            pl.BlockSpec((1, sub, LANES, SEQ_TILE), lambda i, t: (i, t, 0, 0)),
            pl.BlockSpec((1, tt, LANES), lambda i, t: (i, t, 0)),
            pl.BlockSpec((1, sub, CT_ROWS, SEQ_TILE), lambda i, t: (i, t, 0, 0)),
            pl.BlockSpec((1, tt, LANES), lambda i, t: (i, t, 0)),
        ],
        out_shape=[
            jax.ShapeDtypeStruct((b, lp, LANES), F32),
            jax.ShapeDtypeStruct((b, nblk, LANES, SEQ_TILE), F32),
            jax.ShapeDtypeStruct((b, lp, DSA_LATENT), BF16),
            jax.ShapeDtypeStruct((b, nblk, CT_ROWS, SEQ_TILE), BF16),
            jax.ShapeDtypeStruct((b, lp, LANES), BF16),
        ],
        scratch_shapes=[pltpu.VMEM((1, LANES), F32)],
        compiler_params=pltpu.CompilerParams(dimension_semantics=("parallel", "arbitrary")),
        name="prep",
    )(ua, ua, bias_vec, mul_vec, kv_g)


def _expand_heads(colv, base, lo_half):
    parts = []
    for p in range(N_HEADS // 2):
        a = colv[:, base + 2 * p:base + 2 * p + 1]
        b = colv[:, base + 2 * p + 1:base + 2 * p + 2]
        parts.append(jnp.where(lo_half, a, b))
    return jnp.concatenate(parts, axis=1)


def _ssd_body(xbc_ref, z_ref, col_ref, row_ref, cw_ref, cb_ref, dsk_ref, ng_ref, y_ref,
              xpad_ref, st_ref):
    c = pl.program_id(1)
    tt = xbc_ref.shape[1]
    ll = SEQ_TILE
    gw = GROUP_WIDTH
    ns = SSD_STATE
    hpg = N_HEADS // SSD_GROUPS
    gcols = hpg * HEAD_DIM

    @pl.when(c == 0)
    def _():
        xpad_ref[0:SUBLANES, :] = jnp.zeros((SUBLANES, SSD_XBC), F32)
        st_ref[...] = jnp.zeros_like(st_ref)

    x = xbc_ref[0]
    xpad_ref[SUBLANES:SUBLANES + tt, :] = x
    conv = cb_ref[...] + cw_ref[SSD_CONV - 1:SSD_CONV, :] * x
    for k in range(SSD_CONV - 1):
        lag = SUBLANES - (SSD_CONV - 1 - k)
        conv = conv + cw_ref[k:k + 1, :] * xpad_ref[lag:lag + tt, :]
    xpad_ref[0:SUBLANES, :] = x[tt - SUBLANES:tt, :]
    act_all = _silu(conv)
    gate_all = _silu(z_ref[0])
    lo_half = lax.broadcasted_iota(I32, (ll, LANES), 1) < HEAD_DIM
    tril = lax.broadcasted_iota(I32, (ll, ll), 1) <= lax.broadcasted_iota(I32, (ll, ll), 0)

    for j in range(tt // ll):
        rows = slice(ll * j, ll * (j + 1))
        act = act_all[rows]
        xs = act[:, 0:gw]
        bm = act[:, gw:gw + SSD_GROUPS * ns]
        cm = act[:, gw + SSD_GROUPS * ns:]
        colv = col_ref[0, rows, :]
        rowv = row_ref[0, j]
        dt_full = _expand_heads(colv, SM_DT, lo_half)
        acs_full = _expand_heads(colv, SM_A, lo_half)
        acs_last = acs_full[ll - 1:ll, :]
        dte_full = jnp.exp(acs_last - acs_full)
        dfs_full = jnp.exp(acs_full)
        xdt = xs * dt_full
        xdt_b = xdt.astype(BF16)
        xdte_b = (xdt * dte_full).astype(BF16)
        cm_b = cm.astype(BF16)

        ys = []
        for g in range(SSD_GROUPS):
            bg = bm[:, ns * g:ns * (g + 1)]
            bg_b = bg.astype(BF16)
            bgt_b = bg.T.astype(BF16)
            cg_b = cm_b[:, ns * g:ns * (g + 1)]
            cb = lax.dot_general(cg_b, bg_b, (((1,), (1,)), ((), ())), preferred_element_type=F32)
            sg = st_ref[g]
            yoff = (jnp.dot(cg_b, sg.astype(BF16), preferred_element_type=F32)
                    * dfs_full[:, gcols * g:gcols * (g + 1)])
            parts = []
            for pr in range(hpg // 2):
                xpair = xdt_b[:, gcols * g + LANES * pr:gcols * g + LANES * (pr + 1)]
                res = []
                for hh in range(2):
                    h = hpg * g + 2 * pr + hh
                    seg = colv[:, SM_A + h:SM_A + h + 1] - rowv[SM_A + h:SM_A + h + 1, :]
                    lm = jnp.exp(jnp.where(tril, seg, -jnp.inf))
                    res.append(jnp.dot((cb * lm).astype(BF16), xpair, preferred_element_type=F32))
                parts.append(jnp.where(lo_half, res[0], res[1]))
            ydiag = jnp.concatenate(parts, axis=1)
            decay = jnp.exp(acs_last[:, gcols * g:gcols * (g + 1)])
            st_ref[g] = decay * sg + jnp.dot(bgt_b, xdte_b[:, gcols * g:gcols * (g + 1)],
                                             preferred_element_type=F32)
            ys.append(ydiag + yoff)

        y = jnp.concatenate(ys, axis=1) + dsk_ref[...] * xs
        gz = y * gate_all[rows]
        outs = []
        for g in range(SSD_GROUPS):
            gg = gz[:, gcols * g:gcols * (g + 1)]
            outs.append(gg * lax.rsqrt(jnp.mean(gg * gg, axis=-1, keepdims=True) + EPS))
        y_ref[0, rows, :] = (jnp.concatenate(outs, axis=1) * ng_ref[...]).astype(BF16)


def _ssd(ua, col, row, conv_w, conv_b, dskip_full, norm_g, *, sub):
    b, lp, _ = ua.shape
    tt = sub * SEQ_TILE
    hpg = N_HEADS // SSD_GROUPS
    return pl.pallas_call(
        _ssd_body,
        grid=(b, lp // tt),
        in_specs=[
            pl.BlockSpec((1, tt, SSD_XBC), lambda i, c: (i, c, A_XBC // SSD_XBC)),
            pl.BlockSpec((1, tt, GROUP_WIDTH), lambda i, c: (i, c, A_Z // GROUP_WIDTH)),
            pl.BlockSpec((1, tt, LANES), lambda i, c: (i, c, 0)),
            pl.BlockSpec((1, sub, LANES, SEQ_TILE), lambda i, c: (i, c, 0, 0)),
            pl.BlockSpec((SSD_CONV, SSD_XBC), lambda i, c: (0, 0)),
            pl.BlockSpec((1, SSD_XBC), lambda i, c: (0, 0)),
            pl.BlockSpec((1, GROUP_WIDTH), lambda i, c: (0, 0)),
            pl.BlockSpec((1, GROUP_WIDTH), lambda i, c: (0, 0)),
        ],
        out_specs=pl.BlockSpec((1, tt, GROUP_WIDTH), lambda i, c: (i, c, 0)),
        out_shape=jax.ShapeDtypeStruct((b, lp, GROUP_WIDTH), BF16),
        scratch_shapes=[
            pltpu.VMEM((SUBLANES + tt, SSD_XBC), F32),
            pltpu.VMEM((SSD_GROUPS, SSD_STATE, hpg * HEAD_DIM), F32),
        ],
        compiler_params=pltpu.CompilerParams(dimension_semantics=("parallel", "arbitrary")),
        name="ssd",
    )(ua, ua, col, row, conv_w, conv_b, dskip_full, norm_g)


def _pool_body(u_ref, w_ref, sc_ref, y_ref, buf_ref):
    t = pl.program_id(1)
    tt = u_ref.shape[1]
    hist = max(POOL_WINDOWS)
    gd = GROUP_WIDTH // len(POOL_WINDOWS)

    @pl.when(t == 0)
    def _():
        buf_ref[0:hist, :] = jnp.zeros((hist, GROUP_WIDTH), F32)

    u = u_ref[0]
    buf_ref[hist:hist + tt, :] = u
    count = (t * tt + 1 + lax.broadcasted_iota(I32, (tt, 1), 0)).astype(F32)
    outs = []
    for gi, win in enumerate(POOL_WINDOWS):
        ug = u[:, gd * gi:gd * (gi + 1)]
        acc = ug
        for k in range(1, win):
            acc = acc + buf_ref[hist - k:hist - k + tt, gd * gi:gd * (gi + 1)]
        pooled = acc / jnp.minimum(count, float(win)) - ug
        outs.append(jnp.dot(pooled.astype(BF16), w_ref[gi], preferred_element_type=F32))
    y_ref[0] = (jnp.concatenate(outs, axis=1) * sc_ref[...]).astype(BF16)
    buf_ref[0:hist, :] = u[tt - hist:tt, :]


def _pool(ua, w, scale, *, sub):
    b, lp, _ = ua.shape
    tt = sub * SEQ_TILE
    ng = len(POOL_WINDOWS)
    gd = GROUP_WIDTH // ng
    return pl.pallas_call(
        _pool_body,
        grid=(b, lp // tt),
        in_specs=[
            pl.BlockSpec((1, tt, GROUP_WIDTH), lambda i, t: (i, t, A_POOL // GROUP_WIDTH)),
            pl.BlockSpec((ng, gd, gd), lambda i, t: (0, 0, 0)),
            pl.BlockSpec((1, GROUP_WIDTH), lambda i, t: (0, 0)),
        ],
        out_specs=pl.BlockSpec((1, tt, GROUP_WIDTH), lambda i, t: (i, t, 0)),
        out_shape=jax.ShapeDtypeStruct((b, lp, GROUP_WIDTH), BF16),
        scratch_shapes=[pltpu.VMEM((max(POOL_WINDOWS) + tt, GROUP_WIDTH), F32)],
        compiler_params=pltpu.CompilerParams(dimension_semantics=("parallel", "arbitrary")),
        name="pool",
    )(ua, w, scale)


def _fox_body(q_ref, k_ref, v_ref, fkp_ref, o_ref, qa_ref, vl_ref, acc_ref, m_ref):
    qi = pl.program_id(1)
    tq = q_ref.shape[1]
    tk = tq
    nt = (((1,), (1,)), ((), ()))
    npair = N_HEADS // 2
    lane = lax.broadcasted_iota(I32, (tq, LANES), 1)
    lo_half = lane < HEAD_DIM
    zero_b = jnp.zeros((tq, LANES), BF16)

    @pl.when(qi == 0)
    def _():
        nkt = vl_ref.shape[0]
        rsel = lax.broadcasted_iota(I32, (ACC_ROWS - 2 * HEAD_DIM, 2 * tk), 0)
        csel = lax.broadcasted_iota(I32, (ACC_ROWS - 2 * HEAD_DIM, 2 * tk), 1)
        ones_rows = jnp.where((rsel == 0) & (csel < tk) | (rsel == 1) & (csel >= tk), 1.0, 0.0).astype(BF16)
        zpad = jnp.zeros((HEAD_DIM, tk), BF16)

        def fill(kt, carry):
            ks = pl.multiple_of(kt * tk, tk)
            for pr in range(npair):
                vt = v_ref[0, pl.ds(ks, tk), LANES * pr:LANES * (pr + 1)].astype(F32).T.astype(BF16)
                top = jnp.concatenate([vt[:HEAD_DIM], zpad], axis=1)
                bot = jnp.concatenate([zpad, vt[HEAD_DIM:]], axis=1)
                vl_ref[kt, pr] = jnp.concatenate([top, bot, ones_rows], axis=0)
            return carry

        lax.fori_loop(0, nkt, fill, 0)

    for pr in range(npair):
        q2 = q_ref[0, :, LANES * pr:LANES * (pr + 1)]
        for hh in range(2):
            h = 2 * pr + hh
            qm = jnp.where(lo_half, q2, zero_b) if hh == 0 else jnp.where(lo_half, zero_b, q2)
            pick = (lane == SM_F + h) | (lane == SM_F + N_HEADS + h) | (lane == SM_F + 2 * N_HEADS + h)
            qa_ref[h] = jnp.concatenate([qm, jnp.where(pick, -1.0, 0.0).astype(BF16)], axis=1)
    acc_ref[...] = jnp.zeros_like(acc_ref)
    m_ref[...] = jnp.full(m_ref.shape, NEG_BIG, F32)
    causal_t = (lax.broadcasted_iota(I32, (tk, tq), 0) <= lax.broadcasted_iota(I32, (tk, tq), 1))
    tail_row = lax.broadcasted_iota(I32, (ACC_ROWS - 2 * HEAD_DIM, tq), 0)

    def step(kt, masked):
        ks = pl.multiple_of(kt * tk, tk)
        fkp = fkp_ref[0, pl.ds(ks, tk), :]
        for pr in range(npair):
            ka = jnp.concatenate([k_ref[0, pl.ds(ks, tk), LANES * pr:LANES * (pr + 1)], fkp], axis=1)
            ps, als = [], []
            for hh in range(2):
                h = 2 * pr + hh
                st = lax.dot_general(ka, qa_ref[h], nt, preferred_element_type=F32)
                if masked:
                    st = jnp.where(causal_t, st, -jnp.inf)
                m_old = m_ref[h]
                mn = jnp.maximum(m_old, jnp.max(st, axis=0, keepdims=True))
                ps.append(jnp.exp2(st - mn).astype(BF16))
                als.append(jnp.exp2(m_old - mn))
                m_ref[h] = mn
            upd = jnp.dot(vl_ref[kt, pr], jnp.concatenate(ps, axis=0), preferred_element_type=F32)
            scale = jnp.concatenate([jnp.broadcast_to(als[0], (HEAD_DIM, tq)),
                                     jnp.broadcast_to(als[1], (HEAD_DIM, tq)),
                                     jnp.where(tail_row == 0, als[0], als[1])], axis=0)
            acc_ref[pr] = scale * acc_ref[pr] + upd

    def body(kt, carry):
        step(kt, False)
        return carry

    lax.fori_loop(0, qi, body, 0)
    step(qi, True)
    for pr in range(npair):
        a = acc_ref[pr]
        o = jnp.concatenate([a[:HEAD_DIM] / a[2 * HEAD_DIM:2 * HEAD_DIM + 1],
                             a[HEAD_DIM:2 * HEAD_DIM] / a[2 * HEAD_DIM + 1:2 * HEAD_DIM + 2]], axis=0)
        o_ref[0, :, LANES * pr:LANES * (pr + 1)] = o.T.astype(BF16)


def _fox(ub, fkp, *, sub):
    b, lp, _ = ub.shape
    tq = sub * SEQ_TILE
    return pl.pallas_call(
        _fox_body,
        grid=(b, lp // tq),
        in_specs=[
            pl.BlockSpec((1, tq, GROUP_WIDTH), lambda i, q: (i, q, B_FQ // GROUP_WIDTH)),
            pl.BlockSpec((1, lp, GROUP_WIDTH), lambda i, q: (i, 0, B_FK // GROUP_WIDTH)),
            pl.BlockSpec((1, lp, GROUP_WIDTH), lambda i, q: (i, 0, B_FV // GROUP_WIDTH)),
            pl.BlockSpec((1, lp, LANES), lambda i, q: (i, 0, 0)),
        ],
        out_specs=pl.BlockSpec((1, tq, GROUP_WIDTH), lambda i, q: (i, q, 0)),
        out_shape=jax.ShapeDtypeStruct((b, lp, GROUP_WIDTH), BF16),
        scratch_shapes=[
            pltpu.VMEM((N_HEADS, tq, 2 * LANES), BF16),
            pltpu.VMEM((lp // tq, N_HEADS // 2, ACC_ROWS, 2 * tq), BF16),
            pltpu.VMEM((N_HEADS // 2, ACC_ROWS, tq), F32),
            pltpu.VMEM((N_HEADS, 1, tq), F32),
        ],
        compiler_params=pltpu.CompilerParams(
            dimension_semantics=("parallel", "arbitrary"), vmem_limit_bytes=_vmem(48)),
        name="fox",
    )(ub, ub, ub, fkp)


def _dsa_body(dq_ref, dqi_ref, ki_ref, c_ref, ct_ref, wr_ref, wuk_ref, wuvt_ref, o_ref,
              key_ref, bias_ref, hi_ref, lo_ref, qm_ref, qlat_ref, acc_ref, m_ref, *, topk, sub):
    qi = pl.program_id(1)
    tq = dq_ref.shape[1]
    tk = tq
    ngrp = qi + 1
    has_tail = qi + 1 < key_ref.shape[0]
    tblk = jnp.minimum(sub * (qi + 1), ct_ref.shape[1] - 1)
    t0 = pl.multiple_of(tblk * SEQ_TILE, SEQ_TILE)
    nt = (((1,), (1,)), ((), ()))
    lo_half = lax.broadcasted_iota(I32, (tq, LANES), 1) < IDX_DIM
    zero_b = jnp.zeros((tq, LANES), BF16)
    shift = CHUNK - N_META
    lg2 = CHUNK.bit_length() - 1
    kf = float(topk)

    def fold8(w):
        return jnp.sum(w.reshape(w.shape[0] // 8, 8, tq), axis=0)

    for h in range(IDX_HEADS):
        q2 = dqi_ref[0, :, LANES * (h // 2):LANES * (h // 2 + 1)]
        qm_ref[tq * h:tq * (h + 1), :] = (jnp.where(lo_half, q2, zero_b) if h % 2 == 0
                                           else jnp.where(lo_half, zero_b, q2))
    wrows = jnp.concatenate([wr_ref[0, j] for j in range(sub)], axis=1)

    def keys_of(kt):
        lg_all = lax.dot_general(kt, qm_ref[...], nt, preferred_element_type=F32)
        sc = None
        for h in range(IDX_HEADS):
            term = wrows[h:h + 1, :] * jnp.maximum(lg_all[:, tq * h:tq * (h + 1)], 0.0)
            sc = term if sc is None else sc + term
        bits = pltpu.bitcast(sc, I32)
        bits = jnp.where(bits == INT_MIN, 0, bits)
        return bits ^ ((bits >> 31) & 0x7FFFFFFF)

    def admissible(ks, rows):
        kcid = (ks + lax.broadcasted_iota(I32, (rows, 1), 0) + shift) >> lg2
        qcid = (qi * tq + lax.broadcasted_iota(I32, (1, tq), 1) + shift) >> lg2
        return kcid <= qcid

    def store_keys(g, key):
        key_ref[g] = key
        hi_ref[g] = (key >> 16).astype(I16)
        lo_ref[g] = ((key & (HALF - 1)) - HALF_BIAS).astype(I16)

    def score_step(g, carry):
        ks = pl.multiple_of(g * tk, tk)
        store_keys(g, keys_of(ki_ref[0, pl.ds(ks, tk), :]))
        return carry

    lax.fori_loop(0, qi, score_step, 0)
    q0 = pl.multiple_of(qi * tk, tk)
    store_keys(qi, jnp.where(admissible(q0, tk), keys_of(ki_ref[0, pl.ds(q0, tk), :]), INT_MIN))
    tkey = jnp.where(admissible(t0, TAIL) & has_tail, keys_of(ki_ref[0, pl.ds(t0, TAIL), :]), INT_MIN)

    one_b = jnp.ones((), BF16)
    zero_s = jnp.zeros((), BF16)

    def count_ge(x16, s16):
        w = jnp.where(x16 >= s16, one_b, zero_s)
        w3 = w.reshape(x16.shape[0] // 16, 16, tq)
        return functools.reduce(lambda a, b: a + b, [w3[r] for r in range(w3.shape[0])]).astype(F32)

    def bisect16(ref, tail16, need):
        def bit_step(i, u):
            uc = u | lax.shift_left(jnp.int32(1), 15 - i)
            s16 = (uc - HALF_BIAS).astype(I16)

            def cnt_step(g, acc):
                return acc + count_ge(ref[g], s16)

            acc = lax.fori_loop(0, ngrp, cnt_step, count_ge(tail16, s16))
            cnt = jnp.sum(acc, axis=0, keepdims=True)
            return jnp.where(cnt >= need, uc, u)

        return lax.fori_loop(0, 16, bit_step, jnp.zeros((1, tq), I32))

    thi = tkey >> 16
    tlo = (tkey & (HALF - 1)) - HALF_BIAS
    u_hi = bisect16(hi_ref, thi.astype(I16), kf)
    t_hi = u_hi - HALF_BIAS
    above16 = (t_hi + 1).astype(I16)

    def above_step(g, acc):
        return acc + count_ge(hi_ref[g], above16)

    n_above = jnp.sum(lax.fori_loop(0, ngrp, above_step, count_ge(thi.astype(I16), above16)), axis=0, keepdims=True)
    n_above = jnp.where(t_hi == HALF_BIAS - 1, 0.0, n_above)
    t_hi16 = t_hi.astype(I16)

    def narrow_step(g, carry):
        lo_ref[g] = jnp.where(hi_ref[g] == t_hi16, lo_ref[g], jnp.int16(-HALF_BIAS))
        return carry

    lax.fori_loop(0, ngrp, narrow_step, 0)
    tlo16 = jnp.where(thi == t_hi, tlo, -HALF_BIAS).astype(I16)
    u_lo = bisect16(lo_ref, tlo16, kf - n_above)
    thr = t_hi * HALF + u_lo

    def gt_step(g, acc):
        return acc + fold8(jnp.where(key_ref[g] > thr, 1.0, 0.0))

    ngt = jnp.sum(lax.fori_loop(0, ngrp, gt_step, fold8(jnp.where(tkey > thr, 1.0, 0.0))),
                  axis=0, keepdims=True)
    room = kf - ngt
    incl = jnp.where(lax.broadcasted_iota(I32, (tk, tk), 1) <= lax.broadcasted_iota(I32, (tk, tk), 0),
                     1.0, 0.0).astype(BF16)
    incl_tail = jnp.where(lax.broadcasted_iota(I32, (TAIL, TAIL), 1) <= lax.broadcasted_iota(I32, (TAIL, TAIL), 0),
                          1.0, 0.0).astype(BF16)

    room = jnp.where(thr == INT_MIN, 0.0, room)

    def bias_of(key, seen):
        rows = key.shape[0]
        eq = key == thr
        eqf = jnp.where(eq, 1.0, 0.0)
        rank = jnp.dot(incl if rows == tk else incl_tail, eqf.astype(BF16), preferred_element_type=F32) + seen
        tie = jnp.where(eq, jnp.where(rank <= room, 0.0, -jnp.inf), -jnp.inf)
        return jnp.where(key > thr, 0.0, tie), seen + jnp.sum(fold8(eqf), axis=0, keepdims=True)

    def mask_step(g, seen):
        bias_ref[g], seen = bias_of(key_ref[g], seen)
        return seen

    seen = lax.fori_loop(0, ngrp, mask_step, jnp.zeros((1, tq), F32))
    tbias, _ = bias_of(tkey, seen)
    tbias_blk = jnp.concatenate([tbias, jnp.full((SEQ_TILE - TAIL, tq), -jnp.inf, F32)], axis=0)

    for h in range(N_HEADS):
        dq2 = dq_ref[0, :, LANES * (h // 2):LANES * (h // 2 + 1)]
        ql = jnp.dot(dq2, wuk_ref[h], preferred_element_type=F32) * (LOG2E * HEAD_DIM ** -0.5)
        qlat_ref[tq * h:tq * (h + 1), :] = ql.astype(BF16)
    acc_ref[...] = jnp.zeros_like(acc_ref)
    m_ref[...] = jnp.full(m_ref.shape, NEG_BIG, F32)

    def attend(ck, cx, bias):
        st_all = lax.dot_general(ck, qlat_ref[...], nt, preferred_element_type=F32)
        ps, als = [], []
        for h in range(N_HEADS):
            st = st_all[:, tq * h:tq * (h + 1)] + bias
            m_old = m_ref[h]
            mn = jnp.maximum(m_old, jnp.max(st, axis=0, keepdims=True))
            ps.append(jnp.exp2(st - mn).astype(BF16))
            als.append(jnp.exp2(m_old - mn))
            m_ref[h] = mn
        upd = jnp.dot(cx, jnp.concatenate(ps, axis=1), preferred_element_type=F32)
        acc_ref[...] = jnp.concatenate(als, axis=1) * acc_ref[...] + upd

    def att_step(g, carry):
        ks = pl.multiple_of(g * tk, tk)
        cx = jnp.concatenate([ct_ref[0, g * sub + j] for j in range(sub)], axis=1)
        attend(c_ref[0, pl.ds(ks, tk), :], cx, bias_ref[g])
        return carry

    lax.fori_loop(0, ngrp, att_step, 0)
    attend(c_ref[0, pl.ds(t0, SEQ_TILE), :], ct_ref[0, tblk], tbias_blk)

    for pr in range(N_HEADS // 2):
        outs = []
        for hh in range(2):
            a = acc_ref[:, tq * (2 * pr + hh):tq * (2 * pr + hh + 1)]
            olat = (a[:DSA_LATENT] / a[DSA_LATENT:DSA_LATENT + 1]).astype(BF16)
            outs.append(jnp.dot(wuvt_ref[2 * pr + hh], olat, preferred_element_type=F32))
        o_ref[0, :, LANES * pr:LANES * (pr + 1)] = jnp.concatenate(outs, axis=0).T.astype(BF16)


def _dsa(ub, c, ct, rowt, wuk_pad, wuv_t, *, topk, sub):
    b, lp, _ = ub.shape
    tq = sub * SEQ_TILE
    ngrp = lp // tq
    nblk = lp // SEQ_TILE
    return pl.pallas_call(
        functools.partial(_dsa_body, topk=topk, sub=sub),
        grid=(b, ngrp),
        in_specs=[
            pl.BlockSpec((1, tq, GROUP_WIDTH), lambda i, q: (i, q, B_DQ // GROUP_WIDTH)),
            pl.BlockSpec((1, tq, IDX_HEADS * IDX_DIM), lambda i, q: (i, q, B_IQ // (IDX_HEADS * IDX_DIM))),
            pl.BlockSpec((1, lp, LANES), lambda i, q: (i, 0, B_IK // LANES)),
            pl.BlockSpec((1, lp, DSA_LATENT), lambda i, q: (i, 0, 0)),
            pl.BlockSpec((1, nblk, CT_ROWS, SEQ_TILE), lambda i, q: (i, 0, 0, 0)),
            pl.BlockSpec((1, sub, N_HEADS, SEQ_TILE), lambda i, q: (i, q, SM_W // N_HEADS, 0)),
            pl.BlockSpec((N_HEADS, LANES, DSA_LATENT), lambda i, q: (0, 0, 0)),
            pl.BlockSpec((N_HEADS, HEAD_DIM, DSA_LATENT), lambda i, q: (0, 0, 0)),
        ],
        out_specs=pl.BlockSpec((1, tq, GROUP_WIDTH), lambda i, q: (i, q, 0)),
        out_shape=jax.ShapeDtypeStruct((b, lp, GROUP_WIDTH), BF16),
        scratch_shapes=[
            pltpu.VMEM((ngrp, tq, tq), I32),
            pltpu.VMEM((ngrp, tq, tq), F32),
            pltpu.VMEM((ngrp, tq, tq), I16),
            pltpu.VMEM((ngrp, tq, tq), I16),
            pltpu.VMEM((IDX_HEADS * tq, LANES), BF16),
            pltpu.VMEM((N_HEADS * tq, DSA_LATENT), BF16),
            pltpu.VMEM((CT_ROWS, N_HEADS * tq), F32),
            pltpu.VMEM((N_HEADS, 1, tq), F32),
        ],
        compiler_params=pltpu.CompilerParams(
            dimension_semantics=("parallel", "arbitrary"), vmem_limit_bytes=_vmem(48)),
        name="dsa",
    )(ub, ub, ub, c, ct, rowt, wuk_pad, wuv_t)


def _outproj_body(ya_ref, yb_ref, yc_ref, yd_ref, x_ref, w_ref, g_ref, o_ref):
    acc = jnp.dot(ya_ref[...], w_ref[0], preferred_element_type=F32)
    acc = acc + jnp.dot(yb_ref[...], w_ref[1], preferred_element_type=F32)
    acc = acc + jnp.dot(yc_ref[...], w_ref[2], preferred_element_type=F32)
    acc = acc + jnp.dot(yd_ref[...], w_ref[3], preferred_element_type=F32)
    ms = jnp.mean(acc * acc, axis=-1, keepdims=True)
    o_ref[...] = x_ref[...] + acc * lax.rsqrt(ms + EPS) * g_ref[...]


def _out_proj(ys, x2d, w4, g, *, layer, tm):
    m, d = x2d.shape
    gw = GROUP_WIDTH
    yspec = pl.BlockSpec((tm, gw), lambda i: (i, 0))
    return pl.pallas_call(
        _outproj_body,
        grid=(m // tm,),
        in_specs=[yspec, yspec, yspec, yspec,
                  pl.BlockSpec((tm, d), lambda i: (i, 0)),
                  pl.BlockSpec((None, 4, gw, d), lambda i: (layer, 0, 0, 0)),
                  pl.BlockSpec((1, d), lambda i: (0, 0))],
        out_specs=pl.BlockSpec((tm, d), lambda i: (i, 0)),
        out_shape=jax.ShapeDtypeStruct((m, d), F32),
        compiler_params=pltpu.CompilerParams(
            dimension_semantics=("parallel",), vmem_limit_bytes=_vmem(48)),
        name="out_proj",
    )(*ys, x2d, w4, g)


def _ffn_body(x_ref, gpre_ref, wg_ref, wu_ref, cw_ref, cb_ref, wd_ref, gpost_ref, o_ref,
              xn_ref, gbuf_ref, carry_ref, *, tiles_per_seq):
    i = pl.program_id(0)
    f = pl.program_id(1)
    nf = pl.num_programs(1)
    tm = x_ref.shape[0]

    @pl.when(f == 0)
    def _():
        x = x_ref[...]
        ms = jnp.mean(x * x, axis=-1, keepdims=True)
        xn_ref[...] = (x * lax.rsqrt(ms + EPS) * gpre_ref[...]).astype(BF16)
        o_ref[...] = jnp.zeros_like(o_ref)

    @pl.when(i % tiles_per_seq == 0)
    def _():
        carry_ref[f] = jnp.zeros(carry_ref.shape[1:], F32)

    xn = xn_ref[...]
    g = jnp.dot(xn, wg_ref[...], preferred_element_type=F32)
    u = jnp.dot(xn, wu_ref[...], preferred_element_type=F32)
    gbuf_ref[0:SUBLANES, :] = carry_ref[f]
    gbuf_ref[SUBLANES:SUBLANES + tm, :] = g
    conv = cb_ref[...] + cw_ref[FFN_CONV - 1:FFN_CONV, :] * g
    for k in range(FFN_CONV - 1):
        lag = SUBLANES - (FFN_CONV - 1 - k)
        conv = conv + cw_ref[k:k + 1, :] * gbuf_ref[lag:lag + tm, :]
    carry_ref[f] = g[tm - SUBLANES:tm, :]
    a = (_silu(conv) * u).astype(BF16)
    o_ref[...] += jnp.dot(a, wd_ref[...], preferred_element_type=F32)

    @pl.when(f == nf - 1)
    def _():
        y = o_ref[...]
        ms = jnp.mean(y * y, axis=-1, keepdims=True)
        o_ref[...] = x_ref[...] + y * lax.rsqrt(ms + EPS) * gpost_ref[...]


def _ffn(x2d, g_pre, w_gate, w_up, conv_w, conv_b, w_down, g_post, *, layer, tm, tf, tiles_per_seq):
    m, d = x2d.shape
    fdim = w_gate.shape[-1]
    nf = fdim // tf
    return pl.pallas_call(
        functools.partial(_ffn_body, tiles_per_seq=tiles_per_seq),
        grid=(m // tm, nf),
        in_specs=[
            pl.BlockSpec((tm, d), lambda i, f: (i, 0)),
            pl.BlockSpec((1, d), lambda i, f: (0, 0)),
            pl.BlockSpec((None, d, tf), lambda i, f: (layer, 0, f)),
            pl.BlockSpec((None, d, tf), lambda i, f: (layer, 0, f)),
            pl.BlockSpec((FFN_CONV, tf), lambda i, f: (0, f)),
            pl.BlockSpec((1, tf), lambda i, f: (0, f)),
            pl.BlockSpec((None, tf, d), lambda i, f: (layer, f, 0)),
            pl.BlockSpec((1, d), lambda i, f: (0, 0)),
        ],
        out_specs=pl.BlockSpec((tm, d), lambda i, f: (i, 0), pipeline_mode=pl.Buffered(1)),
        out_shape=jax.ShapeDtypeStruct((m, d), F32),
        scratch_shapes=[
            pltpu.VMEM((tm, d), BF16),
            pltpu.VMEM((SUBLANES + tm, tf), F32),
            pltpu.VMEM((nf, SUBLANES, tf), F32),
        ],
        compiler_params=pltpu.CompilerParams(
            dimension_semantics=("arbitrary", "arbitrary"), vmem_limit_bytes=_vmem(60)),
        name="ffn",
    )(x2d, g_pre, w_gate, w_up, conv_w, conv_b, w_down, g_post)


def _permute_w_in(w_in):
    gw = GROUP_WIDTH
    sizes = (gw, SSD_XBC, N_HEADS, gw, 3 * gw, N_HEADS, gw, DSA_LATENT, IDX_HEADS * IDX_DIM, IDX_DIM, IDX_HEADS)
    offs = [0]
    for s in sizes:
        offs.append(offs[-1] + s)
    z, xbc, dt, pool, qkv, fl, dq, dc, dqi, dki, dwi = (w_in[..., offs[k]:offs[k + 1]] for k in range(len(sizes)))
    qkv = jnp.concatenate([qkv[..., :gw] * (LOG2E * HEAD_DIM ** -0.5), qkv[..., gw:]], axis=-1)
    zeros = lambda n: jnp.zeros(w_in.shape[:-1] + (n,), w_in.dtype)
    small = jnp.concatenate([dt, fl, dwi, zeros(SM_A - SM_W - IDX_HEADS), dt, zeros(LANES - SM_A - N_HEADS)], axis=-1)
    a = jnp.concatenate([xbc, z, pool, dc, small, zeros(A_COLS - 2 * gw - SSD_XBC - DSA_LATENT - LANES)], axis=-1)
    bcols = jnp.concatenate([qkv, dq, dqi, dki, dki], axis=-1)
    bcols = jnp.concatenate([bcols, zeros(B_COLS - bcols.shape[-1])], axis=-1)
    return jnp.concatenate([a, bcols], axis=-1)


def _lane_vec(pieces):
    v = jnp.zeros((LANES,), F32)
    for off, val in pieces:
        v = v.at[off:off + val.shape[0]].set(val.astype(F32))
    return v[None, :]


def _pad_head_weights(w_uk, w_uv):
    h, r, d = w_uk.shape
    uk = jnp.zeros((h, 2 * d, r), F32)
    for i in range(h):
        o = d * (i % 2)
        uk = uk.at[i, o:o + d, :].set(w_uk[i].T)
    return uk.astype(BF16), jnp.swapaxes(w_uv, 1, 2).astype(BF16)


def _tile_sizes(b, lp):
    m = b * lp
    tm_proj = next(t for t in (1024, 512, 256, 128) if m % t == 0)
    tm_out = next(t for t in (512, 256, 128) if m % t == 0)
    tm_ffn = next(t for t in (1056, 528, 384, 320, 256, 128) if lp % t == 0)
    att_sub = 3 if lp % (3 * SEQ_TILE) == 0 else 1
    tn_proj = 1280
    tf_ffn = 512
    return tm_proj, tn_proj, tm_out, tm_ffn, tf_ffn, att_sub


def _layer(h, p, big, layer, *, topk):
    b, lp, d = h.shape
    m = b * lp
    tm_proj, tn_proj, tm_out, tm_ffn, tf_ffn, att_sub = _tile_sizes(b, lp)
    row = lambda v: v.astype(F32)[None, :]

    ua, ub = _in_proj(h.reshape(m, d), row(p["norm_mix_pre"]), big["w_in"], layer=layer, tm=tm_proj, tn=tn_proj)
    ua = ua.reshape(b, lp, A_COLS)
    ub = ub.reshape(b, lp, B_COLS)

    bias_vec = _lane_vec([(SM_DT, p["ssd_dt_bias"]), (SM_F, p["fox_f_bias"]), (SM_A, p["ssd_dt_bias"])])
    wscale = jnp.full((IDX_HEADS,), (IDX_HEADS ** -0.5) * (IDX_DIM ** -0.5), F32)
    mul_vec = _lane_vec([(SM_W, wscale), (SM_A, -jnp.exp(p["ssd_a_log"].astype(F32)))])
    col, rowt, c, ct, fkp = _prep(ua, bias_vec, mul_vec, row(p["dsa_kv_norm"]), sub=att_sub)

    dskip_full = jnp.repeat(p["ssd_d"].astype(F32), HEAD_DIM)[None, :]
    y_a = _ssd(ua, col, rowt, p["ssd_conv_w"].astype(F32), row(p["ssd_conv_b"]), dskip_full, row(p["ssd_norm"]),
               sub=att_sub)
    y_b = _pool(ua, p["pool_w"].astype(BF16), row(p["pool_scale"]), sub=att_sub)
    y_c = _fox(ub, fkp, sub=att_sub)
    wuk_pad, wuv_t = _pad_head_weights(p["dsa_w_uk"], p["dsa_w_uv"])
    y_d = _dsa(ub, c, ct, rowt, wuk_pad, wuv_t, topk=topk, sub=att_sub)

    ys = [y.reshape(m, GROUP_WIDTH) for y in (y_a, y_b, y_c, y_d)]
    x1 = _out_proj(ys, h.reshape(m, d), big["w_out"], row(p["norm_mix_post"]), layer=layer, tm=tm_out)
    x2 = _ffn(x1, row(p["norm_ffn_pre"]), big["ffn_w_gate"], big["ffn_w_up"],
              p["ffn_conv_w"].astype(F32), row(p["ffn_conv_b"]), big["ffn_w_down"],
              row(p["norm_ffn_post"]), layer=layer, tm=tm_ffn, tf=tf_ffn, tiles_per_seq=lp // tm_ffn)
    return x2.reshape(b, lp, d)


def _stack_big_weights(w_in, w_out, ffn_w_gate, ffn_w_up, ffn_w_down):
    depth, d, _ = w_out.shape
    return dict(w_in=_to_bf16(_permute_w_in(w_in)),
                w_out=_to_bf16(w_out).reshape(depth, 4, GROUP_WIDTH, d),
                ffn_w_gate=_to_bf16(ffn_w_gate), ffn_w_up=_to_bf16(ffn_w_up), ffn_w_down=_to_bf16(ffn_w_down))


def kernel(x, meta_tokens, norm_mix_pre, norm_mix_post, norm_ffn_pre, norm_ffn_post, w_in, ssd_conv_w, ssd_conv_b, ssd_dt_bias, ssd_a_log, ssd_d, ssd_norm, pool_w, pool_scale, fox_f_bias, dsa_kv_norm, dsa_w_uk, dsa_w_uv, w_out, ffn_w_gate, ffn_w_up, ffn_conv_w, ffn_conv_b, ffn_w_down):
    bsz, seq, d = x.shape
    n = N_META + seq
    lp = -(-n // SEQ_TILE) * SEQ_TILE
    topk = min(DSA_TOPK_MAX, seq // 4)
    meta = jnp.broadcast_to(meta_tokens.astype(x.dtype)[None], (bsz, N_META, d))
    h = jnp.concatenate([meta, x, jnp.zeros((bsz, lp - n, d), x.dtype)], axis=1)
    small = dict(norm_mix_pre=norm_mix_pre, norm_mix_post=norm_mix_post, norm_ffn_pre=norm_ffn_pre,
                 norm_ffn_post=norm_ffn_post, ssd_conv_w=ssd_conv_w, ssd_conv_b=ssd_conv_b,
                 ssd_dt_bias=ssd_dt_bias, ssd_a_log=ssd_a_log, ssd_d=ssd_d, ssd_norm=ssd_norm,
                 pool_w=pool_w, pool_scale=pool_scale, fox_f_bias=fox_f_bias, dsa_kv_norm=dsa_kv_norm,
                 dsa_w_uk=dsa_w_uk, dsa_w_uv=dsa_w_uv, ffn_conv_w=ffn_conv_w, ffn_conv_b=ffn_conv_b)
    big = _stack_big_weights(w_in, w_out, ffn_w_gate, ffn_w_up, ffn_w_down)
    for i in range(norm_mix_pre.shape[0]):
        h = _layer(h, {k: v[i] for k, v in small.items()}, big, i, topk=topk)
    return h[:, N_META:n]
```

```python
import functools

import jax
import jax.numpy as jnp
from jax import lax
from jax.experimental import pallas as pl
from jax.experimental.pallas import tpu as pltpu

F32 = jnp.float32
BF16 = jnp.bfloat16
I32 = jnp.int32
I16 = jnp.int16

EPS = 1e-6
N_META = 16
CHUNK = 64
HEAD_DIM = 64
GROUP_WIDTH = 512
N_HEADS = 8
SSD_GROUPS = 2
SSD_STATE = 128
SSD_CONV = 4
SSD_XBC = GROUP_WIDTH + 2 * SSD_GROUPS * SSD_STATE
POOL_WINDOWS = (2, 4, 8, 16)
DSA_LATENT = 128
IDX_HEADS = 4
IDX_DIM = 64
DSA_TOPK_MAX = 256
FFN_CONV = 3

LANES = 128
SEQ_TILE = 128
INT_MIN = -(2 ** 31)
HALF = 1 << 16
HALF_BIAS = 1 << 15
SUBLANES = 8
NEG_BIG = -1e30
LOG2E = 1.4426950408889634

A_COLS = 2560
B_COLS = 2560
A_XBC, A_Z, A_POOL, A_CKV, A_SMALL = 0, 1024, 1536, 2048, 2176
B_FQ, B_FK, B_FV, B_DQ, B_IQ, B_IK = 0, 512, 1024, 1536, 2048, 2304
SM_DT = 0
SM_F = 8
SM_W = 16
SM_A = 24
CT_ROWS = DSA_LATENT + 16
ACC_ROWS = 2 * HEAD_DIM + 16
TAIL = N_META


def _vmem(mb):
    return int(mb * 1024 * 1024)


def _softplus_parts(x):
    t = jnp.log1p(jnp.exp(-jnp.abs(x)))
    return jnp.maximum(x, 0.0) + t, jnp.minimum(x, 0.0) - t


def _silu(x):
    return x / (1.0 + jnp.exp(-x))


def _cast_body(x_ref, o_ref):
    o_ref[...] = x_ref[...].astype(BF16)


def _to_bf16(w, *, tr=256):
    shape = w.shape
    w2 = w.reshape(-1, shape[-1])
    r, c = w2.shape
    out = pl.pallas_call(
        _cast_body,
        grid=(r // tr,),
        in_specs=[pl.BlockSpec((tr, c), lambda i: (i, 0))],
        out_specs=pl.BlockSpec((tr, c), lambda i: (i, 0)),
        out_shape=jax.ShapeDtypeStruct((r, c), BF16),
        compiler_params=pltpu.CompilerParams(dimension_semantics=("parallel",), vmem_limit_bytes=_vmem(40)),
        name="to_bf16",
    )(w2)
    return out.reshape(shape)


def _inproj_body(x_ref, g_ref, w_ref, oa_ref, ob_ref, xn_ref, *, n_a):
    j = pl.program_id(1)

    @pl.when(j == 0)
    def _():
        x = x_ref[...]
        ms = jnp.mean(x * x, axis=-1, keepdims=True)
        xn_ref[...] = (x * lax.rsqrt(ms + EPS) * g_ref[...]).astype(BF16)

    acc = jnp.dot(xn_ref[...], w_ref[...], preferred_element_type=F32)

    @pl.when(j < n_a)
    def _():
        oa_ref[...] = acc

    @pl.when(j >= n_a)
    def _():
        ob_ref[...] = acc.astype(BF16)


def _in_proj(x2d, g, w_perm, *, layer, tm, tn):
    m, d = x2d.shape
    n_a, n_b = A_COLS // tn, B_COLS // tn
    return pl.pallas_call(
        functools.partial(_inproj_body, n_a=n_a),
        grid=(m // tm, n_a + n_b),
        in_specs=[
            pl.BlockSpec((tm, d), lambda i, j: (i, 0)),
            pl.BlockSpec((1, d), lambda i, j: (0, 0)),
            pl.BlockSpec((None, d, tn), lambda i, j: (layer, 0, j)),
        ],
        out_specs=[
            pl.BlockSpec((tm, tn), lambda i, j: (i, jnp.minimum(j, n_a - 1))),
            pl.BlockSpec((tm, tn), lambda i, j: (i, jnp.maximum(j - n_a, 0))),
        ],
        out_shape=[
            jax.ShapeDtypeStruct((m, A_COLS), F32),
            jax.ShapeDtypeStruct((m, B_COLS), BF16),
        ],
        scratch_shapes=[pltpu.VMEM((tm, d), BF16)],
        compiler_params=pltpu.CompilerParams(
            dimension_semantics=("parallel", "arbitrary"), vmem_limit_bytes=_vmem(58)),
        name="in_proj",
    )(x2d, g, w_perm)


def _prep_body(sm_ref, dc_ref, bias_ref, mul_ref, kvg_ref, col_ref, row_ref, c_ref, ct_ref, fkp_ref,
               carry_ref):
    t = pl.program_id(1)

    @pl.when(t == 0)
    def _():
        carry_ref[...] = jnp.zeros_like(carry_ref)

    tt = sm_ref.shape[1]
    sub = tt // SEQ_TILE
    s = sm_ref[0]
    lane = lax.broadcasted_iota(I32, (tt, LANES), 1)
    is_dt = lane < SM_F
    is_f = (lane >= SM_F) & (lane < SM_W)
    is_a = (lane >= SM_A) & (lane < SM_A + N_HEADS)
    sp, ls = _softplus_parts(s + bias_ref[...])
    v = jnp.where(is_dt, sp, jnp.where(is_f, ls, jnp.where(is_a, sp, s) * mul_ref[...]))
    ri = lax.broadcasted_iota(I32, (tt, tt), 0)
    ci = lax.broadcasted_iota(I32, (tt, tt), 1)
    tril = jnp.where(ci <= ri, 1.0, 0.0).astype(F32)
    run = jnp.dot(tril, v, precision=lax.Precision.HIGHEST, preferred_element_type=F32)
    rowi = lax.broadcasted_iota(I32, (tt, 1), 0)
    local = run
    for j in range(1, sub):
        local = jnp.where(rowi >= SEQ_TILE * j, run - run[SEQ_TILE * j - 1:SEQ_TILE * j, :], local)
    out = jnp.where(is_f, run + carry_ref[...], jnp.where(is_a, local, v))
    col_ref[0] = out
    out_t = out.T
    for j in range(sub):
        row_ref[0, j] = out_t[:, SEQ_TILE * j:SEQ_TILE * (j + 1)]
    carry_ref[...] = jnp.where(is_f[0:1], out[tt - 1:tt, :], 0.0)

    f0 = jnp.where(is_f, out, 0.0) * LOG2E
    hi = f0.astype(BF16).astype(F32)
    r1 = f0 - hi
    mid = r1.astype(BF16).astype(F32)
    lo = (r1 - mid).astype(BF16).astype(F32)
    fkp_ref[0] = (hi + pltpu.roll(mid, N_HEADS, axis=1) + pltpu.roll(lo, 2 * N_HEADS, axis=1)).astype(BF16)

    dc = dc_ref[0]
    ms = jnp.mean(dc * dc, axis=-1, keepdims=True)
    cn = dc * lax.rsqrt(ms + EPS) * kvg_ref[...]
    c_ref[0] = cn.astype(BF16)
    cn_t = cn.T
    ones = jnp.ones((CT_ROWS - DSA_LATENT, SEQ_TILE), F32)
    for j in range(sub):
        ct_ref[0, j] = jnp.concatenate([cn_t[:, SEQ_TILE * j:SEQ_TILE * (j + 1)], ones], axis=0).astype(BF16)


def _expand_heads(colv, base, lo_half):
    parts = []
    for p in range(N_HEADS // 2):
        a = colv[:, base + 2 * p:base + 2 * p + 1]
        b = colv[:, base + 2 * p + 1:base + 2 * p + 2]
        parts.append(jnp.where(lo_half, a, b))
    return jnp.concatenate(parts, axis=1)


def _ssd_body(xbc_ref, z_ref, col_ref, row_ref, cw_ref, cb_ref, dsk_ref, ng_ref, y_ref,
              xpad_ref, st_ref):
    c = pl.program_id(1)
    tt = xbc_ref.shape[1]
    ll = SEQ_TILE
    gw = GROUP_WIDTH
    ns = SSD_STATE
    hpg = N_HEADS // SSD_GROUPS
    gcols = hpg * HEAD_DIM

    @pl.when(c == 0)
    def _():
        xpad_ref[0:SUBLANES, :] = jnp.zeros((SUBLANES, SSD_XBC), F32)
        st_ref[...] = jnp.zeros_like(st_ref)

    x = xbc_ref[0]
    xpad_ref[SUBLANES:SUBLANES + tt, :] = x
    conv = cb_ref[...] + cw_ref[SSD_CONV - 1:SSD_CONV, :] * x
    for k in range(SSD_CONV - 1):
        lag = SUBLANES - (SSD_CONV - 1 - k)
        conv = conv + cw_ref[k:k + 1, :] * xpad_ref[lag:lag + tt, :]
    xpad_ref[0:SUBLANES, :] = x[tt - SUBLANES:tt, :]
    act_all = _silu(conv)
    gate_all = _silu(z_ref[0])
    lo_half = lax.broadcasted_iota(I32, (ll, LANES), 1) < HEAD_DIM
    tril = lax.broadcasted_iota(I32, (ll, ll), 1) <= lax.broadcasted_iota(I32, (ll, ll), 0)

    for j in range(tt // ll):
        rows = slice(ll * j, ll * (j + 1))
        act = act_all[rows]
        xs = act[:, 0:gw]
        bm = act[:, gw:gw + SSD_GROUPS * ns]
        cm = act[:, gw + SSD_GROUPS * ns:]
        colv = col_ref[0, rows, :]
        rowv = row_ref[0, j]
        dt_full = _expand_heads(colv, SM_DT, lo_half)
        acs_full = _expand_heads(colv, SM_A, lo_half)
        acs_last = acs_full[ll - 1:ll, :]
        dte_full = jnp.exp(acs_last - acs_full)
        dfs_full = jnp.exp(acs_full)
        xdt = xs * dt_full
        xdt_b = xdt.astype(BF16)
        xdte_b = (xdt * dte_full).astype(BF16)
        cm_b = cm.astype(BF16)

        ys = []
        for g in range(SSD_GROUPS):
            bg = bm[:, ns * g:ns * (g + 1)]
            bg_b = bg.astype(BF16)
            bgt_b = bg.T.astype(BF16)
            cg_b = cm_b[:, ns * g:ns * (g + 1)]
            cb = lax.dot_general(cg_b, bg_b, (((1,), (1,)), ((), ())), preferred_element_type=F32)
            sg = st_ref[g]
            yoff = (jnp.dot(cg_b, sg.astype(BF16), preferred_element_type=F32)
                    * dfs_full[:, gcols * g:gcols * (g + 1)])
            parts = []
            for pr in range(hpg // 2):
                xpair = xdt_b[:, gcols * g + LANES * pr:gcols * g + LANES * (pr + 1)]
                res = []
                for hh in range(2):
                    h = hpg * g + 2 * pr + hh
                    seg = colv[:, SM_A + h:SM_A + h + 1] - rowv[SM_A + h:SM_A + h + 1, :]
                    lm = jnp.exp(jnp.where(tril, seg, -jnp.inf))
                    res.append(jnp.dot((cb * lm).astype(BF16), xpair, preferred_element_type=F32))
                parts.append(jnp.where(lo_half, res[0], res[1]))
            ydiag = jnp.concatenate(parts, axis=1)
            decay = jnp.exp(acs_last[:, gcols * g:gcols * (g + 1)])
            st_ref[g] = decay * sg + jnp.dot(bgt_b, xdte_b[:, gcols * g:gcols * (g + 1)],
                                             preferred_element_type=F32)
            ys.append(ydiag + yoff)

        y = jnp.concatenate(ys, axis=1) + dsk_ref[...] * xs
        gz = y * gate_all[rows]
        outs = []
        for g in range(SSD_GROUPS):
            gg = gz[:, gcols * g:gcols * (g + 1)]
            outs.append(gg * lax.rsqrt(jnp.mean(gg * gg, axis=-1, keepdims=True) + EPS))
        y_ref[0, rows, :] = (jnp.concatenate(outs, axis=1) * ng_ref[...]).astype(BF16)


def _ssd(ua, col, row, conv_w, conv_b, dskip_full, norm_g, *, sub):
    b, lp, _ = ua.shape
    tt = sub * SEQ_TILE
    hpg = N_HEADS // SSD_GROUPS
    return pl.pallas_call(
        _ssd_body,
        grid=(b, lp // tt),
        in_specs=[
            pl.BlockSpec((1, tt, SSD_XBC), lambda i, c: (i, c, A_XBC // SSD_XBC)),
            pl.BlockSpec((1, tt, GROUP_WIDTH), lambda i, c: (i, c, A_Z // GROUP_WIDTH)),
            pl.BlockSpec((1, tt, LANES), lambda i, c: (i, c, 0)),
            pl.BlockSpec((1, sub, LANES, SEQ_TILE), lambda i, c: (i, c, 0, 0)),
            pl.BlockSpec((SSD_CONV, SSD_XBC), lambda i, c: (0, 0)),
            pl.BlockSpec((1, SSD_XBC), lambda i, c: (0, 0)),
            pl.BlockSpec((1, GROUP_WIDTH), lambda i, c: (0, 0)),
            pl.BlockSpec((1, GROUP_WIDTH), lambda i, c: (0, 0)),
        ],
        out_specs=pl.BlockSpec((1, tt, GROUP_WIDTH), lambda i, c: (i, c, 0)),
        out_shape=jax.ShapeDtypeStruct((b, lp, GROUP_WIDTH), BF16),
        scratch_shapes=[
            pltpu.VMEM((SUBLANES + tt, SSD_XBC), F32),
            pltpu.VMEM((SSD_GROUPS, SSD_STATE, hpg * HEAD_DIM), F32),
        ],
        compiler_params=pltpu.CompilerParams(dimension_semantics=("parallel", "arbitrary")),
        name="ssd",
    )(ua, ua, col, row, conv_w, conv_b, dskip_full, norm_g)


def _pool_body(u_ref, w_ref, sc_ref, y_ref, buf_ref):
    t = pl.program_id(1)
    tt = u_ref.shape[1]
    hist = max(POOL_WINDOWS)
    gd = GROUP_WIDTH // len(POOL_WINDOWS)

    @pl.when(t == 0)
    def _():
        buf_ref[0:hist, :] = jnp.zeros((hist, GROUP_WIDTH), F32)

    u = u_ref[0]
    buf_ref[hist:hist + tt, :] = u
    count = (t * tt + 1 + lax.broadcasted_iota(I32, (tt, 1), 0)).astype(F32)
    outs = []
    for gi, win in enumerate(POOL_WINDOWS):
        ug = u[:, gd * gi:gd * (gi + 1)]
        acc = ug
        for k in range(1, win):
            acc = acc + buf_ref[hist - k:hist - k + tt, gd * gi:gd * (gi + 1)]
        pooled = acc / jnp.minimum(count, float(win)) - ug
        outs.append(jnp.dot(pooled.astype(BF16), w_ref[gi], preferred_element_type=F32))
    y_ref[0] = (jnp.concatenate(outs, axis=1) * sc_ref[...]).astype(BF16)
    buf_ref[0:hist, :] = u[tt - hist:tt, :]


def _prep_pool_body(sm_ref, dc_ref, u_ref, bias_ref, mul_ref, kvg_ref, pw_ref, psc_ref,
                    col_ref, row_ref, c_ref, ct_ref, fkp_ref, yb_ref, carry_ref, buf_ref):
    _prep_body(sm_ref, dc_ref, bias_ref, mul_ref, kvg_ref, col_ref, row_ref, c_ref, ct_ref, fkp_ref, carry_ref)
    _pool_body(u_ref, pw_ref, psc_ref, yb_ref, buf_ref)


def _prep_pool(ua, bias_vec, mul_vec, kv_g, pool_w, pool_scale, *, sub):
    b, lp, _ = ua.shape
    tt = sub * SEQ_TILE
    nblk = lp // SEQ_TILE
    ng = len(POOL_WINDOWS)
    gd = GROUP_WIDTH // ng
    vec = pl.BlockSpec((1, LANES), lambda i, t: (0, 0))
    seq = pl.BlockSpec((1, tt, LANES), lambda i, t: (i, t, 0))
    return pl.pallas_call(
        _prep_pool_body,
        grid=(b, lp // tt),
        in_specs=[
            pl.BlockSpec((1, tt, LANES), lambda i, t: (i, t, A_SMALL // LANES)),
            pl.BlockSpec((1, tt, LANES), lambda i, t: (i, t, A_CKV // LANES)),
            pl.BlockSpec((1, tt, GROUP_WIDTH), lambda i, t: (i, t, A_POOL // GROUP_WIDTH)),
            vec, vec, vec,
            pl.BlockSpec((ng, gd, gd), lambda i, t: (0, 0, 0)),
            pl.BlockSpec((1, GROUP_WIDTH), lambda i, t: (0, 0)),
        ],
        out_specs=[
            seq,
            pl.BlockSpec((1, sub, LANES, SEQ_TILE), lambda i, t: (i, t, 0, 0)),
            seq,
            pl.BlockSpec((1, sub, CT_ROWS, SEQ_TILE), lambda i, t: (i, t, 0, 0)),
            seq,
            pl.BlockSpec((1, tt, GROUP_WIDTH), lambda i, t: (i, t, 0)),
        ],
        out_shape=[
            jax.ShapeDtypeStruct((b, lp, LANES), F32),
            jax.ShapeDtypeStruct((b, nblk, LANES, SEQ_TILE), F32),
            jax.ShapeDtypeStruct((b, lp, DSA_LATENT), BF16),
            jax.ShapeDtypeStruct((b, nblk, CT_ROWS, SEQ_TILE), BF16),
            jax.ShapeDtypeStruct((b, lp, LANES), BF16),
            jax.ShapeDtypeStruct((b, lp, GROUP_WIDTH), BF16),
        ],
        scratch_shapes=[pltpu.VMEM((1, LANES), F32), pltpu.VMEM((max(POOL_WINDOWS) + tt, GROUP_WIDTH), F32)],
        compiler_params=pltpu.CompilerParams(dimension_semantics=("parallel", "arbitrary")),
        name="prep_pool",
    )(ua, ua, ua, bias_vec, mul_vec, kv_g, pool_w, pool_scale)


def _fox_body(q_ref, k_ref, v_ref, fkp_ref, o_ref, qa_ref, vl_ref, acc_ref, m_ref):
    qi = pl.program_id(1)
    tq = q_ref.shape[1]
    tk = tq
    nt = (((1,), (1,)), ((), ()))
    npair = N_HEADS // 2
    lane = lax.broadcasted_iota(I32, (tq, LANES), 1)
    lo_half = lane < HEAD_DIM
    zero_b = jnp.zeros((tq, LANES), BF16)

    @pl.when(qi == 0)
    def _():
        nkt = vl_ref.shape[0]
        rsel = lax.broadcasted_iota(I32, (ACC_ROWS - 2 * HEAD_DIM, 2 * tk), 0)
        csel = lax.broadcasted_iota(I32, (ACC_ROWS - 2 * HEAD_DIM, 2 * tk), 1)
        ones_rows = jnp.where((rsel == 0) & (csel < tk) | (rsel == 1) & (csel >= tk), 1.0, 0.0).astype(BF16)
        zpad = jnp.zeros((HEAD_DIM, tk), BF16)

        def fill(kt, carry):
            ks = pl.multiple_of(kt * tk, tk)
            for pr in range(npair):
                vt = v_ref[0, pl.ds(ks, tk), LANES * pr:LANES * (pr + 1)].astype(F32).T.astype(BF16)
                top = jnp.concatenate([vt[:HEAD_DIM], zpad], axis=1)
                bot = jnp.concatenate([zpad, vt[HEAD_DIM:]], axis=1)
                vl_ref[kt, pr] = jnp.concatenate([top, bot, ones_rows], axis=0)
            return carry

        lax.fori_loop(0, nkt, fill, 0)

    for pr in range(npair):
        q2 = q_ref[0, :, LANES * pr:LANES * (pr + 1)]
        for hh in range(2):
            h = 2 * pr + hh
            qm = jnp.where(lo_half, q2, zero_b) if hh == 0 else jnp.where(lo_half, zero_b, q2)
            pick = (lane == SM_F + h) | (lane == SM_F + N_HEADS + h) | (lane == SM_F + 2 * N_HEADS + h)
            qa_ref[h] = jnp.concatenate([qm, jnp.where(pick, -1.0, 0.0).astype(BF16)], axis=1)
    acc_ref[...] = jnp.zeros_like(acc_ref)
    m_ref[...] = jnp.full(m_ref.shape, NEG_BIG, F32)
    causal_t = (lax.broadcasted_iota(I32, (tk, tq), 0) <= lax.broadcasted_iota(I32, (tk, tq), 1))
    tail_row = lax.broadcasted_iota(I32, (ACC_ROWS - 2 * HEAD_DIM, tq), 0)

    def step(kt, masked):
        ks = pl.multiple_of(kt * tk, tk)
        fkp = fkp_ref[0, pl.ds(ks, tk), :]
        for pr in range(npair):
            ka = jnp.concatenate([k_ref[0, pl.ds(ks, tk), LANES * pr:LANES * (pr + 1)], fkp], axis=1)
            ps, als = [], []
            for hh in range(2):
                h = 2 * pr + hh
                st = lax.dot_general(ka, qa_ref[h], nt, preferred_element_type=F32)
                if masked:
                    st = jnp.where(causal_t, st, -jnp.inf)
                m_old = m_ref[h]
                mn = jnp.maximum(m_old, jnp.max(st, axis=0, keepdims=True))
                ps.append(jnp.exp2(st - mn).astype(BF16))
                als.append(jnp.exp2(m_old - mn))
                m_ref[h] = mn
            upd = jnp.dot(vl_ref[kt, pr], jnp.concatenate(ps, axis=0), preferred_element_type=F32)
            scale = jnp.concatenate([jnp.broadcast_to(als[0], (HEAD_DIM, tq)),
                                     jnp.broadcast_to(als[1], (HEAD_DIM, tq)),
                                     jnp.where(tail_row == 0, als[0], als[1])], axis=0)
            acc_ref[pr] = scale * acc_ref[pr] + upd

    def body(kt, carry):
        step(kt, False)
        return carry

    lax.fori_loop(0, qi, body, 0)
    step(qi, True)
    for pr in range(npair):
        a = acc_ref[pr]
        o = jnp.concatenate([a[:HEAD_DIM] / a[2 * HEAD_DIM:2 * HEAD_DIM + 1],
                             a[HEAD_DIM:2 * HEAD_DIM] / a[2 * HEAD_DIM + 1:2 * HEAD_DIM + 2]], axis=0)
        o_ref[0, :, LANES * pr:LANES * (pr + 1)] = o.T.astype(BF16)


def _fox(ub, fkp, *, sub):
    b, lp, _ = ub.shape
    tq = sub * SEQ_TILE
    return pl.pallas_call(
        _fox_body,
        grid=(b, lp // tq),
        in_specs=[
            pl.BlockSpec((1, tq, GROUP_WIDTH), lambda i, q: (i, q, B_FQ // GROUP_WIDTH)),
            pl.BlockSpec((1, lp, GROUP_WIDTH), lambda i, q: (i, 0, B_FK // GROUP_WIDTH)),
            pl.BlockSpec((1, lp, GROUP_WIDTH), lambda i, q: (i, 0, B_FV // GROUP_WIDTH)),
            pl.BlockSpec((1, lp, LANES), lambda i, q: (i, 0, 0)),
        ],
        out_specs=pl.BlockSpec((1, tq, GROUP_WIDTH), lambda i, q: (i, q, 0)),
        out_shape=jax.ShapeDtypeStruct((b, lp, GROUP_WIDTH), BF16),
        scratch_shapes=[
            pltpu.VMEM((N_HEADS, tq, 2 * LANES), BF16),
            pltpu.VMEM((lp // tq, N_HEADS // 2, ACC_ROWS, 2 * tq), BF16),
            pltpu.VMEM((N_HEADS // 2, ACC_ROWS, tq), F32),
            pltpu.VMEM((N_HEADS, 1, tq), F32),
        ],
        compiler_params=pltpu.CompilerParams(
            dimension_semantics=("parallel", "arbitrary"), vmem_limit_bytes=_vmem(48)),
        name="fox",
    )(ub, ub, ub, fkp)


def _dsa_body(dq_ref, dqi_ref, ki_ref, c_ref, ct_ref, wr_ref, wuk_ref, wuvt_ref, o_ref,
              key_ref, bias_ref, hi_ref, lo_ref, qm_ref, qlat_ref, acc_ref, m_ref, *, topk, sub):
    qi = pl.program_id(1)
    tq = dq_ref.shape[1]
    tk = tq
    ngrp = qi + 1
    has_tail = qi + 1 < key_ref.shape[0]
    tblk = jnp.minimum(sub * (qi + 1), ct_ref.shape[1] - 1)
    t0 = pl.multiple_of(tblk * SEQ_TILE, SEQ_TILE)
    nt = (((1,), (1,)), ((), ()))
    lo_half = lax.broadcasted_iota(I32, (tq, LANES), 1) < IDX_DIM
    zero_b = jnp.zeros((tq, LANES), BF16)
    shift = CHUNK - N_META
    lg2 = CHUNK.bit_length() - 1
    kf = float(topk)

    def fold8(w):
        return jnp.sum(w.reshape(w.shape[0] // 8, 8, tq), axis=0)

    for h in range(IDX_HEADS):
        q2 = dqi_ref[0, :, LANES * (h // 2):LANES * (h // 2 + 1)]
        qm_ref[tq * h:tq * (h + 1), :] = (jnp.where(lo_half, q2, zero_b) if h % 2 == 0
                                           else jnp.where(lo_half, zero_b, q2))
    wrows = jnp.concatenate([wr_ref[0, j] for j in range(sub)], axis=1)

    def keys_of(kt):
        lg_all = lax.dot_general(kt, qm_ref[...], nt, preferred_element_type=F32)
        sc = None
        for h in range(IDX_HEADS):
            term = wrows[h:h + 1, :] * jnp.maximum(lg_all[:, tq * h:tq * (h + 1)], 0.0)
            sc = term if sc is None else sc + term
        bits = pltpu.bitcast(sc, I32)
        bits = jnp.where(bits == INT_MIN, 0, bits)
        return bits ^ ((bits >> 31) & 0x7FFFFFFF)

    def admissible(ks, rows):
        kcid = (ks + lax.broadcasted_iota(I32, (rows, 1), 0) + shift) >> lg2
        qcid = (qi * tq + lax.broadcasted_iota(I32, (1, tq), 1) + shift) >> lg2
        return kcid <= qcid

    def store_keys(g, key):
        key_ref[g] = key
        hi_ref[g] = (key >> 16).astype(I16)
        lo_ref[g] = ((key & (HALF - 1)) - HALF_BIAS).astype(I16)

    def score_step(g, carry):
        ks = pl.multiple_of(g * tk, tk)
        store_keys(g, keys_of(ki_ref[0, pl.ds(ks, tk), :]))
        return carry

    lax.fori_loop(0, qi, score_step, 0)
    q0 = pl.multiple_of(qi * tk, tk)
    store_keys(qi, jnp.where(admissible(q0, tk), keys_of(ki_ref[0, pl.ds(q0, tk), :]), INT_MIN))
    tkey = jnp.where(admissible(t0, TAIL) & has_tail, keys_of(ki_ref[0, pl.ds(t0, TAIL), :]), INT_MIN)

    one_b = jnp.ones((), BF16)
    zero_s = jnp.zeros((), BF16)

    def count_ge(x16, s16):
        w = jnp.where(x16 >= s16, one_b, zero_s)
        w3 = w.reshape(x16.shape[0] // 16, 16, tq)
        return functools.reduce(lambda a, b: a + b, [w3[r] for r in range(w3.shape[0])]).astype(F32)

    def bisect16(ref, tail16, need):
        def bit_step(i, u):
            uc = u | lax.shift_left(jnp.int32(1), 15 - i)
            s16 = (uc - HALF_BIAS).astype(I16)

            def cnt_step(g, acc):
                return acc + count_ge(ref[g], s16)

            acc = lax.fori_loop(0, ngrp, cnt_step, count_ge(tail16, s16))
            cnt = jnp.sum(acc, axis=0, keepdims=True)
            return jnp.where(cnt >= need, uc, u)

        return lax.fori_loop(0, 16, bit_step, jnp.zeros((1, tq), I32))

    thi = tkey >> 16
    tlo = (tkey & (HALF - 1)) - HALF_BIAS
    u_hi = bisect16(hi_ref, thi.astype(I16), kf)
    t_hi = u_hi - HALF_BIAS
    above16 = (t_hi + 1).astype(I16)

    def above_step(g, acc):
        return acc + count_ge(hi_ref[g], above16)

    n_above = jnp.sum(lax.fori_loop(0, ngrp, above_step, count_ge(thi.astype(I16), above16)), axis=0, keepdims=True)
    n_above = jnp.where(t_hi == HALF_BIAS - 1, 0.0, n_above)
    t_hi16 = t_hi.astype(I16)

    def narrow_step(g, carry):
        lo_ref[g] = jnp.where(hi_ref[g] == t_hi16, lo_ref[g], jnp.int16(-HALF_BIAS))
        return carry

    lax.fori_loop(0, ngrp, narrow_step, 0)
    tlo16 = jnp.where(thi == t_hi, tlo, -HALF_BIAS).astype(I16)
    u_lo = bisect16(lo_ref, tlo16, kf - n_above)
    thr = t_hi * HALF + u_lo

    def gt_step(g, acc):
        return acc + fold8(jnp.where(key_ref[g] > thr, 1.0, 0.0))

    ngt = jnp.sum(lax.fori_loop(0, ngrp, gt_step, fold8(jnp.where(tkey > thr, 1.0, 0.0))),
                  axis=0, keepdims=True)
    room = kf - ngt
    incl = jnp.where(lax.broadcasted_iota(I32, (tk, tk), 1) <= lax.broadcasted_iota(I32, (tk, tk), 0),
                     1.0, 0.0).astype(BF16)
    incl_tail = jnp.where(lax.broadcasted_iota(I32, (TAIL, TAIL), 1) <= lax.broadcasted_iota(I32, (TAIL, TAIL), 0),
                          1.0, 0.0).astype(BF16)

    room = jnp.where(thr == INT_MIN, 0.0, room)

    def bias_of(key, seen):
        rows = key.shape[0]
        eq = key == thr
        eqf = jnp.where(eq, 1.0, 0.0)
        rank = jnp.dot(incl if rows == tk else incl_tail, eqf.astype(BF16), preferred_element_type=F32) + seen
        tie = jnp.where(eq, jnp.where(rank <= room, 0.0, -jnp.inf), -jnp.inf)
        return jnp.where(key > thr, 0.0, tie), seen + jnp.sum(fold8(eqf), axis=0, keepdims=True)

    def mask_step(g, seen):
        bias_ref[g], seen = bias_of(key_ref[g], seen)
        return seen

    seen = lax.fori_loop(0, ngrp, mask_step, jnp.zeros((1, tq), F32))
    tbias, _ = bias_of(tkey, seen)

    for h in range(N_HEADS):
        dq2 = dq_ref[0, :, LANES * (h // 2):LANES * (h // 2 + 1)]
        ql = jnp.dot(dq2, wuk_ref[h], preferred_element_type=F32) * (LOG2E * HEAD_DIM ** -0.5)
        qlat_ref[tq * h:tq * (h + 1), :] = ql.astype(BF16)
    acc_ref[...] = jnp.zeros_like(acc_ref)
    m_ref[...] = jnp.full(m_ref.shape, NEG_BIG, F32)

    def attend(ck, cx, bias):
        st_all = lax.dot_general(ck, qlat_ref[...], nt, preferred_element_type=F32)
        ps, als = [], []
        for h in range(N_HEADS):
            st = st_all[:, tq * h:tq * (h + 1)] + bias
            m_old = m_ref[h]
            mn = jnp.maximum(m_old, jnp.max(st, axis=0, keepdims=True))
            ps.append(jnp.exp2(st - mn).astype(BF16))
            als.append(jnp.exp2(m_old - mn))
            m_ref[h] = mn
        upd = jnp.dot(cx, jnp.concatenate(ps, axis=1), preferred_element_type=F32)
        acc_ref[...] = jnp.concatenate(als, axis=1) * acc_ref[...] + upd

    def att_step(g, carry):
        ks = pl.multiple_of(g * tk, tk)
        cx = jnp.concatenate([ct_ref[0, g * sub + j] for j in range(sub)], axis=1)
        attend(c_ref[0, pl.ds(ks, tk), :], cx, bias_ref[g])
        return carry

    lax.fori_loop(0, ngrp, att_step, 0)
    attend(c_ref[0, pl.ds(t0, TAIL), :], ct_ref[0, tblk][:, :TAIL], tbias)

    for pr in range(N_HEADS // 2):
        outs = []
        for hh in range(2):
            a = acc_ref[:, tq * (2 * pr + hh):tq * (2 * pr + hh + 1)]
            olat = (a[:DSA_LATENT] / a[DSA_LATENT:DSA_LATENT + 1]).astype(BF16)
            outs.append(jnp.dot(wuvt_ref[2 * pr + hh], olat, preferred_element_type=F32))
        o_ref[0, :, LANES * pr:LANES * (pr + 1)] = jnp.concatenate(outs, axis=0).T.astype(BF16)


def _dsa(ub, c, ct, rowt, wuk_pad, wuv_t, *, topk, sub):
    b, lp, _ = ub.shape
    tq = sub * SEQ_TILE
    ngrp = lp // tq
    nblk = lp // SEQ_TILE
    return pl.pallas_call(
        functools.partial(_dsa_body, topk=topk, sub=sub),
        grid=(b, ngrp),
        in_specs=[
            pl.BlockSpec((1, tq, GROUP_WIDTH), lambda i, q: (i, q, B_DQ // GROUP_WIDTH)),
            pl.BlockSpec((1, tq, IDX_HEADS * IDX_DIM), lambda i, q: (i, q, B_IQ // (IDX_HEADS * IDX_DIM))),
            pl.BlockSpec((1, lp, LANES), lambda i, q: (i, 0, B_IK // LANES)),
            pl.BlockSpec((1, lp, DSA_LATENT), lambda i, q: (i, 0, 0)),
            pl.BlockSpec((1, nblk, CT_ROWS, SEQ_TILE), lambda i, q: (i, 0, 0, 0)),
            pl.BlockSpec((1, sub, N_HEADS, SEQ_TILE), lambda i, q: (i, q, SM_W // N_HEADS, 0)),
            pl.BlockSpec((N_HEADS, LANES, DSA_LATENT), lambda i, q: (0, 0, 0)),
            pl.BlockSpec((N_HEADS, HEAD_DIM, DSA_LATENT), lambda i, q: (0, 0, 0)),
        ],
        out_specs=pl.BlockSpec((1, tq, GROUP_WIDTH), lambda i, q: (i, q, 0)),
        out_shape=jax.ShapeDtypeStruct((b, lp, GROUP_WIDTH), BF16),
        scratch_shapes=[
            pltpu.VMEM((ngrp, tq, tq), I32),
            pltpu.VMEM((ngrp, tq, tq), F32),
            pltpu.VMEM((ngrp, tq, tq), I16),
            pltpu.VMEM((ngrp, tq, tq), I16),
            pltpu.VMEM((IDX_HEADS * tq, LANES), BF16),
            pltpu.VMEM((N_HEADS * tq, DSA_LATENT), BF16),
            pltpu.VMEM((CT_ROWS, N_HEADS * tq), F32),
            pltpu.VMEM((N_HEADS, 1, tq), F32),
        ],
        compiler_params=pltpu.CompilerParams(
            dimension_semantics=("parallel", "arbitrary"), vmem_limit_bytes=_vmem(48)),
        name="dsa",
    )(ub, ub, ub, c, ct, rowt, wuk_pad, wuv_t)


def _outproj_body(ya_ref, yb_ref, yc_ref, yd_ref, x_ref, w_ref, g_ref, o_ref):
    acc = jnp.dot(ya_ref[...], w_ref[0], preferred_element_type=F32)
    acc = acc + jnp.dot(yb_ref[...], w_ref[1], preferred_element_type=F32)
    acc = acc + jnp.dot(yc_ref[...], w_ref[2], preferred_element_type=F32)
    acc = acc + jnp.dot(yd_ref[...], w_ref[3], preferred_element_type=F32)
    ms = jnp.mean(acc * acc, axis=-1, keepdims=True)
    o_ref[...] = x_ref[...] + acc * lax.rsqrt(ms + EPS) * g_ref[...]


def _out_proj(ys, x2d, w4, g, *, layer, tm):
    m, d = x2d.shape
    gw = GROUP_WIDTH
    yspec = pl.BlockSpec((tm, gw), lambda i: (i, 0))
    return pl.pallas_call(
        _outproj_body,
        grid=(m // tm,),
        in_specs=[yspec, yspec, yspec, yspec,
                  pl.BlockSpec((tm, d), lambda i: (i, 0)),
                  pl.BlockSpec((None, 4, gw, d), lambda i: (layer, 0, 0, 0)),
                  pl.BlockSpec((1, d), lambda i: (0, 0))],
        out_specs=pl.BlockSpec((tm, d), lambda i: (i, 0)),
        out_shape=jax.ShapeDtypeStruct((m, d), F32),
        compiler_params=pltpu.CompilerParams(
            dimension_semantics=("parallel",), vmem_limit_bytes=_vmem(48)),
        name="out_proj",
    )(*ys, x2d, w4, g)


def _ffn_body(x_ref, gpre_ref, wg_ref, wu_ref, cw_ref, cb_ref, wd_ref, gpost_ref, o_ref,
              xn_ref, gbuf_ref, carry_ref, *, tiles_per_seq):
    i = pl.program_id(0)
    f = pl.program_id(1)
    nf = pl.num_programs(1)
    tm = x_ref.shape[0]

    @pl.when(f == 0)
    def _():
        x = x_ref[...]
        ms = jnp.mean(x * x, axis=-1, keepdims=True)
        xn_ref[...] = (x * lax.rsqrt(ms + EPS) * gpre_ref[...]).astype(BF16)
        o_ref[...] = jnp.zeros_like(o_ref)

    @pl.when(i % tiles_per_seq == 0)
    def _():
        carry_ref[f] = jnp.zeros(carry_ref.shape[1:], F32)

    xn = xn_ref[...]
    g = jnp.dot(xn, wg_ref[...], preferred_element_type=F32)
    u = jnp.dot(xn, wu_ref[...], preferred_element_type=F32)
    gbuf_ref[0:SUBLANES, :] = carry_ref[f]
    gbuf_ref[SUBLANES:SUBLANES + tm, :] = g
    conv = cb_ref[...] + cw_ref[FFN_CONV - 1:FFN_CONV, :] * g
    for k in range(FFN_CONV - 1):
        lag = SUBLANES - (FFN_CONV - 1 - k)
        conv = conv + cw_ref[k:k + 1, :] * gbuf_ref[lag:lag + tm, :]
    carry_ref[f] = g[tm - SUBLANES:tm, :]
    a = (_silu(conv) * u).astype(BF16)
    o_ref[...] += jnp.dot(a, wd_ref[...], preferred_element_type=F32)

    @pl.when(f == nf - 1)
    def _():
        y = o_ref[...]
        ms = jnp.mean(y * y, axis=-1, keepdims=True)
        o_ref[...] = x_ref[...] + y * lax.rsqrt(ms + EPS) * gpost_ref[...]


def _ffn(x2d, g_pre, w_gate, w_up, conv_w, conv_b, w_down, g_post, *, layer, tm, tf, tiles_per_seq):
    m, d = x2d.shape
    fdim = w_gate.shape[-1]
    nf = fdim // tf
    return pl.pallas_call(
        functools.partial(_ffn_body, tiles_per_seq=tiles_per_seq),
        grid=(m // tm, nf),
        in_specs=[
            pl.BlockSpec((tm, d), lambda i, f: (i, 0)),
            pl.BlockSpec((1, d), lambda i, f: (0, 0)),
            pl.BlockSpec((None, d, tf), lambda i, f: (layer, 0, f)),
            pl.BlockSpec((None, d, tf), lambda i, f: (layer, 0, f)),
            pl.BlockSpec((FFN_CONV, tf), lambda i, f: (0, f)),
            pl.BlockSpec((1, tf), lambda i, f: (0, f)),
            pl.BlockSpec((None, tf, d), lambda i, f: (layer, f, 0)),
            pl.BlockSpec((1, d), lambda i, f: (0, 0)),
        ],
        out_specs=pl.BlockSpec((tm, d), lambda i, f: (i, 0), pipeline_mode=pl.Buffered(1)),
        out_shape=jax.ShapeDtypeStruct((m, d), F32),
        scratch_shapes=[
            pltpu.VMEM((tm, d), BF16),
            pltpu.VMEM((SUBLANES + tm, tf), F32),
            pltpu.VMEM((nf, SUBLANES, tf), F32),
        ],
        compiler_params=pltpu.CompilerParams(
            dimension_semantics=("arbitrary", "arbitrary"), vmem_limit_bytes=_vmem(60)),
        name="ffn",
    )(x2d, g_pre, w_gate, w_up, conv_w, conv_b, w_down, g_post)


def _permute_w_in(w_in):
    gw = GROUP_WIDTH
    sizes = (gw, SSD_XBC, N_HEADS, gw, 3 * gw, N_HEADS, gw, DSA_LATENT, IDX_HEADS * IDX_DIM, IDX_DIM, IDX_HEADS)
    offs = [0]
    for s in sizes:
        offs.append(offs[-1] + s)
    z, xbc, dt, pool, qkv, fl, dq, dc, dqi, dki, dwi = (w_in[..., offs[k]:offs[k + 1]] for k in range(len(sizes)))
    qkv = jnp.concatenate([qkv[..., :gw] * (LOG2E * HEAD_DIM ** -0.5), qkv[..., gw:]], axis=-1)
    zeros = lambda n: jnp.zeros(w_in.shape[:-1] + (n,), w_in.dtype)
    small = jnp.concatenate([dt, fl, dwi, zeros(SM_A - SM_W - IDX_HEADS), dt, zeros(LANES - SM_A - N_HEADS)], axis=-1)
    a = jnp.concatenate([xbc, z, pool, dc, small, zeros(A_COLS - 2 * gw - SSD_XBC - DSA_LATENT - LANES)], axis=-1)
    bcols = jnp.concatenate([qkv, dq, dqi, dki, dki], axis=-1)
    bcols = jnp.concatenate([bcols, zeros(B_COLS - bcols.shape[-1])], axis=-1)
    return jnp.concatenate([a, bcols], axis=-1)


def _lane_vec(pieces):
    v = jnp.zeros((LANES,), F32)
    for off, val in pieces:
        v = v.at[off:off + val.shape[0]].set(val.astype(F32))
    return v[None, :]


def _pad_head_weights(w_uk, w_uv):
    h, r, d = w_uk.shape
    uk = jnp.zeros((h, 2 * d, r), F32)
    for i in range(h):
        o = d * (i % 2)
        uk = uk.at[i, o:o + d, :].set(w_uk[i].T)
    return uk.astype(BF16), jnp.swapaxes(w_uv, 1, 2).astype(BF16)


def _tile_sizes(b, lp):
    m = b * lp
    tm_proj = next(t for t in (1024, 512, 256, 128) if m % t == 0)
    tm_out = next(t for t in (512, 256, 128) if m % t == 0)
    tm_ffn = next(t for t in (1056, 528, 384, 320, 256, 128) if lp % t == 0)
    att_sub = 3 if lp % (3 * SEQ_TILE) == 0 else 1
    tn_proj = 1280
    tf_ffn = 512
    return tm_proj, tn_proj, tm_out, tm_ffn, tf_ffn, att_sub


def _layer(h, p, big, layer, *, topk):
    b, lp, d = h.shape
    m = b * lp
    tm_proj, tn_proj, tm_out, tm_ffn, tf_ffn, att_sub = _tile_sizes(b, lp)
    row = lambda v: v.astype(F32)[None, :]

    ua, ub = _in_proj(h.reshape(m, d), row(p["norm_mix_pre"]), big["w_in"], layer=layer, tm=tm_proj, tn=tn_proj)
    ua = ua.reshape(b, lp, A_COLS)
    ub = ub.reshape(b, lp, B_COLS)

    bias_vec = _lane_vec([(SM_DT, p["ssd_dt_bias"]), (SM_F, p["fox_f_bias"]), (SM_A, p["ssd_dt_bias"])])
    wscale = jnp.full((IDX_HEADS,), (IDX_HEADS ** -0.5) * (IDX_DIM ** -0.5), F32)
    mul_vec = _lane_vec([(SM_W, wscale), (SM_A, -jnp.exp(p["ssd_a_log"].astype(F32)))])
    col, rowt, c, ct, fkp, y_b = _prep_pool(ua, bias_vec, mul_vec, row(p["dsa_kv_norm"]),
                                            p["pool_w"].astype(BF16), row(p["pool_scale"]), sub=att_sub)

    dskip_full = jnp.repeat(p["ssd_d"].astype(F32), HEAD_DIM)[None, :]
    y_a = _ssd(ua, col, rowt, p["ssd_conv_w"].astype(F32), row(p["ssd_conv_b"]), dskip_full, row(p["ssd_norm"]),
               sub=att_sub)
    y_c = _fox(ub, fkp, sub=att_sub)
    wuk_pad, wuv_t = _pad_head_weights(p["dsa_w_uk"], p["dsa_w_uv"])
    y_d = _dsa(ub, c, ct, rowt, wuk_pad, wuv_t, topk=topk, sub=att_sub)

    ys = [y.reshape(m, GROUP_WIDTH) for y in (y_a, y_b, y_c, y_d)]
    x1 = _out_proj(ys, h.reshape(m, d), big["w_out"], row(p["norm_mix_post"]), layer=layer, tm=tm_out)
    x2 = _ffn(x1, row(p["norm_ffn_pre"]), big["ffn_w_gate"], big["ffn_w_up"],
              p["ffn_conv_w"].astype(F32), row(p["ffn_conv_b"]), big["ffn_w_down"],
              row(p["norm_ffn_post"]), layer=layer, tm=tm_ffn, tf=tf_ffn, tiles_per_seq=lp // tm_ffn)
    return x2.reshape(b, lp, d)


def _stack_big_weights(w_in, w_out, ffn_w_gate, ffn_w_up, ffn_w_down):
    depth, d, _ = w_out.shape
    return dict(w_in=_to_bf16(_permute_w_in(w_in)),
                w_out=_to_bf16(w_out).reshape(depth, 4, GROUP_WIDTH, d),
                ffn_w_gate=_to_bf16(ffn_w_gate), ffn_w_up=_to_bf16(ffn_w_up), ffn_w_down=_to_bf16(ffn_w_down))


def kernel(x, meta_tokens, norm_mix_pre, norm_mix_post, norm_ffn_pre, norm_ffn_post, w_in, ssd_conv_w, ssd_conv_b, ssd_dt_bias, ssd_a_log, ssd_d, ssd_norm, pool_w, pool_scale, fox_f_bias, dsa_kv_norm, dsa_w_uk, dsa_w_uv, w_out, ffn_w_gate, ffn_w_up, ffn_conv_w, ffn_conv_b, ffn_w_down):
    bsz, seq, d = x.shape
    n = N_META + seq
    lp = -(-n // SEQ_TILE) * SEQ_TILE
    topk = min(DSA_TOPK_MAX, seq // 4)
    meta = jnp.broadcast_to(meta_tokens.astype(x.dtype)[None], (bsz, N_META, d))
    h = jnp.concatenate([meta, x, jnp.zeros((bsz, lp - n, d), x.dtype)], axis=1)
    small = dict(norm_mix_pre=norm_mix_pre, norm_mix_post=norm_mix_post, norm_ffn_pre=norm_ffn_pre,
                 norm_ffn_post=norm_ffn_post, ssd_conv_w=ssd_conv_w, ssd_conv_b=ssd_conv_b,
                 ssd_dt_bias=ssd_dt_bias, ssd_a_log=ssd_a_log, ssd_d=ssd_d, ssd_norm=ssd_norm,
                 pool_w=pool_w, pool_scale=pool_scale, fox_f_bias=fox_f_bias, dsa_kv_norm=dsa_kv_norm,
                 dsa_w_uk=dsa_w_uk, dsa_w_uv=dsa_w_uv, ffn_conv_w=ffn_conv_w, ffn_conv_b=ffn_conv_b)
    big = _stack_big_weights(w_in, w_out, ffn_w_gate, ffn_w_up, ffn_w_down)
    for i in range(norm_mix_pre.shape[0]):
        h = _layer(h, {k: v[i] for k, v in small.items()}, big, i, topk=topk)
    return h[:, N_META:n]
```

```python
import functools

import jax
import jax.numpy as jnp
from jax import lax
from jax.experimental import pallas as pl
from jax.experimental.pallas import tpu as pltpu

F32 = jnp.float32
BF16 = jnp.bfloat16
I32 = jnp.int32
I16 = jnp.int16

EPS = 1e-6
N_META = 16
CHUNK = 64
HEAD_DIM = 64
GROUP_WIDTH = 512
N_HEADS = 8
SSD_GROUPS = 2
SSD_STATE = 128
SSD_CONV = 4
SSD_XBC = GROUP_WIDTH + 2 * SSD_GROUPS * SSD_STATE
POOL_WINDOWS = (2, 4, 8, 16)
DSA_LATENT = 128
IDX_HEADS = 4
IDX_DIM = 64
DSA_TOPK_MAX = 256
FFN_CONV = 3

LANES = 128
SEQ_TILE = 128
INT_MIN = -(2 ** 31)
HALF = 1 << 16
HALF_BIAS = 1 << 15
SUBLANES = 8
NEG_BIG = -1e30
LOG2E = 1.4426950408889634

A_COLS = 2560
B_COLS = 2560
A_XBC, A_Z, A_POOL, A_CKV, A_SMALL = 0, 1024, 1536, 2048, 2176
B_FQ, B_FK, B_FV, B_DQ, B_IQ, B_IK = 0, 512, 1024, 1536, 2048, 2304
SM_DT = 0
SM_F = 8
SM_W = 16
SM_A = 24
CT_ROWS = DSA_LATENT + 16
ACC_ROWS = 2 * HEAD_DIM + 16
TAIL = N_META


def _vmem(mb):
    return int(mb * 1024 * 1024)


def _softplus_parts(x):
    t = jnp.log1p(jnp.exp(-jnp.abs(x)))
    return jnp.maximum(x, 0.0) + t, jnp.minimum(x, 0.0) - t


def _silu(x):
    return x / (1.0 + jnp.exp(-x))


def _cast_body(x_ref, o_ref):
    o_ref[...] = x_ref[...].astype(BF16)


def _to_bf16(w, *, tr=256):
    shape = w.shape
    w2 = w.reshape(-1, shape[-1])
    r, c = w2.shape
    out = pl.pallas_call(
        _cast_body,
        grid=(r // tr,),
        in_specs=[pl.BlockSpec((tr, c), lambda i: (i, 0))],
        out_specs=pl.BlockSpec((tr, c), lambda i: (i, 0)),
        out_shape=jax.ShapeDtypeStruct((r, c), BF16),
        compiler_params=pltpu.CompilerParams(dimension_semantics=("parallel",), vmem_limit_bytes=_vmem(40)),
        name="to_bf16",
    )(w2)
    return out.reshape(shape)


def _inproj_body(x_ref, g_ref, w_ref, oa_ref, ob_ref, xn_ref, *, n_a):
    j = pl.program_id(1)

    @pl.when(j == 0)
    def _():
        x = x_ref[...]
        ms = jnp.mean(x * x, axis=-1, keepdims=True)
        xn_ref[...] = (x * lax.rsqrt(ms + EPS) * g_ref[...]).astype(BF16)

    acc = jnp.dot(xn_ref[...], w_ref[...], preferred_element_type=F32)

    @pl.when(j < n_a)
    def _():
        oa_ref[...] = acc

    @pl.when(j >= n_a)
    def _():
        ob_ref[...] = acc.astype(BF16)


def _in_proj(x2d, g, w_perm, *, layer, tm, tn):
    m, d = x2d.shape
    n_a, n_b = A_COLS // tn, B_COLS // tn
    return pl.pallas_call(
        functools.partial(_inproj_body, n_a=n_a),
        grid=(m // tm, n_a + n_b),
        in_specs=[
            pl.BlockSpec((tm, d), lambda i, j: (i, 0)),
            pl.BlockSpec((1, d), lambda i, j: (0, 0)),
            pl.BlockSpec((None, d, tn), lambda i, j: (layer, 0, j)),
        ],
        out_specs=[
            pl.BlockSpec((tm, tn), lambda i, j: (i, jnp.minimum(j, n_a - 1))),
            pl.BlockSpec((tm, tn), lambda i, j: (i, jnp.maximum(j - n_a, 0))),
        ],
        out_shape=[
            jax.ShapeDtypeStruct((m, A_COLS), F32),
            jax.ShapeDtypeStruct((m, B_COLS), BF16),
        ],
        scratch_shapes=[pltpu.VMEM((tm, d), BF16)],
        compiler_params=pltpu.CompilerParams(
            dimension_semantics=("parallel", "arbitrary"), vmem_limit_bytes=_vmem(58)),
        name="in_proj",
    )(x2d, g, w_perm)


def _prep_body(sm_ref, dc_ref, bias_ref, mul_ref, kvg_ref, col_ref, row_ref, c_ref, ct_ref, fkp_ref,
               carry_ref):
    t = pl.program_id(1)

    @pl.when(t == 0)
    def _():
        carry_ref[...] = jnp.zeros_like(carry_ref)

    tt = sm_ref.shape[1]
    sub = tt // SEQ_TILE
    s = sm_ref[0]
    lane = lax.broadcasted_iota(I32, (tt, LANES), 1)
    is_dt = lane < SM_F
    is_f = (lane >= SM_F) & (lane < SM_W)
    is_a = (lane >= SM_A) & (lane < SM_A + N_HEADS)
    sp, ls = _softplus_parts(s + bias_ref[...])
    v = jnp.where(is_dt, sp, jnp.where(is_f, ls, jnp.where(is_a, sp, s) * mul_ref[...]))
    ri = lax.broadcasted_iota(I32, (tt, tt), 0)
    ci = lax.broadcasted_iota(I32, (tt, tt), 1)
    tril = jnp.where(ci <= ri, 1.0, 0.0).astype(F32)
    run = jnp.dot(tril, v, precision=lax.Precision.HIGHEST, preferred_element_type=F32)
    rowi = lax.broadcasted_iota(I32, (tt, 1), 0)
    local = run
    for j in range(1, sub):
        local = jnp.where(rowi >= SEQ_TILE * j, run - run[SEQ_TILE * j - 1:SEQ_TILE * j, :], local)
    out = jnp.where(is_f, run + carry_ref[...], jnp.where(is_a, local, v))
    col_ref[0] = out
    out_t = out.T
    for j in range(sub):
        row_ref[0, j] = out_t[:, SEQ_TILE * j:SEQ_TILE * (j + 1)]
    carry_ref[...] = jnp.where(is_f[0:1], out[tt - 1:tt, :], 0.0)

    f0 = jnp.where(is_f, out, 0.0) * LOG2E
    hi = f0.astype(BF16).astype(F32)
    r1 = f0 - hi
    mid = r1.astype(BF16).astype(F32)
    lo = (r1 - mid).astype(BF16).astype(F32)
    fkp_ref[0] = (hi + pltpu.roll(mid, N_HEADS, axis=1) + pltpu.roll(lo, 2 * N_HEADS, axis=1)).astype(BF16)

    dc = dc_ref[0]
    ms = jnp.mean(dc * dc, axis=-1, keepdims=True)
    cn = dc * lax.rsqrt(ms + EPS) * kvg_ref[...]
    c_ref[0] = cn.astype(BF16)
    cn_t = cn.T
    ones = jnp.ones((CT_ROWS - DSA_LATENT, SEQ_TILE), F32)
    for j in range(sub):
        ct_ref[0, j] = jnp.concatenate([cn_t[:, SEQ_TILE * j:SEQ_TILE * (j + 1)], ones], axis=0).astype(BF16)


def _expand_heads(colv, base, lo_half):
    parts = []
    for p in range(N_HEADS // 2):
        a = colv[:, base + 2 * p:base + 2 * p + 1]
        b = colv[:, base + 2 * p + 1:base + 2 * p + 2]
        parts.append(jnp.where(lo_half, a, b))
    return jnp.concatenate(parts, axis=1)


def _ssd_body(xbc_ref, z_ref, col_ref, row_ref, cw_ref, cb_ref, dsk_ref, ng_ref, y_ref,
              xpad_ref, st_ref):
    c = pl.program_id(1)
    tt = xbc_ref.shape[1]
    ll = SEQ_TILE
    gw = GROUP_WIDTH
    ns = SSD_STATE
    hpg = N_HEADS // SSD_GROUPS
    gcols = hpg * HEAD_DIM

    @pl.when(c == 0)
    def _():
        xpad_ref[0:SUBLANES, :] = jnp.zeros((SUBLANES, SSD_XBC), F32)
        st_ref[...] = jnp.zeros_like(st_ref)

    x = xbc_ref[0]
    xpad_ref[SUBLANES:SUBLANES + tt, :] = x
    conv = cb_ref[...] + cw_ref[SSD_CONV - 1:SSD_CONV, :] * x
    for k in range(SSD_CONV - 1):
        lag = SUBLANES - (SSD_CONV - 1 - k)
        conv = conv + cw_ref[k:k + 1, :] * xpad_ref[lag:lag + tt, :]
    xpad_ref[0:SUBLANES, :] = x[tt - SUBLANES:tt, :]
    act_all = _silu(conv)
    gate_all = _silu(z_ref[0])
    lo_half = lax.broadcasted_iota(I32, (ll, LANES), 1) < HEAD_DIM
    tril = lax.broadcasted_iota(I32, (ll, ll), 1) <= lax.broadcasted_iota(I32, (ll, ll), 0)

    for j in range(tt // ll):
        rows = slice(ll * j, ll * (j + 1))
        act = act_all[rows]
        xs = act[:, 0:gw]
        bm = act[:, gw:gw + SSD_GROUPS * ns]
        cm = act[:, gw + SSD_GROUPS * ns:]
        colv = col_ref[0, rows, :]
        rowv = row_ref[0, j]
        dt_full = _expand_heads(colv, SM_DT, lo_half)
        acs_full = _expand_heads(colv, SM_A, lo_half)
        acs_last = acs_full[ll - 1:ll, :]
        dte_full = jnp.exp(acs_last - acs_full)
        dfs_full = jnp.exp(acs_full)
        xdt = xs * dt_full
        xdt_b = xdt.astype(BF16)
        xdte_b = (xdt * dte_full).astype(BF16)
        cm_b = cm.astype(BF16)

        ys = []
        for g in range(SSD_GROUPS):
            bg = bm[:, ns * g:ns * (g + 1)]
            bg_b = bg.astype(BF16)
            bgt_b = bg.T.astype(BF16)
            cg_b = cm_b[:, ns * g:ns * (g + 1)]
            cb = lax.dot_general(cg_b, bg_b, (((1,), (1,)), ((), ())), preferred_element_type=F32)
            sg = st_ref[g]
            yoff = (jnp.dot(cg_b, sg.astype(BF16), preferred_element_type=F32)
                    * dfs_full[:, gcols * g:gcols * (g + 1)])
            parts = []
            for pr in range(hpg // 2):
                xpair = xdt_b[:, gcols * g + LANES * pr:gcols * g + LANES * (pr + 1)]
                res = []
                for hh in range(2):
                    h = hpg * g + 2 * pr + hh
                    seg = colv[:, SM_A + h:SM_A + h + 1] - rowv[SM_A + h:SM_A + h + 1, :]
                    lm = jnp.exp(jnp.where(tril, seg, -jnp.inf))
                    res.append(jnp.dot((cb * lm).astype(BF16), xpair, preferred_element_type=F32))
                parts.append(jnp.where(lo_half, res[0], res[1]))
            ydiag = jnp.concatenate(parts, axis=1)
            decay = jnp.exp(acs_last[:, gcols * g:gcols * (g + 1)])
            st_ref[g] = decay * sg + jnp.dot(bgt_b, xdte_b[:, gcols * g:gcols * (g + 1)],
                                             preferred_element_type=F32)
            ys.append(ydiag + yoff)

        y = jnp.concatenate(ys, axis=1) + dsk_ref[...] * xs
        gz = y * gate_all[rows]
        outs = []
        for g in range(SSD_GROUPS):
            gg = gz[:, gcols * g:gcols * (g + 1)]
            outs.append(gg * lax.rsqrt(jnp.mean(gg * gg, axis=-1, keepdims=True) + EPS))
        y_ref[0, rows, :] = (jnp.concatenate(outs, axis=1) * ng_ref[...]).astype(BF16)


def _pool_body(u_ref, w_ref, sc_ref, y_ref, buf_ref):
    t = pl.program_id(1)
    tt = u_ref.shape[1]
    hist = max(POOL_WINDOWS)
    gd = GROUP_WIDTH // len(POOL_WINDOWS)

    @pl.when(t == 0)
    def _():
        buf_ref[0:hist, :] = jnp.zeros((hist, GROUP_WIDTH), F32)

    u = u_ref[0]
    buf_ref[hist:hist + tt, :] = u
    count = (t * tt + 1 + lax.broadcasted_iota(I32, (tt, 1), 0)).astype(F32)
    outs = []
    for gi, win in enumerate(POOL_WINDOWS):
        ug = u[:, gd * gi:gd * (gi + 1)]
        acc = ug
        for k in range(1, win):
            acc = acc + buf_ref[hist - k:hist - k + tt, gd * gi:gd * (gi + 1)]
        pooled = acc / jnp.minimum(count, float(win)) - ug
        outs.append(jnp.dot(pooled.astype(BF16), w_ref[gi], preferred_element_type=F32))
    y_ref[0] = (jnp.concatenate(outs, axis=1) * sc_ref[...]).astype(BF16)
    buf_ref[0:hist, :] = u[tt - hist:tt, :]


def _mixers_body(sm_ref, dc_ref, u_ref, xbc_ref, z_ref, bias_ref, mul_ref, kvg_ref, pw_ref, psc_ref,
                 cw_ref, cb_ref, dsk_ref, ng_ref,
                 col_ref, row_ref, c_ref, ct_ref, fkp_ref, yb_ref, ya_ref,
                 carry_ref, buf_ref, xpad_ref, st_ref):
    _prep_body(sm_ref, dc_ref, bias_ref, mul_ref, kvg_ref, col_ref, row_ref, c_ref, ct_ref, fkp_ref, carry_ref)
    _pool_body(u_ref, pw_ref, psc_ref, yb_ref, buf_ref)
    _ssd_body(xbc_ref, z_ref, col_ref, row_ref, cw_ref, cb_ref, dsk_ref, ng_ref, ya_ref, xpad_ref, st_ref)


def _mixers(ua, bias_vec, mul_vec, kv_g, pool_w, pool_scale, conv_w, conv_b, dskip_full, norm_g, *, sub):
    b, lp, _ = ua.shape
    tt = sub * SEQ_TILE
    nblk = lp // SEQ_TILE
    ng = len(POOL_WINDOWS)
    gd = GROUP_WIDTH // ng
    hpg = N_HEADS // SSD_GROUPS
    vec = pl.BlockSpec((1, LANES), lambda i, t: (0, 0))
    gvec = pl.BlockSpec((1, GROUP_WIDTH), lambda i, t: (0, 0))
    seq = pl.BlockSpec((1, tt, LANES), lambda i, t: (i, t, 0))
    gseq = pl.BlockSpec((1, tt, GROUP_WIDTH), lambda i, t: (i, t, 0))
    return pl.pallas_call(
        _mixers_body,
        grid=(b, lp // tt),
        in_specs=[
            pl.BlockSpec((1, tt, LANES), lambda i, t: (i, t, A_SMALL // LANES)),
            pl.BlockSpec((1, tt, LANES), lambda i, t: (i, t, A_CKV // LANES)),
            pl.BlockSpec((1, tt, GROUP_WIDTH), lambda i, t: (i, t, A_POOL // GROUP_WIDTH)),
            pl.BlockSpec((1, tt, SSD_XBC), lambda i, t: (i, t, A_XBC // SSD_XBC)),
            pl.BlockSpec((1, tt, GROUP_WIDTH), lambda i, t: (i, t, A_Z // GROUP_WIDTH)),
            vec, vec, vec,
            pl.BlockSpec((ng, gd, gd), lambda i, t: (0, 0, 0)),
            gvec,
            pl.BlockSpec((SSD_CONV, SSD_XBC), lambda i, t: (0, 0)),
            pl.BlockSpec((1, SSD_XBC), lambda i, t: (0, 0)),
            gvec, gvec,
        ],
        out_specs=[
            seq,
            pl.BlockSpec((1, sub, LANES, SEQ_TILE), lambda i, t: (i, t, 0, 0)),
            seq,
            pl.BlockSpec((1, sub, CT_ROWS, SEQ_TILE), lambda i, t: (i, t, 0, 0)),
            seq, gseq, gseq,
        ],
        out_shape=[
            jax.ShapeDtypeStruct((b, lp, LANES), F32),
            jax.ShapeDtypeStruct((b, nblk, LANES, SEQ_TILE), F32),
            jax.ShapeDtypeStruct((b, lp, DSA_LATENT), BF16),
            jax.ShapeDtypeStruct((b, nblk, CT_ROWS, SEQ_TILE), BF16),
            jax.ShapeDtypeStruct((b, lp, LANES), BF16),
            jax.ShapeDtypeStruct((b, lp, GROUP_WIDTH), BF16),
            jax.ShapeDtypeStruct((b, lp, GROUP_WIDTH), BF16),
        ],
        scratch_shapes=[
            pltpu.VMEM((1, LANES), F32),
            pltpu.VMEM((max(POOL_WINDOWS) + tt, GROUP_WIDTH), F32),
            pltpu.VMEM((SUBLANES + tt, SSD_XBC), F32),
            pltpu.VMEM((SSD_GROUPS, SSD_STATE, hpg * HEAD_DIM), F32),
        ],
        compiler_params=pltpu.CompilerParams(dimension_semantics=("parallel", "arbitrary")),
        name="mixers",
    )(ua, ua, ua, ua, ua, bias_vec, mul_vec, kv_g, pool_w, pool_scale, conv_w, conv_b, dskip_full, norm_g)


def _fox_body(q_ref, k_ref, v_ref, fkp_ref, o_ref, qa_ref, vl_ref, acc_ref, m_ref):
    qi = pl.program_id(1)
    tq = q_ref.shape[1]
    tk = tq
    nt = (((1,), (1,)), ((), ()))
    npair = N_HEADS // 2
    lane = lax.broadcasted_iota(I32, (tq, LANES), 1)
    lo_half = lane < HEAD_DIM
    zero_b = jnp.zeros((tq, LANES), BF16)

    @pl.when(qi == 0)
    def _():
        nkt = vl_ref.shape[0]
        rsel = lax.broadcasted_iota(I32, (ACC_ROWS - 2 * HEAD_DIM, 2 * tk), 0)
        csel = lax.broadcasted_iota(I32, (ACC_ROWS - 2 * HEAD_DIM, 2 * tk), 1)
        ones_rows = jnp.where((rsel == 0) & (csel < tk) | (rsel == 1) & (csel >= tk), 1.0, 0.0).astype(BF16)
        zpad = jnp.zeros((HEAD_DIM, tk), BF16)

        def fill(kt, carry):
            ks = pl.multiple_of(kt * tk, tk)
            for pr in range(npair):
                vt = v_ref[0, pl.ds(ks, tk), LANES * pr:LANES * (pr + 1)].astype(F32).T.astype(BF16)
                top = jnp.concatenate([vt[:HEAD_DIM], zpad], axis=1)
                bot = jnp.concatenate([zpad, vt[HEAD_DIM:]], axis=1)
                vl_ref[kt, pr] = jnp.concatenate([top, bot, ones_rows], axis=0)
            return carry

        lax.fori_loop(0, nkt, fill, 0)

    for pr in range(npair):
        q2 = q_ref[0, :, LANES * pr:LANES * (pr + 1)]
        for hh in range(2):
            h = 2 * pr + hh
            qm = jnp.where(lo_half, q2, zero_b) if hh == 0 else jnp.where(lo_half, zero_b, q2)
            pick = (lane == SM_F + h) | (lane == SM_F + N_HEADS + h) | (lane == SM_F + 2 * N_HEADS + h)
            qa_ref[h] = jnp.concatenate([qm, jnp.where(pick, -1.0, 0.0).astype(BF16)], axis=1)
    acc_ref[...] = jnp.zeros_like(acc_ref)
    m_ref[...] = jnp.full(m_ref.shape, NEG_BIG, F32)
    causal_t = (lax.broadcasted_iota(I32, (tk, tq), 0) <= lax.broadcasted_iota(I32, (tk, tq), 1))
    tail_row = lax.broadcasted_iota(I32, (ACC_ROWS - 2 * HEAD_DIM, tq), 0)

    def step(kt, masked):
        ks = pl.multiple_of(kt * tk, tk)
        fkp = fkp_ref[0, pl.ds(ks, tk), :]
        for pr in range(npair):
            ka = jnp.concatenate([k_ref[0, pl.ds(ks, tk), LANES * pr:LANES * (pr + 1)], fkp], axis=1)
            ps, als = [], []
            for hh in range(2):
                h = 2 * pr + hh
                st = lax.dot_general(ka, qa_ref[h], nt, preferred_element_type=F32)
                if masked:
                    st = jnp.where(causal_t, st, -jnp.inf)
                m_old = m_ref[h]
                mn = jnp.maximum(m_old, jnp.max(st, axis=0, keepdims=True))
                ps.append(jnp.exp2(st - mn).astype(BF16))
                als.append(jnp.exp2(m_old - mn))
                m_ref[h] = mn
            upd = jnp.dot(vl_ref[kt, pr], jnp.concatenate(ps, axis=0), preferred_element_type=F32)
            scale = jnp.concatenate([jnp.broadcast_to(als[0], (HEAD_DIM, tq)),
                                     jnp.broadcast_to(als[1], (HEAD_DIM, tq)),
                                     jnp.where(tail_row == 0, als[0], als[1])], axis=0)
            acc_ref[pr] = scale * acc_ref[pr] + upd

    def body(kt, carry):
        step(kt, False)
        return carry

    lax.fori_loop(0, qi, body, 0)
    step(qi, True)
    for pr in range(npair):
        a = acc_ref[pr]
        o = jnp.concatenate([a[:HEAD_DIM] / a[2 * HEAD_DIM:2 * HEAD_DIM + 1],
                             a[HEAD_DIM:2 * HEAD_DIM] / a[2 * HEAD_DIM + 1:2 * HEAD_DIM + 2]], axis=0)
        o_ref[0, :, LANES * pr:LANES * (pr + 1)] = o.T.astype(BF16)


def _fox(ub, fkp, *, sub):
    b, lp, _ = ub.shape
    tq = sub * SEQ_TILE
    return pl.pallas_call(
        _fox_body,
        grid=(b, lp // tq),
        in_specs=[
            pl.BlockSpec((1, tq, GROUP_WIDTH), lambda i, q: (i, q, B_FQ // GROUP_WIDTH)),
            pl.BlockSpec((1, lp, GROUP_WIDTH), lambda i, q: (i, 0, B_FK // GROUP_WIDTH)),
            pl.BlockSpec((1, lp, GROUP_WIDTH), lambda i, q: (i, 0, B_FV // GROUP_WIDTH)),
            pl.BlockSpec((1, lp, LANES), lambda i, q: (i, 0, 0)),
        ],
        out_specs=pl.BlockSpec((1, tq, GROUP_WIDTH), lambda i, q: (i, q, 0)),
        out_shape=jax.ShapeDtypeStruct((b, lp, GROUP_WIDTH), BF16),
        scratch_shapes=[
            pltpu.VMEM((N_HEADS, tq, 2 * LANES), BF16),
            pltpu.VMEM((lp // tq, N_HEADS // 2, ACC_ROWS, 2 * tq), BF16),
            pltpu.VMEM((N_HEADS // 2, ACC_ROWS, tq), F32),
            pltpu.VMEM((N_HEADS, 1, tq), F32),
        ],
        compiler_params=pltpu.CompilerParams(
            dimension_semantics=("parallel", "arbitrary"), vmem_limit_bytes=_vmem(48)),
        name="fox",
    )(ub, ub, ub, fkp)


def _dsa_body(dq_ref, dqi_ref, ki_ref, c_ref, ct_ref, wr_ref, wuk_ref, wuvt_ref, o_ref,
              key_ref, bias_ref, hi_ref, lo_ref, qm_ref, qlat_ref, acc_ref, m_ref, *, topk, sub):
    qi = pl.program_id(1)
    tq = dq_ref.shape[1]
    tk = tq
    ngrp = qi + 1
    has_tail = qi + 1 < key_ref.shape[0]
    tblk = jnp.minimum(sub * (qi + 1), ct_ref.shape[1] - 1)
    t0 = pl.multiple_of(tblk * SEQ_TILE, SEQ_TILE)
    nt = (((1,), (1,)), ((), ()))
    lo_half = lax.broadcasted_iota(I32, (tq, LANES), 1) < IDX_DIM
    zero_b = jnp.zeros((tq, LANES), BF16)
    shift = CHUNK - N_META
    lg2 = CHUNK.bit_length() - 1
    kf = float(topk)

    def fold8(w):
        return jnp.sum(w.reshape(w.shape[0] // 8, 8, tq), axis=0)

    for h in range(IDX_HEADS):
        q2 = dqi_ref[0, :, LANES * (h // 2):LANES * (h // 2 + 1)]
        qm_ref[tq * h:tq * (h + 1), :] = (jnp.where(lo_half, q2, zero_b) if h % 2 == 0
                                           else jnp.where(lo_half, zero_b, q2))
    wrows = jnp.concatenate([wr_ref[0, j] for j in range(sub)], axis=1)

    def keys_of(kt):
        lg_all = lax.dot_general(kt, qm_ref[...], nt, preferred_element_type=F32)
        sc = None
        for h in range(IDX_HEADS):
            term = wrows[h:h + 1, :] * jnp.maximum(lg_all[:, tq * h:tq * (h + 1)], 0.0)
            sc = term if sc is None else sc + term
        bits = pltpu.bitcast(sc, I32)
        bits = jnp.where(bits == INT_MIN, 0, bits)
        return bits ^ ((bits >> 31) & 0x7FFFFFFF)

    def admissible(ks, rows):
        kcid = (ks + lax.broadcasted_iota(I32, (rows, 1), 0) + shift) >> lg2
        qcid = (qi * tq + lax.broadcasted_iota(I32, (1, tq), 1) + shift) >> lg2
        return kcid <= qcid

    def store_keys(g, key):
        key_ref[g] = key
        hi_ref[g] = (key >> 16).astype(I16)
        lo_ref[g] = ((key & (HALF - 1)) - HALF_BIAS).astype(I16)

    def score_step(g, carry):
        ks = pl.multiple_of(g * tk, tk)
        store_keys(g, keys_of(ki_ref[0, pl.ds(ks, tk), :]))
        return carry

    lax.fori_loop(0, qi, score_step, 0)
    q0 = pl.multiple_of(qi * tk, tk)
    store_keys(qi, jnp.where(admissible(q0, tk), keys_of(ki_ref[0, pl.ds(q0, tk), :]), INT_MIN))
    tkey = jnp.where(admissible(t0, TAIL) & has_tail, keys_of(ki_ref[0, pl.ds(t0, TAIL), :]), INT_MIN)

    one_b = jnp.ones((), BF16)
    zero_s = jnp.zeros((), BF16)

    def count_ge(x16, s16):
        w = jnp.where(x16 >= s16, one_b, zero_s)
        w3 = w.reshape(x16.shape[0] // 16, 16, tq)
        return functools.reduce(lambda a, b: a + b, [w3[r] for r in range(w3.shape[0])]).astype(F32)

    def bisect16(ref, tail16, need):
        def bit_step(i, u):
            uc = u | lax.shift_left(jnp.int32(1), 15 - i)
            s16 = (uc - HALF_BIAS).astype(I16)

            def cnt_step(g, acc):
                return acc + count_ge(ref[g], s16)

            acc = lax.fori_loop(0, ngrp, cnt_step, count_ge(tail16, s16))
            cnt = jnp.sum(acc, axis=0, keepdims=True)
            return jnp.where(cnt >= need, uc, u)

        return lax.fori_loop(0, 16, bit_step, jnp.zeros((1, tq), I32))

    thi = tkey >> 16
    tlo = (tkey & (HALF - 1)) - HALF_BIAS
    u_hi = bisect16(hi_ref, thi.astype(I16), kf)
    t_hi = u_hi - HALF_BIAS
    above16 = (t_hi + 1).astype(I16)

    def above_step(g, acc):
        return acc + count_ge(hi_ref[g], above16)

    n_above = jnp.sum(lax.fori_loop(0, ngrp, above_step, count_ge(thi.astype(I16), above16)), axis=0, keepdims=True)
    n_above = jnp.where(t_hi == HALF_BIAS - 1, 0.0, n_above)
    t_hi16 = t_hi.astype(I16)

    def narrow_step(g, carry):
        lo_ref[g] = jnp.where(hi_ref[g] == t_hi16, lo_ref[g], jnp.int16(-HALF_BIAS))
        return carry

    lax.fori_loop(0, ngrp, narrow_step, 0)
    tlo16 = jnp.where(thi == t_hi, tlo, -HALF_BIAS).astype(I16)
    u_lo = bisect16(lo_ref, tlo16, kf - n_above)
    thr = t_hi * HALF + u_lo

    def gt_step(g, acc):
        return acc + fold8(jnp.where(key_ref[g] > thr, 1.0, 0.0))

    ngt = jnp.sum(lax.fori_loop(0, ngrp, gt_step, fold8(jnp.where(tkey > thr, 1.0, 0.0))),
                  axis=0, keepdims=True)
    room = kf - ngt
    incl = jnp.where(lax.broadcasted_iota(I32, (tk, tk), 1) <= lax.broadcasted_iota(I32, (tk, tk), 0),
                     1.0, 0.0).astype(BF16)
    incl_tail = jnp.where(lax.broadcasted_iota(I32, (TAIL, TAIL), 1) <= lax.broadcasted_iota(I32, (TAIL, TAIL), 0),
                          1.0, 0.0).astype(BF16)

    room = jnp.where(thr == INT_MIN, 0.0, room)

    def bias_of(key, seen):
        rows = key.shape[0]
        eq = key == thr
        eqf = jnp.where(eq, 1.0, 0.0)
        rank = jnp.dot(incl if rows == tk else incl_tail, eqf.astype(BF16), preferred_element_type=F32) + seen
        tie = jnp.where(eq, jnp.where(rank <= room, 0.0, -jnp.inf), -jnp.inf)
        return jnp.where(key > thr, 0.0, tie), seen + jnp.sum(fold8(eqf), axis=0, keepdims=True)

    def mask_step(g, seen):
        bias_ref[g], seen = bias_of(key_ref[g], seen)
        return seen

    seen = lax.fori_loop(0, ngrp, mask_step, jnp.zeros((1, tq), F32))
    tbias, _ = bias_of(tkey, seen)

    for h in range(N_HEADS):
        dq2 = dq_ref[0, :, LANES * (h // 2):LANES * (h // 2 + 1)]
        ql = jnp.dot(dq2, wuk_ref[h], preferred_element_type=F32) * (LOG2E * HEAD_DIM ** -0.5)
        qlat_ref[tq * h:tq * (h + 1), :] = ql.astype(BF16)
    acc_ref[...] = jnp.zeros_like(acc_ref)
    m_ref[...] = jnp.full(m_ref.shape, NEG_BIG, F32)

    def attend(ck, cx, bias):
        st_all = lax.dot_general(ck, qlat_ref[...], nt, preferred_element_type=F32)
        ps, als = [], []
        for h in range(N_HEADS):
            st = st_all[:, tq * h:tq * (h + 1)] + bias
            m_old = m_ref[h]
            mn = jnp.maximum(m_old, jnp.max(st, axis=0, keepdims=True))
            ps.append(jnp.exp2(st - mn).astype(BF16))
            als.append(jnp.exp2(m_old - mn))
            m_ref[h] = mn
        upd = jnp.dot(cx, jnp.concatenate(ps, axis=1), preferred_element_type=F32)
        acc_ref[...] = jnp.concatenate(als, axis=1) * acc_ref[...] + upd

    def att_step(g, carry):
        ks = pl.multiple_of(g * tk, tk)
        cx = jnp.concatenate([ct_ref[0, g * sub + j] for j in range(sub)], axis=1)
        attend(c_ref[0, pl.ds(ks, tk), :], cx, bias_ref[g])
        return carry

    lax.fori_loop(0, ngrp, att_step, 0)
    attend(c_ref[0, pl.ds(t0, TAIL), :], ct_ref[0, tblk][:, :TAIL], tbias)

    for pr in range(N_HEADS // 2):
        outs = []
        for hh in range(2):
            a = acc_ref[:, tq * (2 * pr + hh):tq * (2 * pr + hh + 1)]
            olat = (a[:DSA_LATENT] / a[DSA_LATENT:DSA_LATENT + 1]).astype(BF16)
            outs.append(jnp.dot(wuvt_ref[2 * pr + hh], olat, preferred_element_type=F32))
        o_ref[0, :, LANES * pr:LANES * (pr + 1)] = jnp.concatenate(outs, axis=0).T.astype(BF16)


def _dsa(ub, c, ct, rowt, wuk_pad, wuv_t, *, topk, sub):
    b, lp, _ = ub.shape
    tq = sub * SEQ_TILE
    ngrp = lp // tq
    nblk = lp // SEQ_TILE
    return pl.pallas_call(
        functools.partial(_dsa_body, topk=topk, sub=sub),
        grid=(b, ngrp),
        in_specs=[
            pl.BlockSpec((1, tq, GROUP_WIDTH), lambda i, q: (i, q, B_DQ // GROUP_WIDTH)),
            pl.BlockSpec((1, tq, IDX_HEADS * IDX_DIM), lambda i, q: (i, q, B_IQ // (IDX_HEADS * IDX_DIM))),
            pl.BlockSpec((1, lp, LANES), lambda i, q: (i, 0, B_IK // LANES)),
            pl.BlockSpec((1, lp, DSA_LATENT), lambda i, q: (i, 0, 0)),
            pl.BlockSpec((1, nblk, CT_ROWS, SEQ_TILE), lambda i, q: (i, 0, 0, 0)),
            pl.BlockSpec((1, sub, N_HEADS, SEQ_TILE), lambda i, q: (i, q, SM_W // N_HEADS, 0)),
            pl.BlockSpec((N_HEADS, LANES, DSA_LATENT), lambda i, q: (0, 0, 0)),
            pl.BlockSpec((N_HEADS, HEAD_DIM, DSA_LATENT), lambda i, q: (0, 0, 0)),
        ],
        out_specs=pl.BlockSpec((1, tq, GROUP_WIDTH), lambda i, q: (i, q, 0)),
        out_shape=jax.ShapeDtypeStruct((b, lp, GROUP_WIDTH), BF16),
        scratch_shapes=[
            pltpu.VMEM((ngrp, tq, tq), I32),
            pltpu.VMEM((ngrp, tq, tq), F32),
            pltpu.VMEM((ngrp, tq, tq), I16),
            pltpu.VMEM((ngrp, tq, tq), I16),
            pltpu.VMEM((IDX_HEADS * tq, LANES), BF16),
            pltpu.VMEM((N_HEADS * tq, DSA_LATENT), BF16),
            pltpu.VMEM((CT_ROWS, N_HEADS * tq), F32),
            pltpu.VMEM((N_HEADS, 1, tq), F32),
        ],
        compiler_params=pltpu.CompilerParams(
            dimension_semantics=("parallel", "arbitrary"), vmem_limit_bytes=_vmem(48)),
        name="dsa",
    )(ub, ub, ub, c, ct, rowt, wuk_pad, wuv_t)


def _outproj_body(ya_ref, yb_ref, yc_ref, yd_ref, x_ref, w_ref, g_ref, o_ref):
    acc = jnp.dot(ya_ref[...], w_ref[0], preferred_element_type=F32)
    acc = acc + jnp.dot(yb_ref[...], w_ref[1], preferred_element_type=F32)
    acc = acc + jnp.dot(yc_ref[...], w_ref[2], preferred_element_type=F32)
    acc = acc + jnp.dot(yd_ref[...], w_ref[3], preferred_element_type=F32)
    ms = jnp.mean(acc * acc, axis=-1, keepdims=True)
    o_ref[...] = x_ref[...] + acc * lax.rsqrt(ms + EPS) * g_ref[...]


def _out_proj(ys, x2d, w4, g, *, layer, tm):
    m, d = x2d.shape
    gw = GROUP_WIDTH
    yspec = pl.BlockSpec((tm, gw), lambda i: (i, 0))
    return pl.pallas_call(
        _outproj_body,
        grid=(m // tm,),
        in_specs=[yspec, yspec, yspec, yspec,
                  pl.BlockSpec((tm, d), lambda i: (i, 0)),
                  pl.BlockSpec((None, 4, gw, d), lambda i: (layer, 0, 0, 0)),
                  pl.BlockSpec((1, d), lambda i: (0, 0))],
        out_specs=pl.BlockSpec((tm, d), lambda i: (i, 0)),
        out_shape=jax.ShapeDtypeStruct((m, d), F32),
        compiler_params=pltpu.CompilerParams(
            dimension_semantics=("parallel",), vmem_limit_bytes=_vmem(48)),
        name="out_proj",
    )(*ys, x2d, w4, g)


def _ffn_body(x_ref, gpre_ref, wg_ref, wu_ref, cw_ref, cb_ref, wd_ref, gpost_ref, o_ref,
              xn_ref, gbuf_ref, carry_ref, *, tiles_per_seq):
    i = pl.program_id(0)
    f = pl.program_id(1)
    nf = pl.num_programs(1)
    tm = x_ref.shape[0]

    @pl.when(f == 0)
    def _():
        x = x_ref[...]
        ms = jnp.mean(x * x, axis=-1, keepdims=True)
        xn_ref[...] = (x * lax.rsqrt(ms + EPS) * gpre_ref[...]).astype(BF16)
        o_ref[...] = jnp.zeros_like(o_ref)

    @pl.when(i % tiles_per_seq == 0)
    def _():
        carry_ref[f] = jnp.zeros(carry_ref.shape[1:], F32)

    xn = xn_ref[...]
    g = jnp.dot(xn, wg_ref[...], preferred_element_type=F32)
    u = jnp.dot(xn, wu_ref[...], preferred_element_type=F32)
    gbuf_ref[0:SUBLANES, :] = carry_ref[f]
    gbuf_ref[SUBLANES:SUBLANES + tm, :] = g
    conv = cb_ref[...] + cw_ref[FFN_CONV - 1:FFN_CONV, :] * g
    for k in range(FFN_CONV - 1):
        lag = SUBLANES - (FFN_CONV - 1 - k)
        conv = conv + cw_ref[k:k + 1, :] * gbuf_ref[lag:lag + tm, :]
    carry_ref[f] = g[tm - SUBLANES:tm, :]
    a = (_silu(conv) * u).astype(BF16)
    o_ref[...] += jnp.dot(a, wd_ref[...], preferred_element_type=F32)

    @pl.when(f == nf - 1)
    def _():
        y = o_ref[...]
        ms = jnp.mean(y * y, axis=-1, keepdims=True)
        o_ref[...] = x_ref[...] + y * lax.rsqrt(ms + EPS) * gpost_ref[...]


def _ffn(x2d, g_pre, w_gate, w_up, conv_w, conv_b, w_down, g_post, *, layer, tm, tf, tiles_per_seq):
    m, d = x2d.shape
    fdim = w_gate.shape[-1]
    nf = fdim // tf
    return pl.pallas_call(
        functools.partial(_ffn_body, tiles_per_seq=tiles_per_seq),
        grid=(m // tm, nf),
        in_specs=[
            pl.BlockSpec((tm, d), lambda i, f: (i, 0)),
            pl.BlockSpec((1, d), lambda i, f: (0, 0)),
            pl.BlockSpec((None, d, tf), lambda i, f: (layer, 0, f)),
            pl.BlockSpec((None, d, tf), lambda i, f: (layer, 0, f)),
            pl.BlockSpec((FFN_CONV, tf), lambda i, f: (0, f)),
            pl.BlockSpec((1, tf), lambda i, f: (0, f)),
            pl.BlockSpec((None, tf, d), lambda i, f: (layer, f, 0)),
            pl.BlockSpec((1, d), lambda i, f: (0, 0)),
        ],
        out_specs=pl.BlockSpec((tm, d), lambda i, f: (i, 0), pipeline_mode=pl.Buffered(1)),
        out_shape=jax.ShapeDtypeStruct((m, d), F32),
        scratch_shapes=[
            pltpu.VMEM((tm, d), BF16),
            pltpu.VMEM((SUBLANES + tm, tf), F32),
            pltpu.VMEM((nf, SUBLANES, tf), F32),
        ],
        compiler_params=pltpu.CompilerParams(
            dimension_semantics=("arbitrary", "arbitrary"), vmem_limit_bytes=_vmem(60)),
        name="ffn",
    )(x2d, g_pre, w_gate, w_up, conv_w, conv_b, w_down, g_post)


def _permute_w_in(w_in):
    gw = GROUP_WIDTH
    sizes = (gw, SSD_XBC, N_HEADS, gw, 3 * gw, N_HEADS, gw, DSA_LATENT, IDX_HEADS * IDX_DIM, IDX_DIM, IDX_HEADS)
    offs = [0]
    for s in sizes:
        offs.append(offs[-1] + s)
    z, xbc, dt, pool, qkv, fl, dq, dc, dqi, dki, dwi = (w_in[..., offs[k]:offs[k + 1]] for k in range(len(sizes)))
    qkv = jnp.concatenate([qkv[..., :gw] * (LOG2E * HEAD_DIM ** -0.5), qkv[..., gw:]], axis=-1)
    zeros = lambda n: jnp.zeros(w_in.shape[:-1] + (n,), w_in.dtype)
    small = jnp.concatenate([dt, fl, dwi, zeros(SM_A - SM_W - IDX_HEADS), dt, zeros(LANES - SM_A - N_HEADS)], axis=-1)
    a = jnp.concatenate([xbc, z, pool, dc, small, zeros(A_COLS - 2 * gw - SSD_XBC - DSA_LATENT - LANES)], axis=-1)
    bcols = jnp.concatenate([qkv, dq, dqi, dki, dki], axis=-1)
    bcols = jnp.concatenate([bcols, zeros(B_COLS - bcols.shape[-1])], axis=-1)
    return jnp.concatenate([a, bcols], axis=-1)


def _lane_vec(pieces):
    v = jnp.zeros((LANES,), F32)
    for off, val in pieces:
        v = v.at[off:off + val.shape[0]].set(val.astype(F32))
    return v[None, :]


def _pad_head_weights(w_uk, w_uv):
    h, r, d = w_uk.shape
    uk = jnp.zeros((h, 2 * d, r), F32)
    for i in range(h):
        o = d * (i % 2)
        uk = uk.at[i, o:o + d, :].set(w_uk[i].T)
    return uk.astype(BF16), jnp.swapaxes(w_uv, 1, 2).astype(BF16)


def _tile_sizes(b, lp):
    m = b * lp
    tm_proj = next(t for t in (1024, 512, 256, 128) if m % t == 0)
    tm_out = next(t for t in (512, 256, 128) if m % t == 0)
    tm_ffn = next(t for t in (1056, 528, 384, 320, 256, 128) if lp % t == 0)
    att_sub = 3 if lp % (3 * SEQ_TILE) == 0 else 1
    tn_proj = 1280
    tf_ffn = 512
    return tm_proj, tn_proj, tm_out, tm_ffn, tf_ffn, att_sub


def _layer(h, p, big, layer, *, topk):
    b, lp, d = h.shape
    m = b * lp
    tm_proj, tn_proj, tm_out, tm_ffn, tf_ffn, att_sub = _tile_sizes(b, lp)
    row = lambda v: v.astype(F32)[None, :]

    ua, ub = _in_proj(h.reshape(m, d), row(p["norm_mix_pre"]), big["w_in"], layer=layer, tm=tm_proj, tn=tn_proj)
    ua = ua.reshape(b, lp, A_COLS)
    ub = ub.reshape(b, lp, B_COLS)

    bias_vec = _lane_vec([(SM_DT, p["ssd_dt_bias"]), (SM_F, p["fox_f_bias"]), (SM_A, p["ssd_dt_bias"])])
    wscale = jnp.full((IDX_HEADS,), (IDX_HEADS ** -0.5) * (IDX_DIM ** -0.5), F32)
    mul_vec = _lane_vec([(SM_W, wscale), (SM_A, -jnp.exp(p["ssd_a_log"].astype(F32)))])
    dskip_full = jnp.repeat(p["ssd_d"].astype(F32), HEAD_DIM)[None, :]
    col, rowt, c, ct, fkp, y_b, y_a = _mixers(
        ua, bias_vec, mul_vec, row(p["dsa_kv_norm"]), p["pool_w"].astype(BF16), row(p["pool_scale"]),
        p["ssd_conv_w"].astype(F32), row(p["ssd_conv_b"]), dskip_full, row(p["ssd_norm"]), sub=att_sub)
    y_c = _fox(ub, fkp, sub=att_sub)
    wuk_pad, wuv_t = _pad_head_weights(p["dsa_w_uk"], p["dsa_w_uv"])
    y_d = _dsa(ub, c, ct, rowt, wuk_pad, wuv_t, topk=topk, sub=att_sub)

    ys = [y.reshape(m, GROUP_WIDTH) for y in (y_a, y_b, y_c, y_d)]
    x1 = _out_proj(ys, h.reshape(m, d), big["w_out"], row(p["norm_mix_post"]), layer=layer, tm=tm_out)
    x2 = _ffn(x1, row(p["norm_ffn_pre"]), big["ffn_w_gate"], big["ffn_w_up"],
              p["ffn_conv_w"].astype(F32), row(p["ffn_conv_b"]), big["ffn_w_down"],
              row(p["norm_ffn_post"]), layer=layer, tm=tm_ffn, tf=tf_ffn, tiles_per_seq=lp // tm_ffn)
    return x2.reshape(b, lp, d)


def _stack_big_weights(w_in, w_out, ffn_w_gate, ffn_w_up, ffn_w_down):
    depth, d, _ = w_out.shape
    return dict(w_in=_to_bf16(_permute_w_in(w_in)),
                w_out=_to_bf16(w_out).reshape(depth, 4, GROUP_WIDTH, d),
                ffn_w_gate=_to_bf16(ffn_w_gate), ffn_w_up=_to_bf16(ffn_w_up), ffn_w_down=_to_bf16(ffn_w_down))


def kernel(x, meta_tokens, norm_mix_pre, norm_mix_post, norm_ffn_pre, norm_ffn_post, w_in, ssd_conv_w, ssd_conv_b, ssd_dt_bias, ssd_a_log, ssd_d, ssd_norm, pool_w, pool_scale, fox_f_bias, dsa_kv_norm, dsa_w_uk, dsa_w_uv, w_out, ffn_w_gate, ffn_w_up, ffn_conv_w, ffn_conv_b, ffn_w_down):
    bsz, seq, d = x.shape
    n = N_META + seq
    lp = -(-n // SEQ_TILE) * SEQ_TILE
    topk = min(DSA_TOPK_MAX, seq // 4)
    meta = jnp.broadcast_to(meta_tokens.astype(x.dtype)[None], (bsz, N_META, d))
    h = jnp.concatenate([meta, x, jnp.zeros((bsz, lp - n, d), x.dtype)], axis=1)
    small = dict(norm_mix_pre=norm_mix_pre, norm_mix_post=norm_mix_post, norm_ffn_pre=norm_ffn_pre,
                 norm_ffn_post=norm_ffn_post, ssd_conv_w=ssd_conv_w, ssd_conv_b=ssd_conv_b,
                 ssd_dt_bias=ssd_dt_bias, ssd_a_log=ssd_a_log, ssd_d=ssd_d, ssd_norm=ssd_norm,
                 pool_w=pool_w, pool_scale=pool_scale, fox_f_bias=fox_f_bias, dsa_kv_norm=dsa_kv_norm,
                 dsa_w_uk=dsa_w_uk, dsa_w_uv=dsa_w_uv, ffn_conv_w=ffn_conv_w, ffn_conv_b=ffn_conv_b)
    big = _stack_big_weights(w_in, w_out, ffn_w_gate, ffn_w_up, ffn_w_down)
    for i in range(norm_mix_pre.shape[0]):
        h = _layer(h, {k: v[i] for k, v in small.items()}, big, i, topk=topk)
    return h[:, N_META:n]
```

```python
import functools

import jax
import jax.numpy as jnp
from jax import lax
from jax.experimental import pallas as pl
from jax.experimental.pallas import tpu as pltpu

F32 = jnp.float32
BF16 = jnp.bfloat16
I32 = jnp.int32
I16 = jnp.int16

EPS = 1e-6
N_META = 16
CHUNK = 64
HEAD_DIM = 64
GROUP_WIDTH = 512
N_HEADS = 8
SSD_GROUPS = 2
SSD_STATE = 128
SSD_CONV = 4
SSD_XBC = GROUP_WIDTH + 2 * SSD_GROUPS * SSD_STATE
POOL_WINDOWS = (2, 4, 8, 16)
DSA_LATENT = 128
IDX_HEADS = 4
IDX_DIM = 64
DSA_TOPK_MAX = 256
FFN_CONV = 3

LANES = 128
SEQ_TILE = 128
INT_MIN = -(2 ** 31)
HALF = 1 << 16
HALF_BIAS = 1 << 15
SUBLANES = 8
NEG_BIG = -1e30
LOG2E = 1.4426950408889634

A_COLS = 2560
B_COLS = 2560
A_XBC, A_Z, A_POOL, A_CKV, A_SMALL = 0, 1024, 1536, 2048, 2176
B_FQ, B_FK, B_FV, B_DQ, B_IQ, B_IK = 0, 512, 1024, 1536, 2048, 2304
SM_DT = 0
SM_F = 8
SM_W = 16
SM_A = 24
CT_ROWS = DSA_LATENT + 16
ACC_ROWS = 2 * HEAD_DIM + 16
TAIL = N_META


def _vmem(mb):
    return int(mb * 1024 * 1024)


def _softplus_parts(x):
    t = jnp.log1p(jnp.exp(-jnp.abs(x)))
    return jnp.maximum(x, 0.0) + t, jnp.minimum(x, 0.0) - t


def _silu(x):
    return x / (1.0 + jnp.exp(-x))


def _cast_body(x_ref, o_ref):
    o_ref[...] = x_ref[...].astype(BF16)


def _to_bf16(w, *, tr=512):
    shape = w.shape
    w2 = w.reshape(-1, shape[-1])
    r, c = w2.shape
    out = pl.pallas_call(
        _cast_body,
        grid=(r // tr,),
        in_specs=[pl.BlockSpec((tr, c), lambda i: (i, 0))],
        out_specs=pl.BlockSpec((tr, c), lambda i: (i, 0)),
        out_shape=jax.ShapeDtypeStruct((r, c), BF16),
        compiler_params=pltpu.CompilerParams(dimension_semantics=("parallel",), vmem_limit_bytes=_vmem(40)),
        name="to_bf16",
    )(w2)
    return out.reshape(shape)


def _inproj_body(x_ref, g_ref, w_ref, oa_ref, ob_ref, xn_ref, *, n_a):
    j = pl.program_id(1)

    @pl.when(j == 0)
    def _():
        x = x_ref[...]
        ms = jnp.mean(x * x, axis=-1, keepdims=True)
        xn_ref[...] = (x * lax.rsqrt(ms + EPS) * g_ref[...]).astype(BF16)

    acc = jnp.dot(xn_ref[...], w_ref[...], preferred_element_type=F32)

    @pl.when(j < n_a)
    def _():
        oa_ref[...] = acc

    @pl.when(j >= n_a)
    def _():
        ob_ref[...] = acc.astype(BF16)


def _in_proj(x2d, g, w_perm, *, layer, tm, tn):
    m, d = x2d.shape
    n_a, n_b = A_COLS // tn, B_COLS // tn
    return pl.pallas_call(
        functools.partial(_inproj_body, n_a=n_a),
        grid=(m // tm, n_a + n_b),
        in_specs=[
            pl.BlockSpec((tm, d), lambda i, j: (i, 0)),
            pl.BlockSpec((1, d), lambda i, j: (0, 0)),
            pl.BlockSpec((None, d, tn), lambda i, j: (layer, 0, j)),
        ],
        out_specs=[
            pl.BlockSpec((tm, tn), lambda i, j: (i, jnp.minimum(j, n_a - 1))),
            pl.BlockSpec((tm, tn), lambda i, j: (i, jnp.maximum(j - n_a, 0))),
        ],
        out_shape=[
            jax.ShapeDtypeStruct((m, A_COLS), F32),
            jax.ShapeDtypeStruct((m, B_COLS), BF16),
        ],
        scratch_shapes=[pltpu.VMEM((tm, d), BF16)],
        compiler_params=pltpu.CompilerParams(
            dimension_semantics=("parallel", "arbitrary"), vmem_limit_bytes=_vmem(58)),
        name="in_proj",
    )(x2d, g, w_perm)


def _prep_body(sm_ref, dc_ref, bias_ref, mul_ref, kvg_ref, col_ref, row_ref, c_ref, ct_ref, fkp_ref,
               carry_ref):
    t = pl.program_id(1)

    @pl.when(t == 0)
    def _():
        carry_ref[...] = jnp.zeros_like(carry_ref)

    tt = sm_ref.shape[1]
    sub = tt // SEQ_TILE
    s = sm_ref[0]
    lane = lax.broadcasted_iota(I32, (tt, LANES), 1)
    is_dt = lane < SM_F
    is_f = (lane >= SM_F) & (lane < SM_W)
    is_a = (lane >= SM_A) & (lane < SM_A + N_HEADS)
    sp, ls = _softplus_parts(s + bias_ref[...])
    v = jnp.where(is_dt, sp, jnp.where(is_f, ls, jnp.where(is_a, sp, s) * mul_ref[...]))
    ri = lax.broadcasted_iota(I32, (tt, tt), 0)
    ci = lax.broadcasted_iota(I32, (tt, tt), 1)
    tril = jnp.where(ci <= ri, 1.0, 0.0).astype(F32)
    run = jnp.dot(tril, v, precision=lax.Precision.HIGHEST, preferred_element_type=F32)
    rowi = lax.broadcasted_iota(I32, (tt, 1), 0)
    local = run
    for j in range(1, sub):
        local = jnp.where(rowi >= SEQ_TILE * j, run - run[SEQ_TILE * j - 1:SEQ_TILE * j, :], local)
    out = jnp.where(is_f, run + carry_ref[...], jnp.where(is_a, local, v))
    col_ref[0] = out
    out_t = out.T
    for j in range(sub):
        row_ref[0, j] = out_t[:, SEQ_TILE * j:SEQ_TILE * (j + 1)]
    carry_ref[...] = jnp.where(is_f[0:1], out[tt - 1:tt, :], 0.0)

    f0 = jnp.where(is_f, out, 0.0) * LOG2E
    hi = f0.astype(BF16).astype(F32)
    r1 = f0 - hi
    mid = r1.astype(BF16).astype(F32)
    lo = (r1 - mid).astype(BF16).astype(F32)
    fkp_ref[0] = (hi + pltpu.roll(mid, N_HEADS, axis=1) + pltpu.roll(lo, 2 * N_HEADS, axis=1)).astype(BF16)

    dc = dc_ref[0]
    ms = jnp.mean(dc * dc, axis=-1, keepdims=True)
    cn = dc * lax.rsqrt(ms + EPS) * kvg_ref[...]
    c_ref[0] = cn.astype(BF16)
    cn_t = cn.T
    ones = jnp.ones((CT_ROWS - DSA_LATENT, SEQ_TILE), F32)
    for j in range(sub):
        ct_ref[0, j] = jnp.concatenate([cn_t[:, SEQ_TILE * j:SEQ_TILE * (j + 1)], ones], axis=0).astype(BF16)


def _expand_heads(colv, base, lo_half):
    parts = []
    for p in range(N_HEADS // 2):
        a = colv[:, base + 2 * p:base + 2 * p + 1]
        b = colv[:, base + 2 * p + 1:base + 2 * p + 2]
        parts.append(jnp.where(lo_half, a, b))
    return jnp.concatenate(parts, axis=1)


def _ssd_body(xbc_ref, z_ref, col_ref, row_ref, cw_ref, cb_ref, dsk_ref, ng_ref, y_ref,
              xpad_ref, st_ref):
    c = pl.program_id(1)
    tt = xbc_ref.shape[1]
    ll = SEQ_TILE
    gw = GROUP_WIDTH
    ns = SSD_STATE
    hpg = N_HEADS // SSD_GROUPS
    gcols = hpg * HEAD_DIM

    @pl.when(c == 0)
    def _():
        xpad_ref[0:SUBLANES, :] = jnp.zeros((SUBLANES, SSD_XBC), F32)
        st_ref[...] = jnp.zeros_like(st_ref)

    x = xbc_ref[0]
    xpad_ref[SUBLANES:SUBLANES + tt, :] = x
    conv = cb_ref[...] + cw_ref[SSD_CONV - 1:SSD_CONV, :] * x
    for k in range(SSD_CONV - 1):
        lag = SUBLANES - (SSD_CONV - 1 - k)
        conv = conv + cw_ref[k:k + 1, :] * xpad_ref[lag:lag + tt, :]
    xpad_ref[0:SUBLANES, :] = x[tt - SUBLANES:tt, :]
    act_all = _silu(conv)
    gate_all = _silu(z_ref[0])
    lo_half = lax.broadcasted_iota(I32, (ll, LANES), 1) < HEAD_DIM
    tril = lax.broadcasted_iota(I32, (ll, ll), 1) <= lax.broadcasted_iota(I32, (ll, ll), 0)

    for j in range(tt // ll):
        rows = slice(ll * j, ll * (j + 1))
        act = act_all[rows]
        xs = act[:, 0:gw]
        bm = act[:, gw:gw + SSD_GROUPS * ns]
        cm = act[:, gw + SSD_GROUPS * ns:]
        colv = col_ref[0, rows, :]
        rowv = row_ref[0, j]
        dt_full = _expand_heads(colv, SM_DT, lo_half)
        acs_full = _expand_heads(colv, SM_A, lo_half)
        acs_last = acs_full[ll - 1:ll, :]
        dte_full = jnp.exp(acs_last - acs_full)
        dfs_full = jnp.exp(acs_full)
        xdt = xs * dt_full
        xdt_b = xdt.astype(BF16)
        xdte_b = (xdt * dte_full).astype(BF16)
        cm_b = cm.astype(BF16)

        ys = []
        for g in range(SSD_GROUPS):
            bg = bm[:, ns * g:ns * (g + 1)]
            bg_b = bg.astype(BF16)
            bgt_b = bg.T.astype(BF16)
            cg_b = cm_b[:, ns * g:ns * (g + 1)]
            cb = lax.dot_general(cg_b, bg_b, (((1,), (1,)), ((), ())), preferred_element_type=F32)
            sg = st_ref[g]
            yoff = (jnp.dot(cg_b, sg.astype(BF16), preferred_element_type=F32)
                    * dfs_full[:, gcols * g:gcols * (g + 1)])
            parts = []
            for pr in range(hpg // 2):
                xpair = xdt_b[:, gcols * g + LANES * pr:gcols * g + LANES * (pr + 1)]
                res = []
                for hh in range(2):
                    h = hpg * g + 2 * pr + hh
                    seg = colv[:, SM_A + h:SM_A + h + 1] - rowv[SM_A + h:SM_A + h + 1, :]
                    lm = jnp.exp(jnp.where(tril, seg, -jnp.inf))
                    res.append(jnp.dot((cb * lm).astype(BF16), xpair, preferred_element_type=F32))
                parts.append(jnp.where(lo_half, res[0], res[1]))
            ydiag = jnp.concatenate(parts, axis=1)
            decay = jnp.exp(acs_last[:, gcols * g:gcols * (g + 1)])
            st_ref[g] = decay * sg + jnp.dot(bgt_b, xdte_b[:, gcols * g:gcols * (g + 1)],
                                             preferred_element_type=F32)
            ys.append(ydiag + yoff)

        y = jnp.concatenate(ys, axis=1) + dsk_ref[...] * xs
        gz = y * gate_all[rows]
        outs = []
        for g in range(SSD_GROUPS):
            gg = gz[:, gcols * g:gcols * (g + 1)]
            outs.append(gg * lax.rsqrt(jnp.mean(gg * gg, axis=-1, keepdims=True) + EPS))
        y_ref[0, rows, :] = (jnp.concatenate(outs, axis=1) * ng_ref[...]).astype(BF16)


def _ssd(ua, col, row, conv_w, conv_b, dskip_full, norm_g, *, sub):
    b, lp, _ = ua.shape
    tt = sub * SEQ_TILE
    hpg = N_HEADS // SSD_GROUPS
    return pl.pallas_call(
        _ssd_body,
        grid=(b, lp // tt),
        in_specs=[
            pl.BlockSpec((1, tt, SSD_XBC), lambda i, c: (i, c, A_XBC // SSD_XBC)),
            pl.BlockSpec((1, tt, GROUP_WIDTH), lambda i, c: (i, c, A_Z // GROUP_WIDTH)),
            pl.BlockSpec((1, tt, LANES), lambda i, c: (i, c, 0)),
            pl.BlockSpec((1, sub, LANES, SEQ_TILE), lambda i, c: (i, c, 0, 0)),
            pl.BlockSpec((SSD_CONV, SSD_XBC), lambda i, c: (0, 0)),
            pl.BlockSpec((1, SSD_XBC), lambda i, c: (0, 0)),
            pl.BlockSpec((1, GROUP_WIDTH), lambda i, c: (0, 0)),
            pl.BlockSpec((1, GROUP_WIDTH), lambda i, c: (0, 0)),
        ],
        out_specs=pl.BlockSpec((1, tt, GROUP_WIDTH), lambda i, c: (i, c, 0)),
        out_shape=jax.ShapeDtypeStruct((b, lp, GROUP_WIDTH), BF16),
        scratch_shapes=[
            pltpu.VMEM((SUBLANES + tt, SSD_XBC), F32),
            pltpu.VMEM((SSD_GROUPS, SSD_STATE, hpg * HEAD_DIM), F32),
        ],
        compiler_params=pltpu.CompilerParams(dimension_semantics=("parallel", "arbitrary")),
        name="ssd",
    )(ua, ua, col, row, conv_w, conv_b, dskip_full, norm_g)


def _pool_body(u_ref, w_ref, sc_ref, y_ref, buf_ref):
    t = pl.program_id(1)
    tt = u_ref.shape[1]
    hist = max(POOL_WINDOWS)
    gd = GROUP_WIDTH // len(POOL_WINDOWS)

    @pl.when(t == 0)
    def _():
        buf_ref[0:hist, :] = jnp.zeros((hist, GROUP_WIDTH), F32)

    u = u_ref[0]
    buf_ref[hist:hist + tt, :] = u
    count = (t * tt + 1 + lax.broadcasted_iota(I32, (tt, 1), 0)).astype(F32)
    outs = []
    for gi, win in enumerate(POOL_WINDOWS):
        ug = u[:, gd * gi:gd * (gi + 1)]
        acc = ug
        for k in range(1, win):
            acc = acc + buf_ref[hist - k:hist - k + tt, gd * gi:gd * (gi + 1)]
        pooled = acc / jnp.minimum(count, float(win)) - ug
        outs.append(jnp.dot(pooled.astype(BF16), w_ref[gi], preferred_element_type=F32))
    y_ref[0] = (jnp.concatenate(outs, axis=1) * sc_ref[...]).astype(BF16)
    buf_ref[0:hist, :] = u[tt - hist:tt, :]


def _prep_pool_body(sm_ref, dc_ref, u_ref, bias_ref, mul_ref, kvg_ref, pw_ref, psc_ref,
                    col_ref, row_ref, c_ref, ct_ref, fkp_ref, yb_ref, carry_ref, buf_ref):
    _prep_body(sm_ref, dc_ref, bias_ref, mul_ref, kvg_ref, col_ref, row_ref, c_ref, ct_ref, fkp_ref, carry_ref)
    _pool_body(u_ref, pw_ref, psc_ref, yb_ref, buf_ref)


def _prep_pool(ua, bias_vec, mul_vec, kv_g, pool_w, pool_scale, *, sub):
    b, lp, _ = ua.shape
    tt = sub * SEQ_TILE
    nblk = lp // SEQ_TILE
    ng = len(POOL_WINDOWS)
    gd = GROUP_WIDTH // ng
    vec = pl.BlockSpec((1, LANES), lambda i, t: (0, 0))
    seq = pl.BlockSpec((1, tt, LANES), lambda i, t: (i, t, 0))
    return pl.pallas_call(
        _prep_pool_body,
        grid=(b, lp // tt),
        in_specs=[
            pl.BlockSpec((1, tt, LANES), lambda i, t: (i, t, A_SMALL // LANES)),
            pl.BlockSpec((1, tt, LANES), lambda i, t: (i, t, A_CKV // LANES)),
            pl.BlockSpec((1, tt, GROUP_WIDTH), lambda i, t: (i, t, A_POOL // GROUP_WIDTH)),
            vec, vec, vec,
            pl.BlockSpec((ng, gd, gd), lambda i, t: (0, 0, 0)),
            pl.BlockSpec((1, GROUP_WIDTH), lambda i, t: (0, 0)),
        ],
        out_specs=[
            seq,
            pl.BlockSpec((1, sub, LANES, SEQ_TILE), lambda i, t: (i, t, 0, 0)),
            seq,
            pl.BlockSpec((1, sub, CT_ROWS, SEQ_TILE), lambda i, t: (i, t, 0, 0)),
            seq,
            pl.BlockSpec((1, tt, GROUP_WIDTH), lambda i, t: (i, t, 0)),
        ],
        out_shape=[
            jax.ShapeDtypeStruct((b, lp, LANES), F32),
            jax.ShapeDtypeStruct((b, nblk, LANES, SEQ_TILE), F32),
            jax.ShapeDtypeStruct((b, lp, DSA_LATENT), BF16),
            jax.ShapeDtypeStruct((b, nblk, CT_ROWS, SEQ_TILE), BF16),
            jax.ShapeDtypeStruct((b, lp, LANES), BF16),
            jax.ShapeDtypeStruct((b, lp, GROUP_WIDTH), BF16),
        ],
        scratch_shapes=[pltpu.VMEM((1, LANES), F32), pltpu.VMEM((max(POOL_WINDOWS) + tt, GROUP_WIDTH), F32)],
        compiler_params=pltpu.CompilerParams(dimension_semantics=("parallel", "arbitrary")),
        name="prep_pool",
    )(ua, ua, ua, bias_vec, mul_vec, kv_g, pool_w, pool_scale)


def _fox_body(q_ref, k_ref, v_ref, fkp_ref, o_ref, qa_ref, vl_ref, acc_ref, m_ref):
    qi = pl.program_id(1)
    tq = q_ref.shape[1]
    tk = tq
    nt = (((1,), (1,)), ((), ()))
    npair = N_HEADS // 2
    lane = lax.broadcasted_iota(I32, (tq, LANES), 1)
    lo_half = lane < HEAD_DIM
    zero_b = jnp.zeros((tq, LANES), BF16)

    @pl.when(qi == 0)
    def _():
        nkt = vl_ref.shape[0]
        rsel = lax.broadcasted_iota(I32, (ACC_ROWS - 2 * HEAD_DIM, 2 * tk), 0)
        csel = lax.broadcasted_iota(I32, (ACC_ROWS - 2 * HEAD_DIM, 2 * tk), 1)
        ones_rows = jnp.where((rsel == 0) & (csel < tk) | (rsel == 1) & (csel >= tk), 1.0, 0.0).astype(BF16)
        zpad = jnp.zeros((HEAD_DIM, tk), BF16)

        def fill(kt, carry):
            ks = pl.multiple_of(kt * tk, tk)
            for pr in range(npair):
                vt = v_ref[0, pl.ds(ks, tk), LANES * pr:LANES * (pr + 1)].astype(F32).T.astype(BF16)
                top = jnp.concatenate([vt[:HEAD_DIM], zpad], axis=1)
                bot = jnp.concatenate([zpad, vt[HEAD_DIM:]], axis=1)
                vl_ref[kt, pr] = jnp.concatenate([top, bot, ones_rows], axis=0)
            return carry

        lax.fori_loop(0, nkt, fill, 0)

    for pr in range(npair):
        q2 = q_ref[0, :, LANES * pr:LANES * (pr + 1)]
        for hh in range(2):
            h = 2 * pr + hh
            qm = jnp.where(lo_half, q2, zero_b) if hh == 0 else jnp.where(lo_half, zero_b, q2)
            pick = (lane == SM_F + h) | (lane == SM_F + N_HEADS + h) | (lane == SM_F + 2 * N_HEADS + h)
            qa_ref[h] = jnp.concatenate([qm, jnp.where(pick, -1.0, 0.0).astype(BF16)], axis=1)
    acc_ref[...] = jnp.zeros_like(acc_ref)
    m_ref[...] = jnp.full(m_ref.shape, NEG_BIG, F32)
    causal_t = (lax.broadcasted_iota(I32, (tk, tq), 0) <= lax.broadcasted_iota(I32, (tk, tq), 1))
    tail_row = lax.broadcasted_iota(I32, (ACC_ROWS - 2 * HEAD_DIM, tq), 0)

    def step(kt, masked):
        ks = pl.multiple_of(kt * tk, tk)
        fkp = fkp_ref[0, pl.ds(ks, tk), :]
        for pr in range(npair):
            ka = jnp.concatenate([k_ref[0, pl.ds(ks, tk), LANES * pr:LANES * (pr + 1)], fkp], axis=1)
            ps, als = [], []
            for hh in range(2):
                h = 2 * pr + hh
                st = lax.dot_general(ka, qa_ref[h], nt, preferred_element_type=F32)
                if masked:
                    st = jnp.where(causal_t, st, -jnp.inf)
                m_old = m_ref[h]
                mn = jnp.maximum(m_old, jnp.max(st, axis=0, keepdims=True))
                ps.append(jnp.exp2(st - mn).astype(BF16))
                als.append(jnp.exp2(m_old - mn))
                m_ref[h] = mn
            upd = jnp.dot(vl_ref[kt, pr], jnp.concatenate(ps, axis=0), preferred_element_type=F32)
            scale = jnp.concatenate([jnp.broadcast_to(als[0], (HEAD_DIM, tq)),
                                     jnp.broadcast_to(als[1], (HEAD_DIM, tq)),
                                     jnp.where(tail_row == 0, als[0], als[1])], axis=0)
            acc_ref[pr] = scale * acc_ref[pr] + upd

    def body(kt, carry):
        step(kt, False)
        return carry

    lax.fori_loop(0, qi, body, 0)
    step(qi, True)
    for pr in range(npair):
        a = acc_ref[pr]
        o = jnp.concatenate([a[:HEAD_DIM] / a[2 * HEAD_DIM:2 * HEAD_DIM + 1],
                             a[HEAD_DIM:2 * HEAD_DIM] / a[2 * HEAD_DIM + 1:2 * HEAD_DIM + 2]], axis=0)
        o_ref[0, :, LANES * pr:LANES * (pr + 1)] = o.T.astype(BF16)


def _fox(ub, fkp, *, sub):
    b, lp, _ = ub.shape
    tq = sub * SEQ_TILE
    return pl.pallas_call(
        _fox_body,
        grid=(b, lp // tq),
        in_specs=[
            pl.BlockSpec((1, tq, GROUP_WIDTH), lambda i, q: (i, q, B_FQ // GROUP_WIDTH)),
            pl.BlockSpec((1, lp, GROUP_WIDTH), lambda i, q: (i, 0, B_FK // GROUP_WIDTH)),
            pl.BlockSpec((1, lp, GROUP_WIDTH), lambda i, q: (i, 0, B_FV // GROUP_WIDTH)),
            pl.BlockSpec((1, lp, LANES), lambda i, q: (i, 0, 0)),
        ],
        out_specs=pl.BlockSpec((1, tq, GROUP_WIDTH), lambda i, q: (i, q, 0)),
        out_shape=jax.ShapeDtypeStruct((b, lp, GROUP_WIDTH), BF16),
        scratch_shapes=[
            pltpu.VMEM((N_HEADS, tq, 2 * LANES), BF16),
            pltpu.VMEM((lp // tq, N_HEADS // 2, ACC_ROWS, 2 * tq), BF16),
            pltpu.VMEM((N_HEADS // 2, ACC_ROWS, tq), F32),
            pltpu.VMEM((N_HEADS, 1, tq), F32),
        ],
        compiler_params=pltpu.CompilerParams(
            dimension_semantics=("parallel", "arbitrary"), vmem_limit_bytes=_vmem(48)),
        name="fox",
    )(ub, ub, ub, fkp)


def _dsa_body(dq_ref, dqi_ref, ki_ref, c_ref, ct_ref, wr_ref, wuk_ref, wuvt_ref, o_ref,
              key_ref, bias_ref, hi_ref, lo_ref, qm_ref, qlat_ref, acc_ref, m_ref, *, topk, sub):
    qi = pl.program_id(1)
    tq = dq_ref.shape[1]
    tk = tq
    ngrp = qi + 1
    has_tail = qi + 1 < key_ref.shape[0]
    tblk = jnp.minimum(sub * (qi + 1), ct_ref.shape[1] - 1)
    t0 = pl.multiple_of(tblk * SEQ_TILE, SEQ_TILE)
    nt = (((1,), (1,)), ((), ()))
    lo_half = lax.broadcasted_iota(I32, (tq, LANES), 1) < IDX_DIM
    zero_b = jnp.zeros((tq, LANES), BF16)
    shift = CHUNK - N_META
    lg2 = CHUNK.bit_length() - 1
    kf = float(topk)

    def fold8(w):
        return jnp.sum(w.reshape(w.shape[0] // 8, 8, tq), axis=0)

    for h in range(IDX_HEADS):
        q2 = dqi_ref[0, :, LANES * (h // 2):LANES * (h // 2 + 1)]
        qm_ref[tq * h:tq * (h + 1), :] = (jnp.where(lo_half, q2, zero_b) if h % 2 == 0
                                           else jnp.where(lo_half, zero_b, q2))
    wrows = jnp.concatenate([wr_ref[0, j] for j in range(sub)], axis=1)

    def keys_of(kt):
        lg_all = lax.dot_general(kt, qm_ref[...], nt, preferred_element_type=F32)
        sc = None
        for h in range(IDX_HEADS):
            term = wrows[h:h + 1, :] * jnp.maximum(lg_all[:, tq * h:tq * (h + 1)], 0.0)
            sc = term if sc is None else sc + term
        bits = pltpu.bitcast(sc, I32)
        bits = jnp.where(bits == INT_MIN, 0, bits)
        return bits ^ ((bits >> 31) & 0x7FFFFFFF)

    def admissible(ks, rows):
        kcid = (ks + lax.broadcasted_iota(I32, (rows, 1), 0) + shift) >> lg2
        qcid = (qi * tq + lax.broadcasted_iota(I32, (1, tq), 1) + shift) >> lg2
        return kcid <= qcid

    def store_keys(g, key):
        key_ref[g] = key
        hi_ref[g] = (key >> 16).astype(I16)
        lo_ref[g] = ((key & (HALF - 1)) - HALF_BIAS).astype(I16)

    def score_step(g, carry):
        ks = pl.multiple_of(g * tk, tk)
        store_keys(g, keys_of(ki_ref[0, pl.ds(ks, tk), :]))
        return carry

    lax.fori_loop(0, qi, score_step, 0)
    q0 = pl.multiple_of(qi * tk, tk)
    store_keys(qi, jnp.where(admissible(q0, tk), keys_of(ki_ref[0, pl.ds(q0, tk), :]), INT_MIN))
    tkey = jnp.where(admissible(t0, TAIL) & has_tail, keys_of(ki_ref[0, pl.ds(t0, TAIL), :]), INT_MIN)

    one_b = jnp.ones((), BF16)
    zero_s = jnp.zeros((), BF16)

    def count_ge(x16, s16):
        w = jnp.where(x16 >= s16, one_b, zero_s)
        w3 = w.reshape(x16.shape[0] // 16, 16, tq)
        return functools.reduce(lambda a, b: a + b, [w3[r] for r in range(w3.shape[0])]).astype(F32)

    def bisect16(ref, tail16, need):
        def bit_step(i, u):
            uc = u | lax.shift_left(jnp.int32(1), 15 - i)
            s16 = (uc - HALF_BIAS).astype(I16)

            def cnt_step(g, acc):
                return acc + count_ge(ref[g], s16)

            acc = lax.fori_loop(0, ngrp, cnt_step, count_ge(tail16, s16))
            cnt = jnp.sum(acc, axis=0, keepdims=True)
            return jnp.where(cnt >= need, uc, u)

        return lax.fori_loop(0, 16, bit_step, jnp.zeros((1, tq), I32))

    thi = tkey >> 16
    tlo = (tkey & (HALF - 1)) - HALF_BIAS
    u_hi = bisect16(hi_ref, thi.astype(I16), kf)
    t_hi = u_hi - HALF_BIAS
    above16 = (t_hi + 1).astype(I16)

    def above_step(g, acc):
        return acc + count_ge(hi_ref[g], above16)

    n_above = jnp.sum(lax.fori_loop(0, ngrp, above_step, count_ge(thi.astype(I16), above16)), axis=0, keepdims=True)
    n_above = jnp.where(t_hi == HALF_BIAS - 1, 0.0, n_above)
    t_hi16 = t_hi.astype(I16)

    def narrow_step(g, carry):
        lo_ref[g] = jnp.where(hi_ref[g] == t_hi16, lo_ref[g], jnp.int16(-HALF_BIAS))
        return carry

    lax.fori_loop(0, ngrp, narrow_step, 0)
    tlo16 = jnp.where(thi == t_hi, tlo, -HALF_BIAS).astype(I16)
    u_lo = bisect16(lo_ref, tlo16, kf - n_above)
    thr = t_hi * HALF + u_lo

    def gt_step(g, acc):
        return acc + fold8(jnp.where(key_ref[g] > thr, 1.0, 0.0))

    ngt = jnp.sum(lax.fori_loop(0, ngrp, gt_step, fold8(jnp.where(tkey > thr, 1.0, 0.0))),
                  axis=0, keepdims=True)
    room = kf - ngt
    incl = jnp.where(lax.broadcasted_iota(I32, (tk, tk), 1) <= lax.broadcasted_iota(I32, (tk, tk), 0),
                     1.0, 0.0).astype(BF16)
    incl_tail = jnp.where(lax.broadcasted_iota(I32, (TAIL, TAIL), 1) <= lax.broadcasted_iota(I32, (TAIL, TAIL), 0),
                          1.0, 0.0).astype(BF16)

    room = jnp.where(thr == INT_MIN, 0.0, room)

    def bias_of(key, seen):
        rows = key.shape[0]
        eq = key == thr
        eqf = jnp.where(eq, 1.0, 0.0)
        rank = jnp.dot(incl if rows == tk else incl_tail, eqf.astype(BF16), preferred_element_type=F32) + seen
        tie = jnp.where(eq, jnp.where(rank <= room, 0.0, -jnp.inf), -jnp.inf)
        return jnp.where(key > thr, 0.0, tie), seen + jnp.sum(fold8(eqf), axis=0, keepdims=True)

    def mask_step(g, seen):
        bias_ref[g], seen = bias_of(key_ref[g], seen)
        return seen

    seen = lax.fori_loop(0, ngrp, mask_step, jnp.zeros((1, tq), F32))
    tbias, _ = bias_of(tkey, seen)

    for h in range(N_HEADS):
        dq2 = dq_ref[0, :, LANES * (h // 2):LANES * (h // 2 + 1)]
        ql = jnp.dot(dq2, wuk_ref[h], preferred_element_type=F32) * (LOG2E * HEAD_DIM ** -0.5)
        qlat_ref[tq * h:tq * (h + 1), :] = ql.astype(BF16)
    acc_ref[...] = jnp.zeros_like(acc_ref)
    m_ref[...] = jnp.full(m_ref.shape, NEG_BIG, F32)

    def attend(ck, cx, bias):
        st_all = lax.dot_general(ck, qlat_ref[...], nt, preferred_element_type=F32)
        ps, als = [], []
        for h in range(N_HEADS):
            st = st_all[:, tq * h:tq * (h + 1)] + bias
            m_old = m_ref[h]
            mn = jnp.maximum(m_old, jnp.max(st, axis=0, keepdims=True))
            ps.append(jnp.exp2(st - mn).astype(BF16))
            als.append(jnp.exp2(m_old - mn))
            m_ref[h] = mn
        upd = jnp.dot(cx, jnp.concatenate(ps, axis=1), preferred_element_type=F32)
        acc_ref[...] = jnp.concatenate(als, axis=1) * acc_ref[...] + upd

    def att_step(g, carry):
        ks = pl.multiple_of(g * tk, tk)
        cx = jnp.concatenate([ct_ref[0, g * sub + j] for j in range(sub)], axis=1)
        attend(c_ref[0, pl.ds(ks, tk), :], cx, bias_ref[g])
        return carry

    lax.fori_loop(0, ngrp, att_step, 0)
    attend(c_ref[0, pl.ds(t0, TAIL), :], ct_ref[0, tblk][:, :TAIL], tbias)

    for pr in range(N_HEADS // 2):
        outs = []
        for hh in range(2):
            a = acc_ref[:, tq * (2 * pr + hh):tq * (2 * pr + hh + 1)]
            olat = (a[:DSA_LATENT] / a[DSA_LATENT:DSA_LATENT + 1]).astype(BF16)
            outs.append(jnp.dot(wuvt_ref[2 * pr + hh], olat, preferred_element_type=F32))
        o_ref[0, :, LANES * pr:LANES * (pr + 1)] = jnp.concatenate(outs, axis=0).T.astype(BF16)


def _dsa(ub, c, ct, rowt, wuk_pad, wuv_t, *, topk, sub):
    b, lp, _ = ub.shape
    tq = sub * SEQ_TILE
    ngrp = lp // tq
    nblk = lp // SEQ_TILE
    return pl.pallas_call(
        functools.partial(_dsa_body, topk=topk, sub=sub),
        grid=(b, ngrp),
        in_specs=[
            pl.BlockSpec((1, tq, GROUP_WIDTH), lambda i, q: (i, q, B_DQ // GROUP_WIDTH)),
            pl.BlockSpec((1, tq, IDX_HEADS * IDX_DIM), lambda i, q: (i, q, B_IQ // (IDX_HEADS * IDX_DIM))),
            pl.BlockSpec((1, lp, LANES), lambda i, q: (i, 0, B_IK // LANES)),
            pl.BlockSpec((1, lp, DSA_LATENT), lambda i, q: (i, 0, 0)),
            pl.BlockSpec((1, nblk, CT_ROWS, SEQ_TILE), lambda i, q: (i, 0, 0, 0)),
            pl.BlockSpec((1, sub, N_HEADS, SEQ_TILE), lambda i, q: (i, q, SM_W // N_HEADS, 0)),
            pl.BlockSpec((N_HEADS, LANES, DSA_LATENT), lambda i, q: (0, 0, 0)),
            pl.BlockSpec((N_HEADS, HEAD_DIM, DSA_LATENT), lambda i, q: (0, 0, 0)),
        ],
        out_specs=pl.BlockSpec((1, tq, GROUP_WIDTH), lambda i, q: (i, q, 0)),
        out_shape=jax.ShapeDtypeStruct((b, lp, GROUP_WIDTH), BF16),
        scratch_shapes=[
            pltpu.VMEM((ngrp, tq, tq), I32),
            pltpu.VMEM((ngrp, tq, tq), F32),
            pltpu.VMEM((ngrp, tq, tq), I16),
            pltpu.VMEM((ngrp, tq, tq), I16),
            pltpu.VMEM((IDX_HEADS * tq, LANES), BF16),
            pltpu.VMEM((N_HEADS * tq, DSA_LATENT), BF16),
            pltpu.VMEM((CT_ROWS, N_HEADS * tq), F32),
            pltpu.VMEM((N_HEADS, 1, tq), F32),
        ],
        compiler_params=pltpu.CompilerParams(
            dimension_semantics=("parallel", "arbitrary"), vmem_limit_bytes=_vmem(48)),
        name="dsa",
    )(ub, ub, ub, c, ct, rowt, wuk_pad, wuv_t)


def _outproj_body(ya_ref, yb_ref, yc_ref, yd_ref, x_ref, w_ref, g_ref, o_ref):
    acc = jnp.dot(ya_ref[...], w_ref[0], preferred_element_type=F32)
    acc = acc + jnp.dot(yb_ref[...], w_ref[1], preferred_element_type=F32)
    acc = acc + jnp.dot(yc_ref[...], w_ref[2], preferred_element_type=F32)
    acc = acc + jnp.dot(yd_ref[...], w_ref[3], preferred_element_type=F32)
    ms = jnp.mean(acc * acc, axis=-1, keepdims=True)
    o_ref[...] = x_ref[...] + acc * lax.rsqrt(ms + EPS) * g_ref[...]


def _out_proj(ys, x2d, w4, g, *, layer, tm):
    m, d = x2d.shape
    gw = GROUP_WIDTH
    yspec = pl.BlockSpec((tm, gw), lambda i: (i, 0))
    return pl.pallas_call(
        _outproj_body,
        grid=(m // tm,),
        in_specs=[yspec, yspec, yspec, yspec,
                  pl.BlockSpec((tm, d), lambda i: (i, 0)),
                  pl.BlockSpec((None, 4, gw, d), lambda i: (layer, 0, 0, 0)),
                  pl.BlockSpec((1, d), lambda i: (0, 0))],
        out_specs=pl.BlockSpec((tm, d), lambda i: (i, 0)),
        out_shape=jax.ShapeDtypeStruct((m, d), F32),
        compiler_params=pltpu.CompilerParams(
            dimension_semantics=("parallel",), vmem_limit_bytes=_vmem(48)),
        name="out_proj",
    )(*ys, x2d, w4, g)


def _ffn_body(x_ref, gpre_ref, wg_ref, wu_ref, cw_ref, cb_ref, wd_ref, gpost_ref, o_ref,
              xn_ref, gbuf_ref, carry_ref, *, tiles_per_seq):
    i = pl.program_id(0)
    f = pl.program_id(1)
    nf = pl.num_programs(1)
    tm = x_ref.shape[0]

    @pl.when(f == 0)
    def _():
        x = x_ref[...]
        ms = jnp.mean(x * x, axis=-1, keepdims=True)
        xn_ref[...] = (x * lax.rsqrt(ms + EPS) * gpre_ref[...]).astype(BF16)
        o_ref[...] = jnp.zeros_like(o_ref)

    @pl.when(i % tiles_per_seq == 0)
    def _():
        carry_ref[f] = jnp.zeros(carry_ref.shape[1:], F32)

    xn = xn_ref[...]
    g = jnp.dot(xn, wg_ref[...], preferred_element_type=F32)
    u = jnp.dot(xn, wu_ref[...], preferred_element_type=F32)
    gbuf_ref[0:SUBLANES, :] = carry_ref[f]
    gbuf_ref[SUBLANES:SUBLANES + tm, :] = g
    conv = cb_ref[...] + cw_ref[FFN_CONV - 1:FFN_CONV, :] * g
    for k in range(FFN_CONV - 1):
        lag = SUBLANES - (FFN_CONV - 1 - k)
        conv = conv + cw_ref[k:k + 1, :] * gbuf_ref[lag:lag + tm, :]
    carry_ref[f] = g[tm - SUBLANES:tm, :]
    a = (_silu(conv) * u).astype(BF16)
    o_ref[...] += jnp.dot(a, wd_ref[...], preferred_element_type=F32)

    @pl.when(f == nf - 1)
    def _():
        y = o_ref[...]
        ms = jnp.mean(y * y, axis=-1, keepdims=True)
        o_ref[...] = x_ref[...] + y * lax.rsqrt(ms + EPS) * gpost_ref[...]


def _ffn(x2d, g_pre, w_gate, w_up, conv_w, conv_b, w_down, g_post, *, layer, tm, tf, tiles_per_seq):
    m, d = x2d.shape
    fdim = w_gate.shape[-1]
    nf = fdim // tf
    return pl.pallas_call(
        functools.partial(_ffn_body, tiles_per_seq=tiles_per_seq),
        grid=(m // tm, nf),
        in_specs=[
            pl.BlockSpec((tm, d), lambda i, f: (i, 0)),
            pl.BlockSpec((1, d), lambda i, f: (0, 0)),
            pl.BlockSpec((None, d, tf), lambda i, f: (layer, 0, f)),
            pl.BlockSpec((None, d, tf), lambda i, f: (layer, 0, f)),
            pl.BlockSpec((FFN_CONV, tf), lambda i, f: (0, f)),
            pl.BlockSpec((1, tf), lambda i, f: (0, f)),
            pl.BlockSpec((None, tf, d), lambda i, f: (layer, f, 0)),
            pl.BlockSpec((1, d), lambda i, f: (0, 0)),
        ],
        out_specs=pl.BlockSpec((tm, d), lambda i, f: (i, 0), pipeline_mode=pl.Buffered(1)),
        out_shape=jax.ShapeDtypeStruct((m, d), F32),
        scratch_shapes=[
            pltpu.VMEM((tm, d), BF16),
            pltpu.VMEM((SUBLANES + tm, tf), F32),
            pltpu.VMEM((nf, SUBLANES, tf), F32),
        ],
        compiler_params=pltpu.CompilerParams(
            dimension_semantics=("arbitrary", "arbitrary"), vmem_limit_bytes=_vmem(60)),
        name="ffn",
    )(x2d, g_pre, w_gate, w_up, conv_w, conv_b, w_down, g_post)


def _permute_w_in(w_in):
    gw = GROUP_WIDTH
    sizes = (gw, SSD_XBC, N_HEADS, gw, 3 * gw, N_HEADS, gw, DSA_LATENT, IDX_HEADS * IDX_DIM, IDX_DIM, IDX_HEADS)
    offs = [0]
    for s in sizes:
        offs.append(offs[-1] + s)
    z, xbc, dt, pool, qkv, fl, dq, dc, dqi, dki, dwi = (w_in[..., offs[k]:offs[k + 1]] for k in range(len(sizes)))
    qkv = jnp.concatenate([qkv[..., :gw] * (LOG2E * HEAD_DIM ** -0.5), qkv[..., gw:]], axis=-1)
    zeros = lambda n: jnp.zeros(w_in.shape[:-1] + (n,), w_in.dtype)
    small = jnp.concatenate([dt, fl, dwi, zeros(SM_A - SM_W - IDX_HEADS), dt, zeros(LANES - SM_A - N_HEADS)], axis=-1)
    a = jnp.concatenate([xbc, z, pool, dc, small, zeros(A_COLS - 2 * gw - SSD_XBC - DSA_LATENT - LANES)], axis=-1)
    bcols = jnp.concatenate([qkv, dq, dqi, dki, dki], axis=-1)
    bcols = jnp.concatenate([bcols, zeros(B_COLS - bcols.shape[-1])], axis=-1)
    return jnp.concatenate([a, bcols], axis=-1)


def _lane_vec(pieces):
    v = jnp.zeros((LANES,), F32)
    for off, val in pieces:
        v = v.at[off:off + val.shape[0]].set(val.astype(F32))
    return v[None, :]


def _pad_head_weights(w_uk, w_uv):
    h, r, d = w_uk.shape
    uk = jnp.zeros((h, 2 * d, r), F32)
    for i in range(h):
        o = d * (i % 2)
        uk = uk.at[i, o:o + d, :].set(w_uk[i].T)
    return uk.astype(BF16), jnp.swapaxes(w_uv, 1, 2).astype(BF16)


def _tile_sizes(b, lp):
    m = b * lp
    tm_proj = next(t for t in (1024, 512, 256, 128) if m % t == 0)
    tm_out = next(t for t in (512, 256, 128) if m % t == 0)
    tm_ffn = next(t for t in (1056, 528, 384, 320, 256, 128) if lp % t == 0)
    att_sub = 3 if lp % (3 * SEQ_TILE) == 0 else 1
    tn_proj = 1280
    tf_ffn = 512
    return tm_proj, tn_proj, tm_out, tm_ffn, tf_ffn, att_sub


def _layer(h, p, big, layer, *, topk):
    b, lp, d = h.shape
    m = b * lp
    tm_proj, tn_proj, tm_out, tm_ffn, tf_ffn, att_sub = _tile_sizes(b, lp)
    row = lambda v: v.astype(F32)[None, :]

    ua, ub = _in_proj(h.reshape(m, d), row(p["norm_mix_pre"]), big["w_in"], layer=layer, tm=tm_proj, tn=tn_proj)
    ua = ua.reshape(b, lp, A_COLS)
    ub = ub.reshape(b, lp, B_COLS)

    bias_vec = _lane_vec([(SM_DT, p["ssd_dt_bias"]), (SM_F, p["fox_f_bias"]), (SM_A, p["ssd_dt_bias"])])
    wscale = jnp.full((IDX_HEADS,), (IDX_HEADS ** -0.5) * (IDX_DIM ** -0.5), F32)
    mul_vec = _lane_vec([(SM_W, wscale), (SM_A, -jnp.exp(p["ssd_a_log"].astype(F32)))])
    col, rowt, c, ct, fkp, y_b = _prep_pool(ua, bias_vec, mul_vec, row(p["dsa_kv_norm"]),
                                            p["pool_w"].astype(BF16), row(p["pool_scale"]), sub=att_sub)

    dskip_full = jnp.repeat(p["ssd_d"].astype(F32), HEAD_DIM)[None, :]
    y_a = _ssd(ua, col, rowt, p["ssd_conv_w"].astype(F32), row(p["ssd_conv_b"]), dskip_full, row(p["ssd_norm"]),
               sub=att_sub)
    y_c = _fox(ub, fkp, sub=att_sub)
    wuk_pad, wuv_t = _pad_head_weights(p["dsa_w_uk"], p["dsa_w_uv"])
    y_d = _dsa(ub, c, ct, rowt, wuk_pad, wuv_t, topk=topk, sub=att_sub)

    ys = [y.reshape(m, GROUP_WIDTH) for y in (y_a, y_b, y_c, y_d)]
    x1 = _out_proj(ys, h.reshape(m, d), big["w_out"], row(p["norm_mix_post"]), layer=layer, tm=tm_out)
    x2 = _ffn(x1, row(p["norm_ffn_pre"]), big["ffn_w_gate"], big["ffn_w_up"],
              p["ffn_conv_w"].astype(F32), row(p["ffn_conv_b"]), big["ffn_w_down"],
              row(p["norm_ffn_post"]), layer=layer, tm=tm_ffn, tf=tf_ffn, tiles_per_seq=lp // tm_ffn)
    return x2.reshape(b, lp, d)


def _stack_big_weights(w_in, w_out, ffn_w_gate, ffn_w_up, ffn_w_down):
    depth, d, _ = w_out.shape
    return dict(w_in=_to_bf16(_permute_w_in(w_in)),
                w_out=_to_bf16(w_out).reshape(depth, 4, GROUP_WIDTH, d),
                ffn_w_gate=_to_bf16(ffn_w_gate), ffn_w_up=_to_bf16(ffn_w_up), ffn_w_down=_to_bf16(ffn_w_down))


def kernel(x, meta_tokens, norm_mix_pre, norm_mix_post, norm_ffn_pre, norm_ffn_post, w_in, ssd_conv_w, ssd_conv_b, ssd_dt_bias, ssd_a_log, ssd_d, ssd_norm, pool_w, pool_scale, fox_f_bias, dsa_kv_norm, dsa_w_uk, dsa_w_uv, w_out, ffn_w_gate, ffn_w_up, ffn_conv_w, ffn_conv_b, ffn_w_down):
    bsz, seq, d = x.shape
    n = N_META + seq
    lp = -(-n // SEQ_TILE) * SEQ_TILE
    topk = min(DSA_TOPK_MAX, seq // 4)
    meta = jnp.broadcast_to(meta_tokens.astype(x.dtype)[None], (bsz, N_META, d))
    h = jnp.concatenate([meta, x, jnp.zeros((bsz, lp - n, d), x.dtype)], axis=1)
    small = dict(norm_mix_pre=norm_mix_pre, norm_mix_post=norm_mix_post, norm_ffn_pre=norm_ffn_pre,
                 norm_ffn_post=norm_ffn_post, ssd_conv_w=ssd_conv_w, ssd_conv_b=ssd_conv_b,
                 ssd_dt_bias=ssd_dt_bias, ssd_a_log=ssd_a_log, ssd_d=ssd_d, ssd_norm=ssd_norm,
                 pool_w=pool_w, pool_scale=pool_scale, fox_f_bias=fox_f_bias, dsa_kv_norm=dsa_kv_norm,
                 dsa_w_uk=dsa_w_uk, dsa_w_uv=dsa_w_uv, ffn_conv_w=ffn_conv_w, ffn_conv_b=ffn_conv_b)
    big = _stack_big_weights(w_in, w_out, ffn_w_gate, ffn_w_up, ffn_w_down)
    for i in range(norm_mix_pre.shape[0]):
        h = _layer(h, {k: v[i] for k, v in small.items()}, big, i, topk=topk)
    return h[:, N_META:n]
```
